```python
import jax
import jax.numpy as jnp
from jax import lax
import numpy as np

D_MODEL = 1024
BATCH = 8
SEQ = 2048
DEPTH = 2

MEM_LEN = 256
HEAD_DIM = 64
BRANCH_WIDTH = D_MODEL // 2
N_BRANCH = 3
RW_HEADS = BRANCH_WIDTH // HEAD_DIM
RW_WIDTH = RW_HEADS * HEAD_DIM
RW_DECAY_LORA = 64
RW_AAA_LORA = 64
RW_MV_LORA = 32
RW_GATE_LORA = 128
RW_LNX_EPS = 1e-5 * HEAD_DIM
RW_COLS = 3 * RW_WIDTH + RW_DECAY_LORA + RW_AAA_LORA + RW_GATE_LORA
RW_SPLITS = [RW_WIDTH, 2 * RW_WIDTH, 3 * RW_WIDTH, 3 * RW_WIDTH + RW_DECAY_LORA, 3 * RW_WIDTH + RW_DECAY_LORA + RW_AAA_LORA]
MB_HEADS = BRANCH_WIDTH // HEAD_DIM
MB_WIDTH = MB_HEADS * HEAD_DIM
MB_BLOCK = 256
MB_TOPK = 3
MB_QCHUNK = 32
MB_COLS = 3 * MB_WIDTH
SW_Q_HEADS = BRANCH_WIDTH // HEAD_DIM
SW_KV_HEADS = SW_Q_HEADS // 4
SW_WIDTH = SW_Q_HEADS * HEAD_DIM
SW_KV_WIDTH = SW_KV_HEADS * HEAD_DIM
SW_WINDOW = 128
SW_COLS = SW_WIDTH + 2 * SW_KV_WIDTH
GATE_COLS = N_BRANCH * D_MODEL
IN_COLS = RW_COLS + MB_COLS + SW_COLS + GATE_COLS
XA_HEADS = 4
XA_HEAD_DIM = 128
XA_WIDTH = XA_HEADS * XA_HEAD_DIM
D_FF = 4 * D_MODEL
ROPE_THETA = 10000.0
NORM_EPS = 1e-6
MASK_VALUE = -1e30

kernel_name = 'hybrid_rwkv7_moba_swa_gated_block'


def rms_norm(x, g):
    xf = x.astype(jnp.float32)
    y = xf * lax.rsqrt(jnp.mean(jnp.square(xf), axis=-1, keepdims=True) + NORM_EPS)
    return (y * g.astype(jnp.float32)).astype(x.dtype)


def rope_tables(positions):
    inv_freq = 1.0 / (ROPE_THETA ** (jnp.arange(0, HEAD_DIM, 2, dtype=jnp.float32) / HEAD_DIM))
    ang = positions.astype(jnp.float32)[..., None] * inv_freq
    return jnp.cos(ang), jnp.sin(ang)


def apply_rope(t, cos, sin):
    tf = t.astype(jnp.float32)
    t1, t2 = jnp.split(tf, 2, axis=-1)
    c = cos[:, :, None, :]
    s = sin[:, :, None, :]
    return jnp.concatenate([t1 * c - t2 * s, t2 * c + t1 * s], axis=-1).astype(t.dtype)


def token_shift_lerp(u, mu):
    prev = jnp.pad(u, ((0, 0), (1, 0), (0, 0)))[:, :-1]
    return u + mu * (prev - u)


def rwkv7_scan(r, w, k, v, kk, a):
    B, S, H, N = r.shape

    def step(state, inp):
        r_t, w_t, k_t, v_t, kk_t, a_t = inp
        sa = jnp.einsum('bhvk,bhk->bhv', state, -kk_t)
        state = (state * w_t[:, :, None, :]
                 + sa[..., :, None] * (kk_t * a_t)[:, :, None, :]
                 + v_t[..., :, None] * k_t[:, :, None, :])
        return state, jnp.einsum('bhvk,bhk->bhv', state, r_t)

    xs = tuple(jnp.moveaxis(t, 1, 0) for t in (r, w, k, v, kk, a))
    _, out = lax.scan(step, jnp.zeros((B, H, N, N), jnp.float32), xs)
    return jnp.moveaxis(out, 0, 1)


def rwkv7_branch(u, mu, w0, w_up, a0, a_up, g_up, k_k, k_a, r_k, lnx_w, lnx_b, v_first, v_gate):
    B, S, _ = u.shape
    H, N = RW_HEADS, HEAD_DIM
    f32 = jnp.float32
    u = token_shift_lerp(u, mu)
    r, k, v, xw, xa, xg = jnp.split(u, RW_SPLITS, axis=-1)
    w_log = -jax.nn.softplus(-(w0 + jnp.tanh(xw) @ w_up).astype(f32)) - 0.5
    decay = jnp.exp(-jnp.exp(w_log))
    a = jax.nn.sigmoid((a0 + xa @ a_up).astype(f32))
    g = jax.nn.sigmoid(xg) @ g_up
    if v_gate is not None:
        v = v + (v_first - v) * v_gate
    kk = (k * k_k).astype(f32).reshape(B, S, H, N)
    kk = kk / jnp.maximum(jnp.sqrt(jnp.sum(kk * kk, axis=-1, keepdims=True)), 1e-12)
    k_eff = k.astype(f32) * (1.0 + (a - 1.0) * k_a)

    def heads(t):
        return t.astype(f32).reshape(B, S, H, N)

    rh, kh, vh, wh, ah = heads(r), heads(k_eff), heads(v), heads(decay), heads(a)
    out = rwkv7_scan(rh, wh, kh, vh, kk, ah)
    mean = jnp.mean(out, axis=-1, keepdims=True)
    var = jnp.mean(jnp.square(out - mean), axis=-1, keepdims=True)
    out = ((out - mean) * lax.rsqrt(var + RW_LNX_EPS)).reshape(B, S, RW_WIDTH) * lnx_w + lnx_b
    bonus = (jnp.sum(rh * kh * r_k, axis=-1, keepdims=True) * vh).reshape(B, S, RW_WIDTH)
    o = ((out + bonus) * g.astype(f32)).astype(u.dtype)
    return o, v


def moba_attention(q, k, v):
    B, S, H, hd = q.shape
    nb = -(-S // MB_BLOCK)
    s_pad = nb * MB_BLOCK
    pad = ((0, 0), (0, 0), (0, s_pad - S), (0, 0))
    qh = (q * hd ** -0.5).transpose(0, 2, 1, 3)
    kp = jnp.pad(k.transpose(0, 2, 1, 3), pad)
    vp = jnp.pad(v.transpose(0, 2, 1, 3), pad)
    kb = kp.reshape(B, H, nb, MB_BLOCK, hd)
    vb = vp.reshape(B, H, nb, MB_BLOCK, hd)
    k_mean = jnp.mean(kb.astype(jnp.float32), axis=3).astype(k.dtype)
    top_k = min(MB_TOPK, max(nb - 1, 1))
    b_idx = jnp.arange(B)[:, None, None, None]
    h_idx = jnp.arange(H)[None, :, None, None]
    blk_ids = jnp.arange(nb)
    own_offsets = jnp.arange(MB_BLOCK)

    def chunk(c):
        q0 = c * MB_QCHUNK
        own = q0 // MB_BLOCK
        qc = lax.dynamic_slice_in_dim(qh, q0, MB_QCHUNK, axis=2)
        q_pos = q0 + jnp.arange(MB_QCHUNK)
        gate = jnp.einsum('bhqd,bhnd->bhqn', qc, k_mean).astype(jnp.float32)
        gate = jnp.where(blk_ids < own, gate, MASK_VALUE)
        _, idx = lax.top_k(gate, top_k)
        sel_ok = idx < own
        k_sel = kb[b_idx, h_idx, idx]
        v_sel = vb[b_idx, h_idx, idx]
        s_sel = jnp.einsum('bhqd,bhqjkd->bhqjk', qc, k_sel).astype(jnp.float32)
        s_sel = jnp.where(sel_ok[..., None], s_sel, MASK_VALUE).reshape(B, H, MB_QCHUNK, top_k * MB_BLOCK)
        k_own = lax.dynamic_slice_in_dim(kp, own * MB_BLOCK, MB_BLOCK, axis=2)
        v_own = lax.dynamic_slice_in_dim(vp, own * MB_BLOCK, MB_BLOCK, axis=2)
        k_pos = own * MB_BLOCK + own_offsets
        s_own = jnp.einsum('bhqd,bhkd->bhqk', qc, k_own).astype(jnp.float32)
        s_own = jnp.where(k_pos[None, :] <= q_pos[:, None], s_own, MASK_VALUE)
        p = jax.nn.softmax(jnp.concatenate([s_sel, s_own], axis=-1), axis=-1).astype(v.dtype)
        p_sel = p[..., :top_k * MB_BLOCK].reshape(B, H, MB_QCHUNK, top_k, MB_BLOCK)
        p_own = p[..., top_k * MB_BLOCK:]
        return (jnp.einsum('bhqjk,bhqjkd->bhqd', p_sel, v_sel)
                + jnp.einsum('bhqk,bhkd->bhqd', p_own, v_own))

    out = lax.map(chunk, jnp.arange(S // MB_QCHUNK))
    return out.transpose(1, 0, 3, 2, 4).reshape(B, S, H * hd)


def swa_sink_attention(q, k, v, sinks):
    B, S, HQ, hd = q.shape
    HKV = k.shape[2]
    G = HQ // HKV
    W = SW_WINDOW
    nblk = S // W
    qb = (q * hd ** -0.5).reshape(B, nblk, W, HKV, G, hd)

    def with_prev(t):
        tb = t.reshape(B, nblk, W, HKV, hd)
        prev = jnp.pad(tb, ((0, 0), (1, 0), (0, 0), (0, 0), (0, 0)))[:, :-1]
        return jnp.concatenate([prev, tb], axis=2)

    kk = with_prev(k)
    vv = with_prev(v)
    s = jnp.einsum('bnqhgd,bnkhd->bnhgqk', qb, kk).astype(jnp.float32)
    q_rel = jnp.arange(W)[:, None] + W
    k_rel = jnp.arange(2 * W)[None, :]
    dist = q_rel - k_rel
    band = (dist >= 0) & (dist < W)
    has_prev = (jnp.arange(nblk) > 0)[:, None, None]
    mask = band[None] & (has_prev | (k_rel >= W)[None])
    s = jnp.where(mask[None, :, None, None], s, MASK_VALUE)
    sink = jnp.broadcast_to(sinks.astype(jnp.float32).reshape(HKV, G)[None, None, :, :, None, None],
                            s.shape[:-1] + (1,))
    p = jax.nn.softmax(jnp.concatenate([s, sink], axis=-1), axis=-1)[..., :-1].astype(v.dtype)
    o = jnp.einsum('bnhgqk,bnkhd->bnqhgd', p, vv)
    return o.reshape(B, S, HQ * hd)


def memory_cross_attention(h, mem_n, wq, wk, wv, wo):
    B, S, _ = h.shape
    M = mem_n.shape[1]
    q = (h @ wq).reshape(B, S, XA_HEADS, XA_HEAD_DIM) * XA_HEAD_DIM ** -0.5
    k = (mem_n @ wk).reshape(B, M, XA_HEADS, XA_HEAD_DIM)
    v = (mem_n @ wv).reshape(B, M, XA_HEADS, XA_HEAD_DIM)
    s = jnp.einsum('bshd,bmhd->bhsm', q, k).astype(jnp.float32)
    p = jax.nn.softmax(s, axis=-1).astype(v.dtype)
    o = jnp.einsum('bhsm,bmhd->bshd', p, v).reshape(B, S, XA_WIDTH)
    return o @ wo


def squared_relu_mlp(h, w_up, w_down):
    return jnp.square(jax.nn.relu(h @ w_up)) @ w_down


def setup_inputs(seed: int = 0) -> dict:
    key = jax.random.key(seed)
    keys = iter(jax.random.split(key, 48))
    L = DEPTH
    D = D_MODEL
    f32 = jnp.float32

    def dense(shape, fan_in, scale=1.0):
        return jax.random.normal(next(keys), shape, f32) * (scale * fan_in ** -0.5)

    def gain(shape):
        return 1.0 + 0.02 * jax.random.normal(next(keys), shape, f32)

    def noise(shape, scale):
        return scale * jax.random.normal(next(keys), shape, f32)

    def unif(shape, lo, hi):
        return jax.random.uniform(next(keys), shape, f32, lo, hi)

    x = jax.random.normal(next(keys), (BATCH, SEQ, D), f32)
    mem = jax.random.normal(next(keys), (BATCH, MEM_LEN, D), f32)
    offset = jax.random.randint(next(keys), (BATCH, 1), 0, 4096, jnp.int32)
    positions = (jnp.arange(SEQ, dtype=jnp.int32)[None, :] + offset).astype(jnp.int32)
    return {
        'x': x,
        'mem': mem,
        'positions': positions,
        'norm_mix_pre': gain((L, D)),
        'norm_mix_post': gain((L, D)),
        'norm_xattn_pre': gain((L, D)),
        'norm_xattn_post': gain((L, D)),
        'norm_mem': gain((L, D)),
        'norm_mlp_pre': gain((L, D)),
        'norm_mlp_post': gain((L, D)),
        'w_in': dense((L, D, IN_COLS), D),
        'rw_mu': unif((L, RW_COLS), 0.0, 1.0),
        'rw_w0': unif((L, RW_WIDTH), -6.0, -1.0),
        'rw_w_up': dense((L, RW_DECAY_LORA, RW_WIDTH), RW_DECAY_LORA, 0.5),
        'rw_a0': noise((L, RW_WIDTH), 0.1),
        'rw_a_up': dense((L, RW_AAA_LORA, RW_WIDTH), RW_AAA_LORA),
        'rw_g_up': dense((L, RW_GATE_LORA, RW_WIDTH), RW_GATE_LORA),
        'rw_k_k': 0.85 + noise((L, RW_WIDTH), 0.02),
        'rw_k_a': 1.0 + noise((L, RW_WIDTH), 0.02),
        'rw_r_k': noise((L, RW_HEADS, HEAD_DIM), 0.1),
        'rw_lnx_w': gain((L, RW_WIDTH)),
        'rw_lnx_b': noise((L, RW_WIDTH), 0.01),
        'rw_vres_down': dense((L - 1, D, RW_MV_LORA), D),
        'rw_vres_mu': unif((L - 1, RW_MV_LORA), 0.0, 1.0),
        'rw_v0': 1.0 + noise((L - 1, RW_WIDTH), 0.1),
        'rw_vres_up': dense((L - 1, RW_MV_LORA, RW_WIDTH), RW_MV_LORA),
        'sw_sinks': noise((L, SW_Q_HEADS), 1.0),
        'w_branch': dense((L, N_BRANCH, BRANCH_WIDTH, D), BRANCH_WIDTH),
        'w_out': dense((L, D, D), D),
        'w_xq': dense((L, D, XA_WIDTH), D),
        'w_xk': dense((L, D, XA_WIDTH), D),
        'w_xv': dense((L, D, XA_WIDTH), D),
        'w_xo': dense((L, XA_WIDTH, D), XA_WIDTH),
        'w_up': dense((L, D, D_FF), D),
        'w_down': dense((L, D_FF, D), D_FF),
    }


def reference(x, mem, positions, norm_mix_pre, norm_mix_post, norm_xattn_pre, norm_xattn_post,
              norm_mem, norm_mlp_pre, norm_mlp_post, w_in, rw_mu, rw_w0, rw_w_up, rw_a0, rw_a_up,
              rw_g_up, rw_k_k, rw_k_a, rw_r_k, rw_lnx_w, rw_lnx_b, rw_vres_down, rw_vres_mu, rw_v0,
              rw_vres_up, sw_sinks, w_branch, w_out, w_xq, w_xk, w_xv, w_xo, w_up, w_down):
    B, S, D = x.shape
    cos, sin = rope_tables(positions)
    col_splits = [RW_COLS, RW_COLS + MB_COLS, RW_COLS + MB_COLS + SW_COLS]
    v_first = None
    for l in range(DEPTH):
        h = rms_norm(x, norm_mix_pre[l])
        u = h @ w_in[l]
        u_rw, u_mb, u_sw, u_gate = jnp.split(u, col_splits, axis=-1)
        if l == 0:
            v_gate = None
        else:
            vm = token_shift_lerp(h @ rw_vres_down[l - 1], rw_vres_mu[l - 1])
            v_gate = jax.nn.sigmoid(rw_v0[l - 1] + vm @ rw_vres_up[l - 1])
        o_rw, v_rw = rwkv7_branch(u_rw, rw_mu[l], rw_w0[l], rw_w_up[l], rw_a0[l], rw_a_up[l],
                                  rw_g_up[l], rw_k_k[l], rw_k_a[l], rw_r_k[l], rw_lnx_w[l],
                                  rw_lnx_b[l], v_first, v_gate)
        if l == 0:
            v_first = v_rw
        mq, mk, mv = jnp.split(u_mb, 3, axis=-1)
        mq = apply_rope(mq.reshape(B, S, MB_HEADS, HEAD_DIM), cos, sin)
        mk = apply_rope(mk.reshape(B, S, MB_HEADS, HEAD_DIM), cos, sin)
        o_mb = moba_attention(mq, mk, mv.reshape(B, S, MB_HEADS, HEAD_DIM))
        sq, sk, sv = jnp.split(u_sw, [SW_WIDTH, SW_WIDTH + SW_KV_WIDTH], axis=-1)
        sq = apply_rope(sq.reshape(B, S, SW_Q_HEADS, HEAD_DIM), cos, sin)
        sk = apply_rope(sk.reshape(B, S, SW_KV_HEADS, HEAD_DIM), cos, sin)
        o_sw = swa_sink_attention(sq, sk, sv.reshape(B, S, SW_KV_HEADS, HEAD_DIM), sw_sinks[l])
        branches = jnp.stack([o_rw, o_mb, o_sw], axis=2)
        proj = jnp.einsum('bsnc,ncd->bsnd', branches, w_branch[l])
        gates = jax.nn.sigmoid(u_gate.reshape(B, S, N_BRANCH, D))
        y = jnp.sum(gates * proj, axis=2) @ w_out[l]
        x = x + rms_norm(y, norm_mix_post[l])
        h = rms_norm(x, norm_xattn_pre[l])
        m = rms_norm(mem, norm_mem[l])
        xa = memory_cross_attention(h, m, w_xq[l], w_xk[l], w_xv[l], w_xo[l])
        x = x + rms_norm(xa, norm_xattn_post[l])
        h = rms_norm(x, norm_mlp_pre[l])
        x = x + rms_norm(squared_relu_mlp(h, w_up[l], w_down[l]), norm_mlp_post[l])
    return x
```

```python
import functools

import jax
import jax.numpy as jnp
from jax import lax
from jax.experimental import pallas as pl
from jax.experimental.pallas import tpu as pltpu

F32 = jnp.float32
BF16 = jnp.bfloat16

D_MODEL = 1024
HEAD_DIM = 64
BRANCH = 512
N_HEADS = BRANCH // HEAD_DIM
RW_COLS = 3 * BRANCH + 64 + 64 + 128
LORA_PAD = 128
MB_BLOCK = 256
MB_TOPK = 3
SW_WINDOW = 128
SW_KV_HEADS = 2
XA_HEADS = 4
XA_HEAD_DIM = 128
XA_WIDTH = XA_HEADS * XA_HEAD_DIM
D_FF = 4 * D_MODEL
ROPE_THETA = 10000.0
NORM_EPS = 1e-6
RW_LNX_EPS = 1e-5 * HEAD_DIM
MASK_VALUE = -1e30

LANES = 128
RW_CHUNK = 64
VMEM_LIMIT = 56 * 1024 * 1024

HI = lax.Precision.HIGHEST
SCAN_PREC = lax.Precision.DEFAULT


def _cparams(*sem):
    return pltpu.CompilerParams(dimension_semantics=sem, vmem_limit_bytes=VMEM_LIMIT)


def _dot(a, b, precision=None):
    return jnp.dot(a, b, preferred_element_type=F32, precision=precision)


def _dot_nt(a, b, precision=None):
    return lax.dot_general(a, b, (((1,), (1,)), ((), ())),
                           preferred_element_type=F32, precision=precision)


def _dot_tn(a, b, precision=None):
    return lax.dot_general(a, b, (((0,), (0,)), ((), ())),
                           preferred_element_type=F32, precision=precision)


def _rms(xf, g):
    ms = jnp.mean(xf * xf, axis=-1, keepdims=True)
    return xf * lax.rsqrt(ms + NORM_EPS) * g


def _const_spec(shape):
    nd = len(shape)
    return pl.BlockSpec(shape, lambda *_: (0,) * nd, pipeline_mode=pl.Buffered(1))


def _rope_kernel(pos_ref, invf_ref, sign_ref, cos_ref, sin_ref):
    ang = pos_ref[...].astype(F32) * invf_ref[...]
    cos_ref[...] = jnp.cos(ang)
    sin_ref[...] = jnp.sin(ang) * sign_ref[...]


def _rope_tables(positions):
    T = positions.size
    tm = 2048
    inv_freq = 1.0 / (ROPE_THETA ** (jnp.arange(0, HEAD_DIM, 2, dtype=F32) / HEAD_DIM))
    invf = jnp.tile(inv_freq, LANES // (HEAD_DIM // 2))[None, :]
    half = jnp.concatenate([-jnp.ones((HEAD_DIM // 2,), F32), jnp.ones((HEAD_DIM // 2,), F32)])
    sign = jnp.tile(half, LANES // HEAD_DIM)[None, :]
    return pl.pallas_call(
        _rope_kernel,
        out_shape=(jax.ShapeDtypeStruct((T, LANES), F32),) * 2,
        grid=(T // tm,),
        in_specs=[pl.BlockSpec((tm, 1), lambda i: (i, 0)),
                  _const_spec((1, LANES)), _const_spec((1, LANES))],
        out_specs=(pl.BlockSpec((tm, LANES), lambda i: (i, 0)),) * 2,
        compiler_params=_cparams("parallel"),
        name="rope_tables",
    )(positions.reshape(T, 1), invf, sign)


def _rope_tile(t, cos, sin, lane_lt_half):
    fwd = pltpu.roll(t, LANES - HEAD_DIM // 2, axis=1)
    bwd = pltpu.roll(t, HEAD_DIM // 2, axis=1)
    partner = jnp.where(lane_lt_half, fwd, bwd)
    return t * cos + partner * sin


def _mix_in_kernel(has_vm, x_ref, g_ref, w_ref, cos_ref, sin_ref, *outs):
    if has_vm:
        urw_ref, vm_ref, mq_ref, mk_ref, mv_ref, sq_ref, sk_ref, sv_ref, gate_ref = outs
    else:
        urw_ref, mq_ref, mk_ref, mv_ref, sq_ref, sk_ref, sv_ref, gate_ref = outs
        vm_ref = None
    h = _rms(x_ref[...], g_ref[...]).astype(BF16)
    cos = cos_ref[...]
    sin = sin_ref[...]
    lane = lax.broadcasted_iota(jnp.int32, cos.shape, 1)
    lt_half = (lane % HEAD_DIM) < (HEAD_DIM // 2)

    col = [0]

    def seg(width):
        a = col[0]
        col[0] += width
        return _dot(h, w_ref[:, a:a + width])

    def rope_store(ref, width, scale):
        acc = seg(width)
        for c in range(width // LANES):
            t = _rope_tile(acc[:, c * LANES:(c + 1) * LANES], cos, sin, lt_half)
            ref[:, c * LANES:(c + 1) * LANES] = (t * scale).astype(ref.dtype)

    urw_ref[...] = seg(RW_COLS)
    if has_vm:
        vm_ref[...] = seg(LORA_PAD)
    rope_store(mq_ref, BRANCH, HEAD_DIM ** -0.5)
    rope_store(mk_ref, BRANCH, 1.0)
    mv_ref[...] = seg(BRANCH).astype(mv_ref.dtype)
    rope_store(sq_ref, BRANCH, HEAD_DIM ** -0.5)
    rope_store(sk_ref, SW_KV_HEADS * HEAD_DIM, 1.0)
    sv_ref[...] = seg(SW_KV_HEADS * HEAD_DIM).astype(sv_ref.dtype)
    for n in range(3):
        gate_ref[:, n * D_MODEL:(n + 1) * D_MODEL] = jax.nn.sigmoid(seg(D_MODEL))


def _mix_in(x2, g, w_cat, cos_t, sin_t, has_vm):
    T = x2.shape[0]
    tm = 256
    ntot = w_cat.shape[1]
    kv_w = SW_KV_HEADS * HEAD_DIM
    widths = [(RW_COLS, F32)]
    if has_vm:
        widths.append((LORA_PAD, F32))
    widths += [(BRANCH, BF16), (BRANCH, BF16), (BRANCH, BF16), (BRANCH, BF16),
               (kv_w, BF16), (kv_w, BF16), (3 * D_MODEL, F32)]
    assert sum(w for w, _ in widths) == ntot
    row = lambda i: (i, 0)
    return pl.pallas_call(
        functools.partial(_mix_in_kernel, has_vm),
        out_shape=tuple(jax.ShapeDtypeStruct((T, w), dt) for w, dt in widths),
        grid=(T // tm,),
        in_specs=[pl.BlockSpec((tm, D_MODEL), row), _const_spec((1, D_MODEL)),
                  _const_spec((D_MODEL, ntot)),
                  pl.BlockSpec((tm, LANES), row), pl.BlockSpec((tm, LANES), row)],
        out_specs=tuple(pl.BlockSpec((tm, w), row) for w, _ in widths),
        compiler_params=_cparams("parallel"),
        name="mix_in",
    )(x2, g, w_cat, cos_t, sin_t)


def _shift_prev(cur, carry_ref):
    rolled = pltpu.roll(cur, 1, axis=0)
    row = lax.broadcasted_iota(jnp.int32, cur.shape, 0)
    prev = jnp.where(row == 0, carry_ref[0:1, :], rolled)
    carry_ref[0:1, :] = cur[cur.shape[0] - 1:, :]
    return prev


def _rwkv_kernel(has_vres, *refs):
    if has_vres:
        (u_ref, vm_ref, vf_ref, mu_ref, w0_ref, wup_ref, a0_ref, aup_ref, gup_ref, kk_ref,
         ka_ref, rk_ref, lw_ref, lb_ref, bd_ref, vmu_ref, v0_ref, vup_ref,
         o_ref, s_ref, cu_ref, cvm_ref, hout_ref) = refs
        vout_ref = None
    else:
        (u_ref, mu_ref, w0_ref, wup_ref, a0_ref, aup_ref, gup_ref, kk_ref,
         ka_ref, rk_ref, lw_ref, lb_ref, bd_ref,
         o_ref, vout_ref, s_ref, cu_ref, hout_ref) = refs
    C = RW_CHUNK
    N = HEAD_DIM
    first = pl.program_id(1) == 0

    @pl.when(first)
    def _():
        s_ref[...] = jnp.zeros_like(s_ref)
        cu_ref[...] = jnp.zeros_like(cu_ref)
        if has_vres:
            cvm_ref[...] = jnp.zeros_like(cvm_ref)

    u = u_ref[...]
    ul = u + mu_ref[...] * (_shift_prev(u, cu_ref) - u)
    r = ul[:, 0:BRANCH]
    k = ul[:, BRANCH:2 * BRANCH]
    v = ul[:, 2 * BRANCH:3 * BRANCH]
    x_wa = ul[:, 3 * BRANCH:3 * BRANCH + LANES]
    x_g = ul[:, 3 * BRANCH + LANES:]

    z = w0_ref[...] + _dot(jnp.tanh(x_wa).astype(BF16), wup_ref[...])
    a = jax.nn.sigmoid(a0_ref[...] + _dot(x_wa.astype(BF16), aup_ref[...]))
    g = _dot(jax.nn.sigmoid(x_g).astype(BF16), gup_ref[...])
    softplus = jnp.maximum(-z, 0.0) + jnp.log1p(jnp.exp(-jnp.abs(z)))
    ell = -jnp.exp(-softplus - 0.5)

    if has_vres:
        vm = vm_ref[...]
        vml = vm + vmu_ref[...] * (_shift_prev(vm, cvm_ref) - vm)
        vgate = jax.nn.sigmoid(v0_ref[...] + _dot(vml.astype(BF16), vup_ref[...]))
        v = v + (vf_ref[...] - v) * vgate
    else:
        vout_ref[...] = v

    bd = bd_ref[...]
    kk = k * kk_ref[...]
    kk = kk / jnp.maximum(jnp.sqrt(_dot(kk * kk, bd, HI)), 1e-12)
    k_eff = k * (1.0 + (a - 1.0) * ka_ref[...])
    b = kk * a

    ti = lax.broadcasted_iota(jnp.int32, (C, C), 0)
    si = lax.broadcasted_iota(jnp.int32, (C, C), 1)
    strict = si < ti
    incl = si <= ti
    cum = _dot(incl.astype(F32), ell, HI)
    cum_last = cum[C - 1:, :]
    g_in = jnp.exp(cum)
    g_inv = jnp.exp(-cum)
    g_tail = jnp.exp(cum_last - cum)
    kap_t = kk * jnp.exp(cum - ell)
    r_t = r * g_in
    b_t = b * g_inv
    k_t = k_eff * g_inv
    b_g = b * g_tail
    k_g = k_eff * g_tail
    g_all = jnp.exp(cum_last)
    eye = (si == ti).astype(F32)

    mm = functools.partial(_dot, precision=SCAN_PREC)
    mm_nt = functools.partial(_dot_nt, precision=SCAN_PREC)
    mm_tn = functools.partial(_dot_tn, precision=SCAN_PREC)

    for h in range(N_HEADS):
        sl = slice(h * N, (h + 1) * N)
        kap_h, r_h, b_h, k_h, v_h = kap_t[:, sl], r_t[:, sl], b_t[:, sl], k_t[:, sl], v[:, sl]
        a_ab = mm_nt(kap_h, b_h)
        a_ak = mm_nt(kap_h, k_h)
        a_rb = mm_nt(r_h, b_h)
        a_rk = mm_nt(r_h, k_h)
        npow = jnp.where(strict, -a_ab, 0.0)
        tinv = eye + npow
        for _ in range(5):
            npow = mm(npow, npow)
            tinv = tinv + mm(tinv, npow)
        akv = mm(jnp.where(strict, a_ak, 0.0), v_h)
        w_til = mm(tinv, kap_h)
        u_til = mm(tinv, akv)
        state = s_ref[h]
        sa = -(mm_nt(w_til, state) + u_til)
        out_h = (mm_nt(r_h, state) + mm(jnp.where(incl, a_rb, 0.0), sa)
                 + mm(jnp.where(incl, a_rk, 0.0), v_h))
        s_ref[h] = state * g_all[:, sl] + mm_tn(sa, b_g[:, sl]) + mm_tn(v_h, k_g[:, sl])
        hout_ref[:, sl] = out_h

    out = hout_ref[...]
    inv_n = 1.0 / N
    mean = _dot(out, bd, HI) * inv_n
    cen = out - mean
    var = _dot(cen * cen, bd, HI) * inv_n
    normed = cen * lax.rsqrt(var + RW_LNX_EPS) * lw_ref[...] + lb_ref[...]
    bonus = _dot(r * k_eff * rk_ref[...], bd, HI) * v
    o_ref[...] = ((normed + bonus) * g).astype(o_ref.dtype)


def _rwkv(u_rw, p, B, S, vres=None):
    T = u_rw.shape[0]
    C = RW_CHUNK
    nc = S // C
    has_vres = vres is not None
    row = lambda b, c: (b * nc + c, 0)
    vec = lambda n: _const_spec((1, n))
    in_specs = [pl.BlockSpec((C, RW_COLS), row)]
    args = [u_rw]
    if has_vres:
        in_specs += [pl.BlockSpec((C, LORA_PAD), row), pl.BlockSpec((C, BRANCH), row)]
        args += [vres["vm"], vres["v_first"]]
    in_specs += [vec(RW_COLS), vec(BRANCH), _const_spec((LORA_PAD, BRANCH)), vec(BRANCH),
                 _const_spec((LORA_PAD, BRANCH)), _const_spec((LORA_PAD, BRANCH)),
                 vec(BRANCH), vec(BRANCH), vec(BRANCH), vec(BRANCH), vec(BRANCH),
                 _const_spec((BRANCH, BRANCH))]
    args += [p["mu"], p["w0"], p["w_up"], p["a0"], p["a_up"], p["g_up"], p["k_k"], p["k_a"],
             p["r_k"], p["lnx_w"], p["lnx_b"], p["bd"]]
    scratch = [pltpu.VMEM((N_HEADS, HEAD_DIM, HEAD_DIM), F32), pltpu.VMEM((8, RW_COLS), F32)]
    if has_vres:
        in_specs += [vec(LORA_PAD), vec(BRANCH), _const_spec((LORA_PAD, BRANCH))]
        args += [vres["mu"], vres["v0"], vres["up"]]
        out_shape = jax.ShapeDtypeStruct((T, BRANCH), BF16)
        out_specs = pl.BlockSpec((C, BRANCH), row)
        scratch.append(pltpu.VMEM((8, LORA_PAD), F32))
    else:
        out_shape = (jax.ShapeDtypeStruct((T, BRANCH), BF16), jax.ShapeDtypeStruct((T, BRANCH), F32))
        out_specs = (pl.BlockSpec((C, BRANCH), row), pl.BlockSpec((C, BRANCH), row))
    scratch.append(pltpu.VMEM((C, BRANCH), F32))
    return pl.pallas_call(
        functools.partial(_rwkv_kernel, has_vres),
        out_shape=out_shape,
        grid=(B, nc),
        in_specs=in_specs,
        out_specs=out_specs,
        scratch_shapes=scratch,
        compiler_params=_cparams("parallel", "arbitrary"),
        name="rwkv",
    )(*args)


def _moba_kernel(q_ref, k_ref, v_ref, o_ref, kmean_ref):
    S = q_ref.shape[0]
    nb = S // MB_BLOCK
    tq = MB_BLOCK
    for n in range(nb):
        blk = k_ref[n * MB_BLOCK:(n + 1) * MB_BLOCK, :].astype(F32)
        kmean_ref[n:n + 1, :] = jnp.mean(blk, axis=0, keepdims=True)
    kmean = kmean_ref[...]
    lane = lax.broadcasted_iota(jnp.int32, (1, LANES), 1)
    blk_id = lax.broadcasted_iota(jnp.int32, (nb, tq), 0)
    qi = lax.broadcasted_iota(jnp.int32, (tq, tq), 0)
    ki = lax.broadcasted_iota(jnp.int32, (tq, tq), 1)
    causal = ki <= qi
    for i in range(nb):
        q_t = q_ref[i * tq:(i + 1) * tq, :]
        n_keys = (i + 1) * MB_BLOCK
        k_all = k_ref[0:n_keys, :]
        v_all = v_ref[0:n_keys, :]
        o_heads = []
        for hh in range(LANES // HEAD_DIM):
            head_lanes = (lane // HEAD_DIM) == hh
            q_h = jnp.where(head_lanes, q_t, jnp.zeros_like(q_t))
            s = _dot_nt(q_h, k_all)
            parts = []
            if i > 0:
                gate = _dot_nt(jnp.where(head_lanes, kmean, 0.0), q_t.astype(F32), HI)
                valid = blk_id < i
                cnt = jnp.zeros((nb, tq), jnp.int32)
                for m in range(i):
                    gm = gate[m:m + 1, :]
                    beats = (gm > gate) | ((gm == gate) & (m < blk_id))
                    cnt = cnt + beats.astype(jnp.int32)
                sel = (valid & (cnt < MB_TOPK)).astype(F32)
                sel_t = jnp.concatenate([sel, jnp.zeros((LANES - nb, tq), F32)], axis=0).T
                for j in range(i):
                    keep = sel_t[:, j:j + 1] > 0.5
                    parts.append(jnp.where(keep, s[:, j * MB_BLOCK:(j + 1) * MB_BLOCK], MASK_VALUE))
            parts.append(jnp.where(causal, s[:, i * MB_BLOCK:], MASK_VALUE))
            s = parts[0] if len(parts) == 1 else jnp.concatenate(parts, axis=1)
            m_row = jnp.max(s, axis=1, keepdims=True)
            p = jnp.exp(s - m_row)
            denom = jnp.sum(p, axis=1, keepdims=True)
            p = p / denom
            o_heads.append((head_lanes, _dot(p.astype(BF16), v_all)))
        o_t = jnp.where(o_heads[0][0], o_heads[0][1], o_heads[1][1])
        o_ref[i * tq:(i + 1) * tq, :] = o_t.astype(o_ref.dtype)


def _moba(q, k, v, B, S):
    T = q.shape[0]
    npair = BRANCH // LANES
    spec = pl.BlockSpec((S, LANES), lambda b, p: (b, p))
    return pl.pallas_call(
        _moba_kernel,
        out_shape=jax.ShapeDtypeStruct((T, BRANCH), BF16),
        grid=(B, npair),
        in_specs=[spec, spec, spec],
        out_specs=spec,
        scratch_shapes=[pltpu.VMEM((S // MB_BLOCK, LANES), F32)],
        compiler_params=_cparams("parallel", "parallel"),
        name="moba",
    )(q, k, v)


def _swa_kernel(q_ref, k_ref, v_ref, sink_ref, o_ref):
    S = q_ref.shape[0]
    W = SW_WINDOW
    lane = lax.broadcasted_iota(jnp.int32, (1, LANES), 1)
    qi = lax.broadcasted_iota(jnp.int32, (W, 2 * W), 0) + W
    ki = lax.broadcasted_iota(jnp.int32, (W, 2 * W), 1)
    band = (qi - ki >= 0) & (qi - ki < W)
    qi0 = lax.broadcasted_iota(jnp.int32, (W, W), 0)
    ki0 = lax.broadcasted_iota(jnp.int32, (W, W), 1)
    causal = ki0 <= qi0

    def block(q_t, k_w, v_w, mask):
        outs = []
        for hh in range(LANES // HEAD_DIM):
            head_lanes = (lane // HEAD_DIM) == hh
            q_h = jnp.where(head_lanes, q_t, jnp.zeros_like(q_t))
            s = jnp.where(mask, _dot_nt(q_h, k_w), MASK_VALUE)
            sink = sink_ref[0, hh:hh + 1, 0:1]
            m_row = jnp.maximum(jnp.max(s, axis=1, keepdims=True), sink)
            p = jnp.exp(s - m_row)
            denom = jnp.sum(p, axis=1, keepdims=True) + jnp.exp(sink - m_row)
            p = p / denom
            outs.append((head_lanes, _dot(p.astype(BF16), v_w)))
        return jnp.where(outs[0][0], outs[0][1], outs[1][1])

    o_ref[0:W, :] = block(q_ref[0:W, :], k_ref[0:W, :], v_ref[0:W, :],
                          causal).astype(o_ref.dtype)

    def body(n, carry):
        q0 = pl.multiple_of(n * W, W)
        k0 = pl.multiple_of((n - 1) * W, W)
        o = block(q_ref[pl.ds(q0, W), :], k_ref[pl.ds(k0, 2 * W), :],
                  v_ref[pl.ds(k0, 2 * W), :], band)
        o_ref[pl.ds(q0, W), :] = o.astype(o_ref.dtype)
        return carry

    lax.fori_loop(1, S // W, body, 0)


def _swa(q, k, v, sinks, B, S):
    T = q.shape[0]
    npair = BRANCH // LANES
    spec = pl.BlockSpec((S, LANES), lambda b, p: (b, p))
    kv_spec = pl.BlockSpec((S, LANES), lambda b, p: (b, 0))
    return pl.pallas_call(
        _swa_kernel,
        out_shape=jax.ShapeDtypeStruct((T, BRANCH), BF16),
        grid=(B, npair),
        in_specs=[spec, kv_spec, kv_spec, pl.BlockSpec((1, 2, LANES), lambda b, p: (p, 0, 0))],
        out_specs=spec,
        compiler_params=_cparams("parallel", "parallel"),
        name="swa",
    )(q, k, v, sinks)


def _mix_out_kernel(x_ref, orw_ref, omb_ref, osw_ref, gate_ref, wb_ref, wo_ref, g_ref, out_ref):
    y = None
    for n, o_ref in enumerate((orw_ref, omb_ref, osw_ref)):
        t = gate_ref[:, n * D_MODEL:(n + 1) * D_MODEL] * _dot(o_ref[...], wb_ref[n])
        y = t if y is None else y + t
    y = _dot(y.astype(BF16), wo_ref[...])
    out_ref[...] = x_ref[...] + _rms(y, g_ref[...])


def _mix_out(x2, o_rw, o_mb, o_sw, gates, w_branch, w_out, g):
    T = x2.shape[0]
    tm = 256
    row = lambda i: (i, 0)
    br = pl.BlockSpec((tm, BRANCH), row)
    return pl.pallas_call(
        _mix_out_kernel,
        out_shape=jax.ShapeDtypeStruct((T, D_MODEL), F32),
        grid=(T // tm,),
        in_specs=[pl.BlockSpec((tm, D_MODEL), row), br, br, br,
                  pl.BlockSpec((tm, 3 * D_MODEL), row),
                  _const_spec((3, BRANCH, D_MODEL)), _const_spec((D_MODEL, D_MODEL)),
                  _const_spec((1, D_MODEL))],
        out_specs=pl.BlockSpec((tm, D_MODEL), row),
        compiler_params=_cparams("parallel"),
        name="mix_out",
    )(x2, o_rw, o_mb, o_sw, gates, w_branch, w_out, g)


def _mem_kv_kernel(m_ref, g_ref, wk_ref, wv_ref, k_ref, v_ref):
    m = _rms(m_ref[...], g_ref[...]).astype(BF16)
    k_ref[...] = _dot(m, wk_ref[...]).astype(k_ref.dtype)
    v_ref[...] = _dot(m, wv_ref[...]).astype(v_ref.dtype)


def _mem_kv(mem2, g, wk, wv, M):
    R = mem2.shape[0]
    row = lambda i: (i, 0)
    return pl.pallas_call(
        _mem_kv_kernel,
        out_shape=(jax.ShapeDtypeStruct((R, XA_WIDTH), BF16),) * 2,
        grid=(R // M,),
        in_specs=[pl.BlockSpec((M, D_MODEL), row), _const_spec((1, D_MODEL)),
                  _const_spec((D_MODEL, XA_WIDTH)), _const_spec((D_MODEL, XA_WIDTH))],
        out_specs=(pl.BlockSpec((M, XA_WIDTH), row),) * 2,
        compiler_params=_cparams("parallel"),
        name="mem_kv",
    )(mem2, g, wk, wv)


def _xattn_kernel(x_ref, k_ref, v_ref, gpre_ref, wq_ref, wo_ref, gpost_ref, out_ref):
    x = x_ref[...]
    h = _rms(x, gpre_ref[...]).astype(BF16)
    q = (_dot(h, wq_ref[...]) * XA_HEAD_DIM ** -0.5).astype(BF16)
    outs = []
    for hd in range(XA_HEADS):
        sl = slice(hd * XA_HEAD_DIM, (hd + 1) * XA_HEAD_DIM)
        s = _dot_nt(q[:, sl], k_ref[:, sl])
        m_row = jnp.max(s, axis=1, keepdims=True)
        p = jnp.exp(s - m_row)
        p = p / jnp.sum(p, axis=1, keepdims=True)
        outs.append(_dot(p.astype(BF16), v_ref[:, sl]))
    o = jnp.concatenate(outs, axis=1).astype(BF16)
    out_ref[...] = x + _rms(_dot(o, wo_ref[...]), gpost_ref[...])


def _xattn(x2, k_mem, v_mem, g_pre, wq, wo, g_post, B, S, M):
    T = x2.shape[0]
    tm = 512
    nt = S // tm
    row = lambda b, i: (b * nt + i, 0)
    kv = pl.BlockSpec((M, XA_WIDTH), lambda b, i: (b, 0))
    return pl.pallas_call(
        _xattn_kernel,
        out_shape=jax.ShapeDtypeStruct((T, D_MODEL), F32),
        grid=(B, nt),
        in_specs=[pl.BlockSpec((tm, D_MODEL), row), kv, kv, _const_spec((1, D_MODEL)),
                  _const_spec((D_MODEL, XA_WIDTH)), _const_spec((XA_WIDTH, D_MODEL)),
                  _const_spec((1, D_MODEL))],
        out_specs=pl.BlockSpec((tm, D_MODEL), row),
        compiler_params=_cparams("parallel", "parallel"),
        name="xattn",
    )(x2, k_mem, v_mem, g_pre, wq, wo, g_post)


def _mlp_kernel(x_ref, gpre_ref, wup_ref, wdn_ref, gpost_ref, out_ref, h_ref, acc_ref):
    j = pl.program_id(1)

    @pl.when(j == 0)
    def _():
        h_ref[...] = _rms(x_ref[...], gpre_ref[...]).astype(BF16)
        acc_ref[...] = jnp.zeros_like(acc_ref)

    a = jnp.maximum(_dot(h_ref[...], wup_ref[...]), 0.0)
    acc_ref[...] += _dot((a * a).astype(BF16), wdn_ref[...])

    @pl.when(j == pl.num_programs(1) - 1)
    def _():
        out_ref[...] = x_ref[...] + _rms(acc_ref[...], gpost_ref[...])


def _mlp(x2, g_pre, w_up, w_down, g_post):
    T = x2.shape[0]
    tm = 512
    tf = 1024
    row = lambda i, j: (i, 0)
    return pl.pallas_call(
        _mlp_kernel,
        out_shape=jax.ShapeDtypeStruct((T, D_MODEL), F32),
        grid=(T // tm, D_FF // tf),
        in_specs=[pl.BlockSpec((tm, D_MODEL), row), _const_spec((1, D_MODEL)),
                  pl.BlockSpec((D_MODEL, tf), lambda i, j: (0, j)),
                  pl.BlockSpec((tf, D_MODEL), lambda i, j: (j, 0)),
                  _const_spec((1, D_MODEL))],
        out_specs=pl.BlockSpec((tm, D_MODEL), row),
        scratch_shapes=[pltpu.VMEM((tm, D_MODEL), BF16), pltpu.VMEM((tm, D_MODEL), F32)],
        compiler_params=_cparams("parallel", "arbitrary"),
        name="mlp",
    )(x2, g_pre, w_up, w_down, g_post)


def _sw_head_order():
    per_kv = N_HEADS // SW_KV_HEADS
    return [h for p in range(per_kv) for h in (p, p + per_kv)]


def _head_cols(order):
    return jnp.concatenate([jnp.arange(h * HEAD_DIM, (h + 1) * HEAD_DIM) for h in order])


def _pad_rows(w, offset):
    out = jnp.zeros((LORA_PAD, w.shape[1]), w.dtype)
    return out.at[offset:offset + w.shape[0]].set(w)


def kernel(x, mem, positions, norm_mix_pre, norm_mix_post, norm_xattn_pre, norm_xattn_post, norm_mem, norm_mlp_pre, norm_mlp_post, w_in, rw_mu, rw_w0, rw_w_up, rw_a0, rw_a_up, rw_g_up, rw_k_k, rw_k_a, rw_r_k, rw_lnx_w, rw_lnx_b, rw_vres_down, rw_vres_mu, rw_v0, rw_vres_up, sw_sinks, w_branch, w_out, w_xq, w_xk, w_xv, w_xo, w_up, w_down):
    B, S, D = x.shape
    M = mem.shape[1]
    depth = w_in.shape[0]
    T = B * S
    x2 = x.reshape(T, D)
    mem2 = mem.reshape(B * M, D)
    cos_t, sin_t = _rope_tables(positions)

    sw_order = _sw_head_order()
    sw_cols = _head_cols(sw_order)
    head_of_col = jnp.arange(BRANCH) // HEAD_DIM
    bd = (head_of_col[:, None] == head_of_col[None, :]).astype(F32)
    mb0 = RW_COLS
    sw0 = mb0 + 3 * BRANCH
    gt0 = sw0 + BRANCH + 2 * SW_KV_HEADS * HEAD_DIM
    vec = lambda t: t.reshape(1, -1)

    v_first = None
    for l in range(depth):
        wl = w_in[l]
        segs = [wl[:, :RW_COLS]]
        if l > 0:
            segs.append(jnp.pad(rw_vres_down[l - 1], ((0, 0), (0, LORA_PAD - rw_vres_down.shape[2]))))
        segs += [wl[:, mb0:sw0], wl[:, sw0:sw0 + BRANCH][:, sw_cols], wl[:, sw0 + BRANCH:gt0],
                 wl[:, gt0:]]
        w_cat = jnp.concatenate(segs, axis=1).astype(BF16)
        outs = _mix_in(x2, vec(norm_mix_pre[l]), w_cat, cos_t, sin_t, l > 0)
        if l > 0:
            u_rw, vm, mq, mk, mv, sq, sk, sv, gates = outs
        else:
            u_rw, mq, mk, mv, sq, sk, sv, gates = outs

        rw_p = dict(
            mu=vec(rw_mu[l]), w0=vec(rw_w0[l]), w_up=_pad_rows(rw_w_up[l], 0).astype(BF16),
            a0=vec(rw_a0[l]), a_up=_pad_rows(rw_a_up[l], rw_w_up.shape[1]).astype(BF16),
            g_up=rw_g_up[l].astype(BF16), k_k=vec(rw_k_k[l]), k_a=vec(rw_k_a[l]),
            r_k=vec(rw_r_k[l]), lnx_w=vec(rw_lnx_w[l]), lnx_b=vec(rw_lnx_b[l]), bd=bd)
        if l == 0:
            o_rw, v_first = _rwkv(u_rw, rw_p, B, S)
        else:
            vres = dict(vm=vm, v_first=v_first,
                        mu=jnp.pad(vec(rw_vres_mu[l - 1]), ((0, 0), (0, LORA_PAD - rw_vres_mu.shape[1]))),
                        v0=vec(rw_v0[l - 1]), up=_pad_rows(rw_vres_up[l - 1], 0).astype(BF16))
            o_rw = _rwkv(u_rw, rw_p, B, S, vres)

        o_mb = _moba(mq, mk, mv, B, S)
        sinks = jnp.broadcast_to(sw_sinks[l][jnp.array(sw_order)].reshape(-1, 2, 1),
                                 (BRANCH // LANES, 2, LANES)).astype(F32)
        o_sw = _swa(sq, sk, sv, sinks, B, S)

        wb = jnp.stack([w_branch[l, 0], w_branch[l, 1], w_branch[l, 2][sw_cols]]).astype(BF16)
        x2 = _mix_out(x2, o_rw, o_mb, o_sw, gates, wb, w_out[l].astype(BF16), vec(norm_mix_post[l]))

        k_mem, v_mem = _mem_kv(mem2, vec(norm_mem[l]), w_xk[l].astype(BF16), w_xv[l].astype(BF16), M)
        x2 = _xattn(x2, k_mem, v_mem, vec(norm_xattn_pre[l]), w_xq[l].astype(BF16),
                    w_xo[l].astype(BF16), vec(norm_xattn_post[l]), B, S, M)

        x2 = _mlp(x2, vec(norm_mlp_pre[l]), w_up[l].astype(BF16), w_down[l].astype(BF16),
                  vec(norm_mlp_post[l]))
    return x2.reshape(B, S, D)
```

```python
import functools

import jax
import jax.numpy as jnp
from jax import lax
from jax.experimental import pallas as pl
from jax.experimental.pallas import tpu as pltpu

F32 = jnp.float32
BF16 = jnp.bfloat16

D_MODEL = 1024
HEAD_DIM = 64
BRANCH = 512
N_HEADS = BRANCH // HEAD_DIM
RW_COLS = 3 * BRANCH + 64 + 64 + 128
LORA_PAD = 128
MB_BLOCK = 256
MB_TOPK = 3
SW_WINDOW = 128
SW_KV_HEADS = 2
XA_HEADS = 4
XA_HEAD_DIM = 128
XA_WIDTH = XA_HEADS * XA_HEAD_DIM
D_FF = 4 * D_MODEL
ROPE_THETA = 10000.0
NORM_EPS = 1e-6
RW_LNX_EPS = 1e-5 * HEAD_DIM
MASK_VALUE = -1e30

LANES = 128
RW_CHUNK = 64
RW_BATCH = 1
VMEM_LIMIT = 56 * 1024 * 1024

HI = lax.Precision.HIGHEST
SCAN_PREC = lax.Precision.DEFAULT


def _cparams(*sem):
    return pltpu.CompilerParams(dimension_semantics=sem, vmem_limit_bytes=VMEM_LIMIT)


def _dot(a, b, precision=None):
    return jnp.dot(a, b, preferred_element_type=F32, precision=precision)


def _dot_nt(a, b, precision=None):
    return lax.dot_general(a, b, (((1,), (1,)), ((), ())),
                           preferred_element_type=F32, precision=precision)


def _dot_tn(a, b, precision=None):
    return lax.dot_general(a, b, (((0,), (0,)), ((), ())),
                           preferred_element_type=F32, precision=precision)


def _rms(xf, g):
    ms = jnp.mean(xf * xf, axis=-1, keepdims=True)
    return xf * lax.rsqrt(ms + NORM_EPS) * g


def _const_spec(shape):
    nd = len(shape)
    return pl.BlockSpec(shape, lambda *_: (0,) * nd, pipeline_mode=pl.Buffered(1))


def _rope_kernel(pos_ref, invf_ref, sign_ref, cos_ref, sin_ref):
    ang = pos_ref[...].astype(F32) * invf_ref[...]
    cos_ref[...] = jnp.cos(ang)
    sin_ref[...] = jnp.sin(ang) * sign_ref[...]


def _rope_tables(positions):
    T = positions.size
    tm = 2048
    inv_freq = 1.0 / (ROPE_THETA ** (jnp.arange(0, HEAD_DIM, 2, dtype=F32) / HEAD_DIM))
    invf = jnp.tile(inv_freq, LANES // (HEAD_DIM // 2))[None, :]
    half = jnp.concatenate([-jnp.ones((HEAD_DIM // 2,), F32), jnp.ones((HEAD_DIM // 2,), F32)])
    sign = jnp.tile(half, LANES // HEAD_DIM)[None, :]
    return pl.pallas_call(
        _rope_kernel,
        out_shape=(jax.ShapeDtypeStruct((T, LANES), F32),) * 2,
        grid=(T // tm,),
        in_specs=[pl.BlockSpec((tm, 1), lambda i: (i, 0)),
                  _const_spec((1, LANES)), _const_spec((1, LANES))],
        out_specs=(pl.BlockSpec((tm, LANES), lambda i: (i, 0)),) * 2,
        compiler_params=_cparams("parallel"),
        name="rope_tables",
    )(positions.reshape(T, 1), invf, sign)


def _rope_tile(t, cos, sin, lane_lt_half):
    fwd = pltpu.roll(t, LANES - HEAD_DIM // 2, axis=1)
    bwd = pltpu.roll(t, HEAD_DIM // 2, axis=1)
    partner = jnp.where(lane_lt_half, fwd, bwd)
    return t * cos + partner * sin


def _mix_in_kernel(has_vm, x_ref, g_ref, w_ref, cos_ref, sin_ref, *outs):
    if has_vm:
        urw_ref, vm_ref, mq_ref, mk_ref, mv_ref, sq_ref, sk_ref, sv_ref, gate_ref = outs
    else:
        urw_ref, mq_ref, mk_ref, mv_ref, sq_ref, sk_ref, sv_ref, gate_ref = outs
        vm_ref = None
    h = _rms(x_ref[...], g_ref[...]).astype(BF16)
    cos = cos_ref[...]
    sin = sin_ref[...]
    lane = lax.broadcasted_iota(jnp.int32, cos.shape, 1)
    lt_half = (lane % HEAD_DIM) < (HEAD_DIM // 2)

    col = [0]

    def seg(width):
        a = col[0]
        col[0] += width
        return _dot(h, w_ref[:, a:a + width])

    def rope_store(ref, width, scale):
        acc = seg(width)
        for c in range(width // LANES):
            t = _rope_tile(acc[:, c * LANES:(c + 1) * LANES], cos, sin, lt_half)
            ref[:, c * LANES:(c + 1) * LANES] = (t * scale).astype(ref.dtype)

    urw_ref[...] = seg(RW_COLS)
    if has_vm:
        vm_ref[...] = seg(LORA_PAD)
    rope_store(mq_ref, BRANCH, HEAD_DIM ** -0.5)
    rope_store(mk_ref, BRANCH, 1.0)
    mv_ref[...] = seg(BRANCH).astype(mv_ref.dtype)
    rope_store(sq_ref, BRANCH, HEAD_DIM ** -0.5)
    rope_store(sk_ref, SW_KV_HEADS * HEAD_DIM, 1.0)
    sv_ref[...] = seg(SW_KV_HEADS * HEAD_DIM).astype(sv_ref.dtype)
    for n in range(3):
        gate_ref[:, n * D_MODEL:(n + 1) * D_MODEL] = jax.nn.sigmoid(seg(D_MODEL))


def _mix_in(x2, g, w_cat, cos_t, sin_t, has_vm):
    T = x2.shape[0]
    tm = 256
    ntot = w_cat.shape[1]
    kv_w = SW_KV_HEADS * HEAD_DIM
    widths = [(RW_COLS, F32)]
    if has_vm:
        widths.append((LORA_PAD, F32))
    widths += [(BRANCH, BF16), (BRANCH, BF16), (BRANCH, BF16), (BRANCH, BF16),
               (kv_w, BF16), (kv_w, BF16), (3 * D_MODEL, F32)]
    assert sum(w for w, _ in widths) == ntot
    row = lambda i: (i, 0)
    return pl.pallas_call(
        functools.partial(_mix_in_kernel, has_vm),
        out_shape=tuple(jax.ShapeDtypeStruct((T, w), dt) for w, dt in widths),
        grid=(T // tm,),
        in_specs=[pl.BlockSpec((tm, D_MODEL), row), _const_spec((1, D_MODEL)),
                  _const_spec((D_MODEL, ntot)),
                  pl.BlockSpec((tm, LANES), row), pl.BlockSpec((tm, LANES), row)],
        out_specs=tuple(pl.BlockSpec((tm, w), row) for w, _ in widths),
        compiler_params=_cparams("parallel"),
        name="mix_in",
    )(x2, g, w_cat, cos_t, sin_t)


def _shift_prev(cur, carry_ref, bi):
    rolled = pltpu.roll(cur, 1, axis=0)
    row = lax.broadcasted_iota(jnp.int32, cur.shape, 0)
    prev = jnp.where(row == 0, carry_ref[bi, 0:1, :], rolled)
    carry_ref[bi, 0:1, :] = cur[cur.shape[0] - 1:, :]
    return prev


def _split_bf16(x, terms):
    pieces = []
    for _ in range(terms):
        hi = x.astype(BF16)
        pieces.append(hi)
        x = x - hi.astype(F32)
    return pieces


def _dot_split_lhs(a, b, terms):
    return sum(_dot(piece, b) for piece in _split_bf16(a, terms))


def _dot_split_rhs(a, b, terms):
    return sum(_dot(a, piece) for piece in _split_bf16(b, terms))


def _rwkv_kernel(has_vres, nb, *refs):
    if has_vres:
        (u_ref, vm_ref, vf_ref, mu_ref, w0_ref, wup_ref, a0_ref, aup_ref, gup_ref, kk_ref,
         ka_ref, rk_ref, lw_ref, lb_ref, bd_ref, vmu_ref, v0_ref, vup_ref,
         o_ref, s_ref, cu_ref, cvm_ref, hout_ref) = refs
        vout_ref = None
    else:
        (u_ref, mu_ref, w0_ref, wup_ref, a0_ref, aup_ref, gup_ref, kk_ref,
         ka_ref, rk_ref, lw_ref, lb_ref, bd_ref,
         o_ref, vout_ref, s_ref, cu_ref, hout_ref) = refs
    C = RW_CHUNK
    P2 = 2 * C
    npair = BRANCH // LANES

    @pl.when(pl.program_id(1) == 0)
    def _():
        s_ref[...] = jnp.zeros_like(s_ref)
        cu_ref[...] = jnp.zeros_like(cu_ref)
        if has_vres:
            cvm_ref[...] = jnp.zeros_like(cvm_ref)

    bd = bd_ref[...]
    ti = lax.broadcasted_iota(jnp.int32, (C, C), 0)
    si = lax.broadcasted_iota(jnp.int32, (C, C), 1)
    tri = (si <= ti).astype(BF16)
    lane = lax.broadcasted_iota(jnp.int32, (1, LANES), 1)
    first_head = lane < HEAD_DIM
    pr = lax.broadcasted_iota(jnp.int32, (P2, P2), 0)
    pc = lax.broadcasted_iota(jnp.int32, (P2, P2), 1)
    same_head = (pr // C) == (pc // C)
    strict = same_head & ((pc % C) < (pr % C))
    incl = same_head & ((pc % C) <= (pr % C))
    eye = (pr == pc).astype(F32)

    def stack(t):
        return jnp.concatenate([jnp.where(first_head, t, 0.0), jnp.where(first_head, 0.0, t)], axis=0)

    pre = []
    inst = []
    for bi in range(nb):
        u = u_ref[bi]
        ul = u + mu_ref[...] * (_shift_prev(u, cu_ref, bi) - u)
        r = ul[:, 0:BRANCH]
        k = ul[:, BRANCH:2 * BRANCH]
        v = ul[:, 2 * BRANCH:3 * BRANCH]
        x_wa = ul[:, 3 * BRANCH:3 * BRANCH + LANES]
        x_g = ul[:, 3 * BRANCH + LANES:]

        z = w0_ref[...] + _dot(jnp.tanh(x_wa).astype(BF16), wup_ref[...])
        a = jax.nn.sigmoid(a0_ref[...] + _dot(x_wa.astype(BF16), aup_ref[...]))
        g = _dot(jax.nn.sigmoid(x_g).astype(BF16), gup_ref[...])
        softplus = jnp.maximum(-z, 0.0) + jnp.log1p(jnp.exp(-jnp.abs(z)))
        ell = -jnp.exp(-softplus - 0.5)

        if has_vres:
            vm = vm_ref[bi]
            vml = vm + vmu_ref[...] * (_shift_prev(vm, cvm_ref, bi) - vm)
            vgate = jax.nn.sigmoid(v0_ref[...] + _dot(vml.astype(BF16), vup_ref[...]))
            v = v + (vf_ref[bi] - v) * vgate
        else:
            vout_ref[bi] = v

        kk = k * kk_ref[...]
        kk = kk / jnp.maximum(jnp.sqrt(_dot_split_lhs(kk * kk, bd, 2)), 1e-12)
        k_eff = k * (1.0 + (a - 1.0) * ka_ref[...])
        b = kk * a
        cum = _dot_split_rhs(tri, ell, 3)
        cum_last = cum[C - 1:, :]
        g_inv = jnp.exp(-cum)
        g_tail = jnp.exp(cum_last - cum)
        kap_t = kk * jnp.exp(cum - ell)
        r_t = r * jnp.exp(cum)
        b_t = b * g_inv
        k_t = k_eff * g_inv
        b_g = b * g_tail
        k_g = k_eff * g_tail
        g_all = jnp.exp(cum_last)
        pre.append((r, k_eff, v, g))
        for p in range(npair):
            sl = slice(p * LANES, (p + 1) * LANES)
            inst.append(dict(bi=bi, p=p, sl=sl, kap=stack(kap_t[:, sl]), r=stack(r_t[:, sl]),
                             b=stack(b_t[:, sl]), k=stack(k_t[:, sl]), v=stack(v[:, sl]),
                             bg=stack(b_g[:, sl]), kg=stack(k_g[:, sl]), gall=g_all[:, sl]))

    mm = functools.partial(_dot, precision=SCAN_PREC)
    mm_nt = functools.partial(_dot_nt, precision=SCAN_PREC)
    mm_tn = functools.partial(_dot_tn, precision=SCAN_PREC)

    for d in inst:
        aa = mm_nt(jnp.concatenate([d["kap"], d["r"]], axis=0),
                   jnp.concatenate([d["b"], d["k"]], axis=0))
        d["n"] = jnp.where(strict, -aa[:P2, :P2], 0.0)
        d["a_kv"] = jnp.concatenate([jnp.where(strict, aa[:P2, P2:], 0.0),
                                     jnp.where(incl, aa[P2:, P2:], 0.0)], axis=0)
        d["a_rb"] = jnp.where(incl, aa[P2:, :P2], 0.0)
    for d in inst:
        av = mm(d["a_kv"], d["v"])
        d["akv"], d["arkv"] = av[:P2], av[P2:]
        d["kgv"] = mm_tn(d["v"], d["kg"])
    for d in inst:
        d["t"] = eye + d["n"]
        d["pw"] = mm(d["n"], d["n"])
    for _ in range(4):
        for d in inst:
            pt = mm(d["pw"], jnp.concatenate([d["pw"], d["t"]], axis=1))
            d["pw"] = pt[:, :P2]
            d["t"] = d["t"] + pt[:, P2:]
    for d in inst:
        d["t"] = d["t"] + mm(d["pw"], d["t"])
    for d in inst:
        ty = mm(d["t"], jnp.concatenate([d["kap"], d["akv"]], axis=1))
        d["w_til"], d["u_til"] = ty[:, :LANES], ty[:, LANES:]
    for d in inst:
        d["state"] = s_ref[d["bi"] * npair + d["p"]]
        xr = mm_nt(jnp.concatenate([d["w_til"], d["r"]], axis=0), d["state"])
        d["sa"] = -(xr[:P2] + d["u_til"])
        d["rs"] = xr[P2:]
    for d in inst:
        s_ref[d["bi"] * npair + d["p"]] = (d["state"] * d["gall"] + mm_tn(d["sa"], d["bg"])
                                           + d["kgv"])
    for d in inst:
        o_st = d["rs"] + mm(d["a_rb"], d["sa"]) + d["arkv"]
        hout_ref[d["bi"], :, d["sl"]] = o_st[:C] + o_st[C:]

    inv_n = 1.0 / HEAD_DIM
    for bi in range(nb):
        r, k_eff, v, g = pre[bi]
        out = hout_ref[bi]
        mean = _dot_split_lhs(out, bd, 2) * inv_n
        cen = out - mean
        var = _dot_split_lhs(cen * cen, bd, 2) * inv_n
        normed = cen * lax.rsqrt(var + RW_LNX_EPS) * lw_ref[...] + lb_ref[...]
        bonus = _dot_split_lhs(r * k_eff * rk_ref[...], bd, 2) * v
        o_ref[bi] = ((normed + bonus) * g).astype(o_ref.dtype)


def _rwkv(u_rw, p, B, S, vres=None):
    T = u_rw.shape[0]
    C = RW_CHUNK
    nb = RW_BATCH
    nc = S // C
    has_vres = vres is not None
    blk = lambda w: pl.BlockSpec((nb, C, w), lambda b, c: (b, c, 0))
    vec = lambda n: _const_spec((1, n))
    in_specs = [blk(RW_COLS)]
    args = [u_rw.reshape(B, S, RW_COLS)]
    if has_vres:
        in_specs += [blk(LORA_PAD), blk(BRANCH)]
        args += [vres["vm"].reshape(B, S, LORA_PAD), vres["v_first"].reshape(B, S, BRANCH)]
    in_specs += [vec(RW_COLS), vec(BRANCH), _const_spec((LORA_PAD, BRANCH)), vec(BRANCH),
                 _const_spec((LORA_PAD, BRANCH)), _const_spec((LORA_PAD, BRANCH)),
                 vec(BRANCH), vec(BRANCH), vec(BRANCH), vec(BRANCH), vec(BRANCH),
                 _const_spec((BRANCH, BRANCH))]
    args += [p["mu"], p["w0"], p["w_up"], p["a0"], p["a_up"], p["g_up"], p["k_k"], p["k_a"],
             p["r_k"], p["lnx_w"], p["lnx_b"], p["bd"]]
    scratch = [pltpu.VMEM((nb * (BRANCH // LANES), 2 * C, LANES), F32),
               pltpu.VMEM((nb, 8, RW_COLS), F32)]
    if has_vres:
        in_specs += [vec(LORA_PAD), vec(BRANCH), _const_spec((LORA_PAD, BRANCH))]
        args += [vres["mu"], vres["v0"], vres["up"]]
        out_shape = jax.ShapeDtypeStruct((B, S, BRANCH), BF16)
        out_specs = blk(BRANCH)
        scratch.append(pltpu.VMEM((nb, 8, LORA_PAD), F32))
    else:
        out_shape = (jax.ShapeDtypeStruct((B, S, BRANCH), BF16),
                     jax.ShapeDtypeStruct((B, S, BRANCH), F32))
        out_specs = (blk(BRANCH), blk(BRANCH))
    scratch.append(pltpu.VMEM((nb, C, BRANCH), F32))
    outs = pl.pallas_call(
        functools.partial(_rwkv_kernel, has_vres, nb),
        out_shape=out_shape,
        grid=(B // nb, nc),
        in_specs=in_specs,
        out_specs=out_specs,
        scratch_shapes=scratch,
        compiler_params=_cparams("parallel", "arbitrary"),
        name="rwkv",
    )(*args)
    if has_vres:
        return outs.reshape(T, BRANCH)
    return outs[0].reshape(T, BRANCH), outs[1].reshape(T, BRANCH)


def _moba_kernel(q_ref, k_ref, v_ref, o_ref, kmean_ref, sel_ref):
    S = q_ref.shape[0]
    nb = S // MB_BLOCK
    tq = MB_BLOCK
    n_heads = LANES // HEAD_DIM
    for n in range(nb):
        blk = k_ref[n * MB_BLOCK:(n + 1) * MB_BLOCK, :].astype(F32)
        kmean_ref[n:n + 1, :] = jnp.mean(blk, axis=0, keepdims=True)
    kmean = kmean_ref[...]
    lane = lax.broadcasted_iota(jnp.int32, (1, LANES), 1)
    head_lanes = [(lane // HEAD_DIM) == hh for hh in range(n_heads)]

    blk_id = lax.broadcasted_iota(jnp.int32, (nb, S), 0)
    own = lax.broadcasted_iota(jnp.int32, (nb, S), 1) // MB_BLOCK
    valid = blk_id < own
    q_all = q_ref[...].astype(F32)
    for hh in range(n_heads):
        gate = _dot_nt(jnp.where(head_lanes[hh], kmean, 0.0), q_all, HI)
        cnt = jnp.zeros((nb, S), jnp.int32)
        for m in range(nb - 1):
            gm = gate[m:m + 1, :]
            beats = valid[m:m + 1, :] & ((gm > gate) | ((gm == gate) & (m < blk_id)))
            cnt = cnt + beats.astype(jnp.int32)
        sel = (valid & (cnt < MB_TOPK)).astype(F32)
        sel_ref[hh] = jnp.concatenate([sel, jnp.zeros((LANES - nb, S), F32)], axis=0).T

    qi = lax.broadcasted_iota(jnp.int32, (tq, tq), 0)
    ki = lax.broadcasted_iota(jnp.int32, (tq, tq), 1)
    causal = ki <= qi

    def scores(i):
        q_t = q_ref[i * tq:(i + 1) * tq, :]
        k_all = k_ref[0:(i + 1) * MB_BLOCK, :]
        return [_dot_nt(jnp.where(head_lanes[hh], q_t, jnp.zeros_like(q_t)), k_all)
                for hh in range(n_heads)]

    s_cur = scores(0)
    for i in range(nb):
        s_next = scores(i + 1) if i + 1 < nb else None
        probs = []
        for hh in range(n_heads):
            s = s_cur[hh]
            parts = []
            if i > 0:
                sel_t = sel_ref[hh, i * tq:(i + 1) * tq, :]
                for j in range(i):
                    keep = sel_t[:, j:j + 1] > 0.5
                    parts.append(jnp.where(keep, s[:, j * MB_BLOCK:(j + 1) * MB_BLOCK], MASK_VALUE))
            parts.append(jnp.where(causal, s[:, i * MB_BLOCK:], MASK_VALUE))
            s = parts[0] if len(parts) == 1 else jnp.concatenate(parts, axis=1)
            m_row = jnp.max(s, axis=1, keepdims=True)
            p = jnp.exp(s - m_row)
            probs.append((p.astype(BF16), jnp.sum(p, axis=1, keepdims=True)))
        v_all = v_ref[0:(i + 1) * MB_BLOCK, :]
        o_heads = [_dot(p, v_all) / denom for p, denom in probs]
        o_t = jnp.where(head_lanes[0], o_heads[0], o_heads[1])
        o_ref[i * tq:(i + 1) * tq, :] = o_t.astype(o_ref.dtype)
        s_cur = s_next


def _moba(q, k, v, B, S):
    T = q.shape[0]
    npair = BRANCH // LANES
    spec = pl.BlockSpec((S, LANES), lambda b, p: (b, p))
    return pl.pallas_call(
        _moba_kernel,
        out_shape=jax.ShapeDtypeStruct((T, BRANCH), BF16),
        grid=(B, npair),
        in_specs=[spec, spec, spec],
        out_specs=spec,
        scratch_shapes=[pltpu.VMEM((S // MB_BLOCK, LANES), F32),
                        pltpu.VMEM((LANES // HEAD_DIM, S, LANES), F32)],
        compiler_params=_cparams("parallel", "parallel"),
        name="moba",
    )(q, k, v)


def _swa_kernel(q_ref, k_ref, v_ref, sink_ref, o_ref):
    S = q_ref.shape[0]
    W = SW_WINDOW
    lane = lax.broadcasted_iota(jnp.int32, (1, LANES), 1)
    qi = lax.broadcasted_iota(jnp.int32, (W, 2 * W), 0) + W
    ki = lax.broadcasted_iota(jnp.int32, (W, 2 * W), 1)
    band = (qi - ki >= 0) & (qi - ki < W)
    qi0 = lax.broadcasted_iota(jnp.int32, (W, W), 0)
    ki0 = lax.broadcasted_iota(jnp.int32, (W, W), 1)
    causal = ki0 <= qi0

    head_lanes = [(lane // HEAD_DIM) == hh for hh in range(LANES // HEAD_DIM)]
    sinks = [sink_ref[0, hh:hh + 1, 0:1] for hh in range(LANES // HEAD_DIM)]
    group = 4

    for g0 in range(0, S // W, group):
        work = []
        for n in range(g0, g0 + group):
            k0 = max(n - 1, 0) * W
            k1 = (n + 1) * W
            mask = causal if n == 0 else band
            q_t = q_ref[n * W:(n + 1) * W, :]
            k_w = k_ref[k0:k1, :]
            for hh in range(LANES // HEAD_DIM):
                q_h = jnp.where(head_lanes[hh], q_t, jnp.zeros_like(q_t))
                work.append(dict(n=n, hh=hh, k0=k0, k1=k1, mask=mask, s=_dot_nt(q_h, k_w)))
        for d in work:
            s = jnp.where(d["mask"], d["s"], MASK_VALUE)
            sink = sinks[d["hh"]]
            m_row = jnp.maximum(jnp.max(s, axis=1, keepdims=True), sink)
            p = jnp.exp(s - m_row)
            d["denom"] = jnp.sum(p, axis=1, keepdims=True) + jnp.exp(sink - m_row)
            d["p"] = p.astype(BF16)
        for d in work:
            d["o"] = _dot(d["p"], v_ref[d["k0"]:d["k1"], :]) / d["denom"]
        for a, b in zip(work[0::2], work[1::2]):
            n = a["n"]
            o_ref[n * W:(n + 1) * W, :] = jnp.where(head_lanes[0], a["o"], b["o"]).astype(o_ref.dtype)


def _swa(q, k, v, sinks, B, S):
    T = q.shape[0]
    npair = BRANCH // LANES
    spec = pl.BlockSpec((S, LANES), lambda b, p: (b, p))
    kv_spec = pl.BlockSpec((S, LANES), lambda b, p: (b, 0))
    return pl.pallas_call(
        _swa_kernel,
        out_shape=jax.ShapeDtypeStruct((T, BRANCH), BF16),
        grid=(B, npair),
        in_specs=[spec, kv_spec, kv_spec, pl.BlockSpec((1, 2, LANES), lambda b, p: (p, 0, 0))],
        out_specs=spec,
        compiler_params=_cparams("parallel", "parallel"),
        name="swa",
    )(q, k, v, sinks)


def _mix_out_kernel(x_ref, orw_ref, omb_ref, osw_ref, gate_ref, wb_ref, wo_ref, g_ref, out_ref):
    y = None
    for n, o_ref in enumerate((orw_ref, omb_ref, osw_ref)):
        t = gate_ref[:, n * D_MODEL:(n + 1) * D_MODEL] * _dot(o_ref[...], wb_ref[n])
        y = t if y is None else y + t
    y = _dot(y.astype(BF16), wo_ref[...])
    out_ref[...] = x_ref[...] + _rms(y, g_ref[...])


def _mix_out(x2, o_rw, o_mb, o_sw, gates, w_branch, w_out, g):
    T = x2.shape[0]
    tm = 256
    row = lambda i: (i, 0)
    br = pl.BlockSpec((tm, BRANCH), row)
    return pl.pallas_call(
        _mix_out_kernel,
        out_shape=jax.ShapeDtypeStruct((T, D_MODEL), F32),
        grid=(T // tm,),
        in_specs=[pl.BlockSpec((tm, D_MODEL), row), br, br, br,
                  pl.BlockSpec((tm, 3 * D_MODEL), row),
                  _const_spec((3, BRANCH, D_MODEL)), _const_spec((D_MODEL, D_MODEL)),
                  _const_spec((1, D_MODEL))],
        out_specs=pl.BlockSpec((tm, D_MODEL), row),
        compiler_params=_cparams("parallel"),
        name="mix_out",
    )(x2, o_rw, o_mb, o_sw, gates, w_branch, w_out, g)


def _mem_kv_kernel(m_ref, g_ref, wk_ref, wv_ref, k_ref, v_ref):
    m = _rms(m_ref[...], g_ref[...]).astype(BF16)
    k_ref[...] = _dot(m, wk_ref[...]).astype(k_ref.dtype)
    v_ref[...] = _dot(m, wv_ref[...]).astype(v_ref.dtype)


def _mem_kv(mem2, g, wk, wv, M):
    R = mem2.shape[0]
    row = lambda i: (i, 0)
    return pl.pallas_call(
        _mem_kv_kernel,
        out_shape=(jax.ShapeDtypeStruct((R, XA_WIDTH), BF16),) * 2,
        grid=(R // M,),
        in_specs=[pl.BlockSpec((M, D_MODEL), row), _const_spec((1, D_MODEL)),
                  _const_spec((D_MODEL, XA_WIDTH)), _const_spec((D_MODEL, XA_WIDTH))],
        out_specs=(pl.BlockSpec((M, XA_WIDTH), row),) * 2,
        compiler_params=_cparams("parallel"),
        name="mem_kv",
    )(mem2, g, wk, wv)


def _xattn_kernel(x_ref, k_ref, v_ref, gpre_ref, wq_ref, wo_ref, gpost_ref, out_ref):
    x = x_ref[...]
    h = _rms(x, gpre_ref[...]).astype(BF16)
    q = (_dot(h, wq_ref[...]) * XA_HEAD_DIM ** -0.5).astype(BF16)
    outs = []
    for hd in range(XA_HEADS):
        sl = slice(hd * XA_HEAD_DIM, (hd + 1) * XA_HEAD_DIM)
        s = _dot_nt(q[:, sl], k_ref[:, sl])
        m_row = jnp.max(s, axis=1, keepdims=True)
        p = jnp.exp(s - m_row)
        p = p / jnp.sum(p, axis=1, keepdims=True)
        outs.append(_dot(p.astype(BF16), v_ref[:, sl]))
    o = jnp.concatenate(outs, axis=1).astype(BF16)
    out_ref[...] = x + _rms(_dot(o, wo_ref[...]), gpost_ref[...])


def _xattn(x2, k_mem, v_mem, g_pre, wq, wo, g_post, B, S, M):
    T = x2.shape[0]
    tm = 512
    nt = S // tm
    row = lambda b, i: (b * nt + i, 0)
    kv = pl.BlockSpec((M, XA_WIDTH), lambda b, i: (b, 0))
    return pl.pallas_call(
        _xattn_kernel,
        out_shape=jax.ShapeDtypeStruct((T, D_MODEL), F32),
        grid=(B, nt),
        in_specs=[pl.BlockSpec((tm, D_MODEL), row), kv, kv, _const_spec((1, D_MODEL)),
                  _const_spec((D_MODEL, XA_WIDTH)), _const_spec((XA_WIDTH, D_MODEL)),
                  _const_spec((1, D_MODEL))],
        out_specs=pl.BlockSpec((tm, D_MODEL), row),
        compiler_params=_cparams("parallel", "parallel"),
        name="xattn",
    )(x2, k_mem, v_mem, g_pre, wq, wo, g_post)


def _mlp_kernel(x_ref, gpre_ref, wup_ref, wdn_ref, gpost_ref, out_ref, h_ref, acc_ref):
    j = pl.program_id(1)

    @pl.when(j == 0)
    def _():
        h_ref[...] = _rms(x_ref[...], gpre_ref[...]).astype(BF16)
        acc_ref[...] = jnp.zeros_like(acc_ref)

    a = jnp.maximum(_dot(h_ref[...], wup_ref[...]), 0.0)
    acc_ref[...] += _dot((a * a).astype(BF16), wdn_ref[...])

    @pl.when(j == pl.num_programs(1) - 1)
    def _():
        out_ref[...] = x_ref[...] + _rms(acc_ref[...], gpost_ref[...])


def _mlp(x2, g_pre, w_up, w_down, g_post):
    T = x2.shape[0]
    tm = 512
    tf = 1024
    row = lambda i, j: (i, 0)
    return pl.pallas_call(
        _mlp_kernel,
        out_shape=jax.ShapeDtypeStruct((T, D_MODEL), F32),
        grid=(T // tm, D_FF // tf),
        in_specs=[pl.BlockSpec((tm, D_MODEL), row), _const_spec((1, D_MODEL)),
                  pl.BlockSpec((D_MODEL, tf), lambda i, j: (0, j)),
                  pl.BlockSpec((tf, D_MODEL), lambda i, j: (j, 0)),
                  _const_spec((1, D_MODEL))],
        out_specs=pl.BlockSpec((tm, D_MODEL), row),
        scratch_shapes=[pltpu.VMEM((tm, D_MODEL), BF16), pltpu.VMEM((tm, D_MODEL), F32)],
        compiler_params=_cparams("parallel", "arbitrary"),
        name="mlp",
    )(x2, g_pre, w_up, w_down, g_post)


def _sw_head_order():
    per_kv = N_HEADS // SW_KV_HEADS
    return [h for p in range(per_kv) for h in (p, p + per_kv)]


def _head_cols(order):
    return jnp.concatenate([jnp.arange(h * HEAD_DIM, (h + 1) * HEAD_DIM) for h in order])


def _pad_rows(w, offset):
    out = jnp.zeros((LORA_PAD, w.shape[1]), w.dtype)
    return out.at[offset:offset + w.shape[0]].set(w)


def kernel(x, mem, positions, norm_mix_pre, norm_mix_post, norm_xattn_pre, norm_xattn_post, norm_mem, norm_mlp_pre, norm_mlp_post, w_in, rw_mu, rw_w0, rw_w_up, rw_a0, rw_a_up, rw_g_up, rw_k_k, rw_k_a, rw_r_k, rw_lnx_w, rw_lnx_b, rw_vres_down, rw_vres_mu, rw_v0, rw_vres_up, sw_sinks, w_branch, w_out, w_xq, w_xk, w_xv, w_xo, w_up, w_down):
    B, S, D = x.shape
    M = mem.shape[1]
    depth = w_in.shape[0]
    T = B * S
    x2 = x.reshape(T, D)
    mem2 = mem.reshape(B * M, D)
    cos_t, sin_t = _rope_tables(positions)

    sw_order = _sw_head_order()
    sw_cols = _head_cols(sw_order)
    head_of_col = jnp.arange(BRANCH) // HEAD_DIM
    bd = (head_of_col[:, None] == head_of_col[None, :]).astype(BF16)
    mb0 = RW_COLS
    sw0 = mb0 + 3 * BRANCH
    gt0 = sw0 + BRANCH + 2 * SW_KV_HEADS * HEAD_DIM
    vec = lambda t: t.reshape(1, -1)

    v_first = None
    for l in range(depth):
        wl = w_in[l]
        segs = [wl[:, :RW_COLS]]
        if l > 0:
            segs.append(jnp.pad(rw_vres_down[l - 1], ((0, 0), (0, LORA_PAD - rw_vres_down.shape[2]))))
        segs += [wl[:, mb0:sw0], wl[:, sw0:sw0 + BRANCH][:, sw_cols], wl[:, sw0 + BRANCH:gt0],
                 wl[:, gt0:]]
        w_cat = jnp.concatenate(segs, axis=1).astype(BF16)
        outs = _mix_in(x2, vec(norm_mix_pre[l]), w_cat, cos_t, sin_t, l > 0)
        if l > 0:
            u_rw, vm, mq, mk, mv, sq, sk, sv, gates = outs
        else:
            u_rw, mq, mk, mv, sq, sk, sv, gates = outs

        rw_p = dict(
            mu=vec(rw_mu[l]), w0=vec(rw_w0[l]), w_up=_pad_rows(rw_w_up[l], 0).astype(BF16),
            a0=vec(rw_a0[l]), a_up=_pad_rows(rw_a_up[l], rw_w_up.shape[1]).astype(BF16),
            g_up=rw_g_up[l].astype(BF16), k_k=vec(rw_k_k[l]), k_a=vec(rw_k_a[l]),
            r_k=vec(rw_r_k[l]), lnx_w=vec(rw_lnx_w[l]), lnx_b=vec(rw_lnx_b[l]), bd=bd)
        if l == 0:
            o_rw, v_first = _rwkv(u_rw, rw_p, B, S)
        else:
            vres = dict(vm=vm, v_first=v_first,
                        mu=jnp.pad(vec(rw_vres_mu[l - 1]), ((0, 0), (0, LORA_PAD - rw_vres_mu.shape[1]))),
                        v0=vec(rw_v0[l - 1]), up=_pad_rows(rw_vres_up[l - 1], 0).astype(BF16))
            o_rw = _rwkv(u_rw, rw_p, B, S, vres)

        o_mb = _moba(mq, mk, mv, B, S)
        sinks = jnp.broadcast_to(sw_sinks[l][jnp.array(sw_order)].reshape(-1, 2, 1),
                                 (BRANCH // LANES, 2, LANES)).astype(F32)
        o_sw = _swa(sq, sk, sv, sinks, B, S)

        wb = jnp.stack([w_branch[l, 0], w_branch[l, 1], w_branch[l, 2][sw_cols]]).astype(BF16)
        x2 = _mix_out(x2, o_rw, o_mb, o_sw, gates, wb, w_out[l].astype(BF16), vec(norm_mix_post[l]))

        k_mem, v_mem = _mem_kv(mem2, vec(norm_mem[l]), w_xk[l].astype(BF16), w_xv[l].astype(BF16), M)
        x2 = _xattn(x2, k_mem, v_mem, vec(norm_xattn_pre[l]), w_xq[l].astype(BF16),
                    w_xo[l].astype(BF16), vec(norm_xattn_post[l]), B, S, M)

        x2 = _mlp(x2, vec(norm_mlp_pre[l]), w_up[l].astype(BF16), w_down[l].astype(BF16),
                  vec(norm_mlp_post[l]))
    return x2.reshape(B, S, D)
```

```python
import functools

import jax
import jax.numpy as jnp
from jax import lax
from jax.experimental import pallas as pl
from jax.experimental.pallas import tpu as pltpu

F32 = jnp.float32
BF16 = jnp.bfloat16

D_MODEL = 1024
HEAD_DIM = 64
BRANCH = 512
N_HEADS = BRANCH // HEAD_DIM
RW_COLS = 3 * BRANCH + 64 + 64 + 128
LORA_PAD = 128
MB_BLOCK = 256
MB_TOPK = 3
SW_WINDOW = 128
SW_KV_HEADS = 2
XA_HEADS = 4
XA_HEAD_DIM = 128
XA_WIDTH = XA_HEADS * XA_HEAD_DIM
D_FF = 4 * D_MODEL
ROPE_THETA = 10000.0
NORM_EPS = 1e-6
RW_LNX_EPS = 1e-5 * HEAD_DIM
MASK_VALUE = -1e30

LANES = 128
MXU_TILE = 256
RW_CHUNK = 64
MLP_ROWS = 1024
MLP_COLS = 2048
MLP_SUB = 512
RW_BATCH = 4
VMEM_LIMIT = 56 * 1024 * 1024

HI = lax.Precision.HIGHEST
SCAN_PREC = lax.Precision.DEFAULT


def _cparams(*sem):
    return pltpu.CompilerParams(dimension_semantics=sem, vmem_limit_bytes=VMEM_LIMIT)


def _dot(a, b, precision=None):
    return jnp.dot(a, b, preferred_element_type=F32, precision=precision)


def _dot_nt(a, b, precision=None):
    return lax.dot_general(a, b, (((1,), (1,)), ((), ())),
                           preferred_element_type=F32, precision=precision)


def _dot_tn(a, b, precision=None):
    return lax.dot_general(a, b, (((0,), (0,)), ((), ())),
                           preferred_element_type=F32, precision=precision)


def _rms(xf, g):
    ms = jnp.mean(xf * xf, axis=-1, keepdims=True)
    return xf * lax.rsqrt(ms + NORM_EPS) * g


def _const_spec(shape):
    nd = len(shape)
    return pl.BlockSpec(shape, lambda *_: (0,) * nd, pipeline_mode=pl.Buffered(1))


def _rope_kernel(pos_ref, invf_ref, sign_ref, cos_ref, sin_ref):
    ang = pos_ref[...].astype(F32) * invf_ref[...]
    cos_ref[...] = jnp.cos(ang)
    sin_ref[...] = jnp.sin(ang) * sign_ref[...]


def _rope_tables(positions):
    T = positions.size
    tm = 2048
    inv_freq = 1.0 / (ROPE_THETA ** (jnp.arange(0, HEAD_DIM, 2, dtype=F32) / HEAD_DIM))
    invf = jnp.tile(inv_freq, LANES // (HEAD_DIM // 2))[None, :]
    half = jnp.concatenate([-jnp.ones((HEAD_DIM // 2,), F32), jnp.ones((HEAD_DIM // 2,), F32)])
    sign = jnp.tile(half, LANES // HEAD_DIM)[None, :]
    return pl.pallas_call(
        _rope_kernel,
        out_shape=(jax.ShapeDtypeStruct((T, LANES), F32),) * 2,
        grid=(T // tm,),
        in_specs=[pl.BlockSpec((tm, 1), lambda i: (i, 0)),
                  _const_spec((1, LANES)), _const_spec((1, LANES))],
        out_specs=(pl.BlockSpec((tm, LANES), lambda i: (i, 0)),) * 2,
        compiler_params=_cparams("parallel"),
        name="rope_tables",
    )(positions.reshape(T, 1), invf, sign)


def _rope_tile(t, cos, sin, lane_lt_half):
    fwd = pltpu.roll(t, LANES - HEAD_DIM // 2, axis=1)
    bwd = pltpu.roll(t, HEAD_DIM // 2, axis=1)
    partner = jnp.where(lane_lt_half, fwd, bwd)
    return t * cos + partner * sin


def _mix_in_kernel(has_vm, x_ref, g_ref, w_ref, cos_ref, sin_ref, *outs):
    if has_vm:
        urw_ref, vm_ref, mq_ref, mk_ref, mv_ref, sq_ref, sk_ref, sv_ref, gate_ref = outs
    else:
        urw_ref, mq_ref, mk_ref, mv_ref, sq_ref, sk_ref, sv_ref, gate_ref = outs
        vm_ref = None
    h = _rms(x_ref[...], g_ref[...]).astype(BF16)
    cos = cos_ref[...]
    sin = sin_ref[...]
    lane = lax.broadcasted_iota(jnp.int32, cos.shape, 1)
    lt_half = (lane % HEAD_DIM) < (HEAD_DIM // 2)

    col = [0]

    def seg(width):
        a = col[0]
        col[0] += width
        return _dot(h, w_ref[:, a:a + width])

    def rope_store(ref, width, scale):
        acc = seg(width)
        for c in range(width // LANES):
            t = _rope_tile(acc[:, c * LANES:(c + 1) * LANES], cos, sin, lt_half)
            ref[:, c * LANES:(c + 1) * LANES] = (t * scale).astype(ref.dtype)

    urw_ref[...] = seg(RW_COLS).astype(urw_ref.dtype)
    if has_vm:
        vm_ref[...] = seg(LORA_PAD)
    rope_store(mq_ref, BRANCH, HEAD_DIM ** -0.5)
    rope_store(mk_ref, BRANCH, 1.0)
    mv_ref[...] = seg(BRANCH).astype(mv_ref.dtype)
    rope_store(sq_ref, BRANCH, HEAD_DIM ** -0.5)
    rope_store(sk_ref, SW_KV_HEADS * HEAD_DIM, 1.0)
    sv_ref[...] = seg(SW_KV_HEADS * HEAD_DIM).astype(sv_ref.dtype)
    for n in range(3):
        gate_ref[:, n * D_MODEL:(n + 1) * D_MODEL] = jax.nn.sigmoid(seg(D_MODEL)).astype(gate_ref.dtype)


def _mix_in(x2, g, w_cat, cos_t, sin_t, has_vm):
    T = x2.shape[0]
    tm = 256
    ntot = w_cat.shape[1]
    kv_w = SW_KV_HEADS * HEAD_DIM
    widths = [(RW_COLS, BF16)]
    if has_vm:
        widths.append((LORA_PAD, F32))
    widths += [(BRANCH, BF16), (BRANCH, BF16), (BRANCH, BF16), (BRANCH, BF16),
               (kv_w, BF16), (kv_w, BF16), (3 * D_MODEL, BF16)]
    assert sum(w for w, _ in widths) == ntot
    row = lambda i: (i, 0)
    return pl.pallas_call(
        functools.partial(_mix_in_kernel, has_vm),
        out_shape=tuple(jax.ShapeDtypeStruct((T, w), dt) for w, dt in widths),
        grid=(T // tm,),
        in_specs=[pl.BlockSpec((tm, D_MODEL), row), _const_spec((1, D_MODEL)),
                  _const_spec((D_MODEL, ntot)),
                  pl.BlockSpec((tm, LANES), row), pl.BlockSpec((tm, LANES), row)],
        out_specs=tuple(pl.BlockSpec((tm, w), row) for w, _ in widths),
        compiler_params=_cparams("parallel"),
        name="mix_in",
    )(x2, g, w_cat, cos_t, sin_t)


def _shift_prev(cur, carry_ref, bi):
    rolled = pltpu.roll(cur, 1, axis=0)
    row = lax.broadcasted_iota(jnp.int32, cur.shape, 0)
    prev = jnp.where(row == 0, carry_ref[bi, 0:1, :], rolled)
    carry_ref[bi, 0:1, :] = cur[cur.shape[0] - 1:, :]
    return prev


def _split_bf16(x, terms):
    pieces = []
    for _ in range(terms):
        hi = x.astype(BF16)
        pieces.append(hi)
        x = x - hi.astype(F32)
    return pieces


def _head_sums(x, bd, terms):
    w = bd.shape[0]
    halves = []
    for c in range(x.shape[1] // w):
        pieces = _split_bf16(x[:, c * w:(c + 1) * w], terms)
        halves.append(sum(_dot(piece, bd) for piece in pieces))
    return jnp.concatenate(halves, axis=1)


def _dot_split_rhs(a, b, terms):
    return sum(_dot(a, piece) for piece in _split_bf16(b, terms))


def _rwkv_kernel(has_vres, nb, *refs):
    if has_vres:
        (u_ref, vm_ref, vf_ref, mu_ref, w0_ref, wup_ref, a0_ref, aup_ref, gup_ref, kk_ref,
         ka_ref, rk_ref, lw_ref, lb_ref, bd_ref, vmu_ref, v0_ref, vup_ref,
         o_ref, s_ref, cu_ref, cvm_ref, hout_ref) = refs
        vout_ref = None
    else:
        (u_ref, mu_ref, w0_ref, wup_ref, a0_ref, aup_ref, gup_ref, kk_ref,
         ka_ref, rk_ref, lw_ref, lb_ref, bd_ref,
         o_ref, vout_ref, s_ref, cu_ref, hout_ref) = refs
    C = RW_CHUNK
    P2 = 2 * C
    npair = BRANCH // LANES

    @pl.when(pl.program_id(1) == 0)
    def _():
        s_ref[...] = jnp.zeros_like(s_ref)
        cu_ref[...] = jnp.zeros_like(cu_ref)
        if has_vres:
            cvm_ref[...] = jnp.zeros_like(cvm_ref)

    bd = bd_ref[...]
    ti = lax.broadcasted_iota(jnp.int32, (C, C), 0)
    si = lax.broadcasted_iota(jnp.int32, (C, C), 1)
    tri = (si <= ti).astype(BF16)
    lane = lax.broadcasted_iota(jnp.int32, (1, LANES), 1)
    first_head = lane < HEAD_DIM
    pr = lax.broadcasted_iota(jnp.int32, (P2, P2), 0)
    pc = lax.broadcasted_iota(jnp.int32, (P2, P2), 1)
    same_head = (pr // C) == (pc // C)
    strict = same_head & ((pc % C) < (pr % C))
    incl = same_head & ((pc % C) <= (pr % C))
    eye = (pr == pc).astype(F32)

    def stack(t):
        return jnp.concatenate([jnp.where(first_head, t, 0.0), jnp.where(first_head, 0.0, t)], axis=0)

    pre = []
    inst = []
    for bi in range(nb):
        u = u_ref[bi].astype(F32)
        ul = u + mu_ref[...] * (_shift_prev(u, cu_ref, bi) - u)
        r = ul[:, 0:BRANCH]
        k = ul[:, BRANCH:2 * BRANCH]
        v = ul[:, 2 * BRANCH:3 * BRANCH]
        x_wa = ul[:, 3 * BRANCH:3 * BRANCH + LANES]
        x_g = ul[:, 3 * BRANCH + LANES:]

        z = w0_ref[...] + _dot(jnp.tanh(x_wa).astype(BF16), wup_ref[...])
        a = jax.nn.sigmoid(a0_ref[...] + _dot(x_wa.astype(BF16), aup_ref[...]))
        g = _dot(jax.nn.sigmoid(x_g).astype(BF16), gup_ref[...])
        softplus = jnp.maximum(-z, 0.0) + jnp.log1p(jnp.exp(-jnp.abs(z)))
        ell = -jnp.exp(-softplus - 0.5)

        if has_vres:
            vm = vm_ref[bi]
            vml = vm + vmu_ref[...] * (_shift_prev(vm, cvm_ref, bi) - vm)
            vgate = jax.nn.sigmoid(v0_ref[...] + _dot(vml.astype(BF16), vup_ref[...]))
            v = v + (vf_ref[bi] - v) * vgate
        else:
            vout_ref[bi] = v

        kk = k * kk_ref[...]
        kk = kk / jnp.maximum(jnp.sqrt(_head_sums(kk * kk, bd, 1)), 1e-12)
        k_eff = k * (1.0 + (a - 1.0) * ka_ref[...])
        b = kk * a
        cum = _dot_split_rhs(tri, ell, 3)
        cum_last = cum[C - 1:, :]
        g_inv = jnp.exp(-cum)
        g_tail = jnp.exp(cum_last - cum)
        kap_t = kk * jnp.exp(cum - ell)
        r_t = r * jnp.exp(cum)
        b_t = b * g_inv
        k_t = k_eff * g_inv
        b_g = b * g_tail
        k_g = k_eff * g_tail
        g_all = jnp.exp(cum_last)
        pre.append((r, k_eff, v, g))
        for p in range(npair):
            sl = slice(p * LANES, (p + 1) * LANES)
            inst.append(dict(bi=bi, p=p, sl=sl, kap=stack(kap_t[:, sl]), r=stack(r_t[:, sl]),
                             b=stack(b_t[:, sl]), k=stack(k_t[:, sl]), v=stack(v[:, sl]),
                             bg=stack(b_g[:, sl]), kg=stack(k_g[:, sl]), gall=g_all[:, sl]))

    def mm(a, b):
        return _dot(a.astype(BF16), b.astype(BF16))

    def mm_nt(a, b):
        return _dot_nt(a.astype(BF16), b.astype(BF16))

    def mm_tn(a, b):
        return _dot_tn(a.astype(BF16), b.astype(BF16))

    for d in inst:
        aa = mm_nt(jnp.concatenate([d["kap"], d["r"]], axis=0),
                   jnp.concatenate([d["b"], d["k"]], axis=0))
        d["n"] = jnp.where(strict, -aa[:P2, :P2], 0.0)
        d["a_ak"] = jnp.where(strict, aa[:P2, P2:], 0.0)
        d["a_r"] = jnp.where(jnp.concatenate([incl, incl], axis=1), aa[P2:, :], 0.0)
    for d in inst:
        d["akv"] = mm(d["a_ak"], d["v"])
    for d in inst:
        d["t"] = eye + d["n"]
        d["pw"] = mm(d["n"], d["n"])
    for _ in range(4):
        for d in inst:
            pt = mm(d["pw"], jnp.concatenate([d["pw"], d["t"]], axis=1))
            d["pw"] = pt[:, :P2]
            d["t"] = d["t"] + pt[:, P2:]
    for d in inst:
        d["t"] = d["t"] + mm(d["pw"], d["t"])
    for d in inst:
        ty = mm(d["t"], jnp.concatenate([d["kap"], d["akv"]], axis=1))
        d["w_til"], d["u_til"] = ty[:, :LANES], ty[:, LANES:]
    for d in inst:
        d["state"] = s_ref[d["bi"] * npair + d["p"]]
        xr = mm_nt(jnp.concatenate([d["w_til"], d["r"]], axis=0), d["state"])
        sa = -(xr[:P2] + d["u_til"])
        d["sv"] = jnp.concatenate([sa, d["v"]], axis=0)
        d["rs"] = xr[P2:]
    for d in inst:
        s_ref[d["bi"] * npair + d["p"]] = (
            d["state"] * d["gall"] + mm_tn(d["sv"], jnp.concatenate([d["bg"], d["kg"]], axis=0)))
    for d in inst:
        o_st = d["rs"] + mm(d["a_r"], d["sv"])
        hout_ref[d["bi"], :, d["sl"]] = o_st[:C] + o_st[C:]

    inv_n = 1.0 / HEAD_DIM
    for bi in range(nb):
        r, k_eff, v, g = pre[bi]
        out = hout_ref[bi]
        mean = _head_sums(out, bd, 2) * inv_n
        cen = out - mean
        var = _head_sums(cen * cen, bd, 1) * inv_n
        normed = cen * lax.rsqrt(var + RW_LNX_EPS) * lw_ref[...] + lb_ref[...]
        bonus = _head_sums(r * k_eff * rk_ref[...], bd, 2) * v
        o_ref[bi] = ((normed + bonus) * g).astype(o_ref.dtype)


def _rwkv(u_rw, p, B, S, vres=None):
    T = u_rw.shape[0]
    C = RW_CHUNK
    nb = RW_BATCH
    nc = S // C
    has_vres = vres is not None
    blk = lambda w: pl.BlockSpec((nb, C, w), lambda b, c: (b, c, 0))
    vec = lambda n: _const_spec((1, n))
    in_specs = [blk(RW_COLS)]
    args = [u_rw.reshape(B, S, RW_COLS)]
    if has_vres:
        in_specs += [blk(LORA_PAD), blk(BRANCH)]
        args += [vres["vm"].reshape(B, S, LORA_PAD), vres["v_first"].reshape(B, S, BRANCH)]
    in_specs += [vec(RW_COLS), vec(BRANCH), _const_spec((LORA_PAD, BRANCH)), vec(BRANCH),
                 _const_spec((LORA_PAD, BRANCH)), _const_spec((LORA_PAD, BRANCH)),
                 vec(BRANCH), vec(BRANCH), vec(BRANCH), vec(BRANCH), vec(BRANCH),
                 _const_spec((MXU_TILE, MXU_TILE))]
    args += [p["mu"], p["w0"], p["w_up"], p["a0"], p["a_up"], p["g_up"], p["k_k"], p["k_a"],
             p["r_k"], p["lnx_w"], p["lnx_b"], p["bd"]]
    scratch = [pltpu.VMEM((nb * (BRANCH // LANES), 2 * C, LANES), F32),
               pltpu.VMEM((nb, 8, RW_COLS), F32)]
    if has_vres:
        in_specs += [vec(LORA_PAD), vec(BRANCH), _const_spec((LORA_PAD, BRANCH))]
        args += [vres["mu"], vres["v0"], vres["up"]]
        out_shape = jax.ShapeDtypeStruct((B, S, BRANCH), BF16)
        out_specs = blk(BRANCH)
        scratch.append(pltpu.VMEM((nb, 8, LORA_PAD), F32))
    else:
        out_shape = (jax.ShapeDtypeStruct((B, S, BRANCH), BF16),
                     jax.ShapeDtypeStruct((B, S, BRANCH), F32))
        out_specs = (blk(BRANCH), blk(BRANCH))
    scratch.append(pltpu.VMEM((nb, C, BRANCH), F32))
    outs = pl.pallas_call(
        functools.partial(_rwkv_kernel, has_vres, nb),
        out_shape=out_shape,
        grid=(B // nb, nc),
        in_specs=in_specs,
        out_specs=out_specs,
        scratch_shapes=scratch,
        compiler_params=_cparams("parallel", "arbitrary"),
        name="rwkv",
    )(*args)
    if has_vres:
        return outs.reshape(T, BRANCH)
    return outs[0].reshape(T, BRANCH), outs[1].reshape(T, BRANCH)


def _moba_kernel(q_ref, k_ref, v_ref, o_ref, kmean_ref, sel_ref):
    S = q_ref.shape[0]
    nb = S // MB_BLOCK
    tq = MB_BLOCK
    n_heads = LANES // HEAD_DIM
    for n in range(nb):
        blk = k_ref[n * MB_BLOCK:(n + 1) * MB_BLOCK, :].astype(F32)
        kmean_ref[n:n + 1, :] = jnp.mean(blk, axis=0, keepdims=True)
    kmean = kmean_ref[...]
    lane = lax.broadcasted_iota(jnp.int32, (1, LANES), 1)
    head_lanes = [(lane // HEAD_DIM) == hh for hh in range(n_heads)]

    blk_id = lax.broadcasted_iota(jnp.int32, (nb, S), 0)
    own = lax.broadcasted_iota(jnp.int32, (nb, S), 1) // MB_BLOCK
    valid = blk_id < own
    q_all = q_ref[...].astype(F32)
    for hh in range(n_heads):
        gate = _dot_nt(jnp.where(head_lanes[hh], kmean, 0.0), q_all, HI)
        cnt = jnp.zeros((nb, S), jnp.int32)
        for m in range(nb - 1):
            gm = gate[m:m + 1, :]
            beats = valid[m:m + 1, :] & ((gm > gate) | ((gm == gate) & (m < blk_id)))
            cnt = cnt + beats.astype(jnp.int32)
        sel = (valid & (cnt < MB_TOPK)).astype(F32)
        sel_ref[hh] = jnp.concatenate([sel, jnp.zeros((LANES - nb, S), F32)], axis=0).T

    qi = lax.broadcasted_iota(jnp.int32, (tq, tq), 0)
    ki = lax.broadcasted_iota(jnp.int32, (tq, tq), 1)
    causal = ki <= qi

    def scores(i):
        q_t = q_ref[i * tq:(i + 1) * tq, :]
        k_all = k_ref[0:(i + 1) * MB_BLOCK, :]
        return [_dot_nt(jnp.where(head_lanes[hh], q_t, jnp.zeros_like(q_t)), k_all)
                for hh in range(n_heads)]

    s_cur = scores(0)
    for i in range(nb):
        s_next = scores(i + 1) if i + 1 < nb else None
        probs = []
        for hh in range(n_heads):
            s = s_cur[hh]
            parts = []
            if i > 0:
                sel_t = sel_ref[hh, i * tq:(i + 1) * tq, :]
                for j in range(i):
                    keep = sel_t[:, j:j + 1] > 0.5
                    parts.append(jnp.where(keep, s[:, j * MB_BLOCK:(j + 1) * MB_BLOCK], MASK_VALUE))
            parts.append(jnp.where(causal, s[:, i * MB_BLOCK:], MASK_VALUE))
            s = parts[0] if len(parts) == 1 else jnp.concatenate(parts, axis=1)
            m_row = jnp.max(s, axis=1, keepdims=True)
            p = jnp.exp(s - m_row)
            probs.append((p.astype(BF16), jnp.sum(p, axis=1, keepdims=True)))
        v_all = v_ref[0:(i + 1) * MB_BLOCK, :]
        o_heads = [_dot(p, v_all) / denom for p, denom in probs]
        o_t = jnp.where(head_lanes[0], o_heads[0], o_heads[1])
        o_ref[i * tq:(i + 1) * tq, :] = o_t.astype(o_ref.dtype)
        s_cur = s_next


def _moba(q, k, v, B, S):
    T = q.shape[0]
    npair = BRANCH // LANES
    spec = pl.BlockSpec((S, LANES), lambda b, p: (b, p))
    return pl.pallas_call(
        _moba_kernel,
        out_shape=jax.ShapeDtypeStruct((T, BRANCH), BF16),
        grid=(B, npair),
        in_specs=[spec, spec, spec],
        out_specs=spec,
        scratch_shapes=[pltpu.VMEM((S // MB_BLOCK, LANES), F32),
                        pltpu.VMEM((LANES // HEAD_DIM, S, LANES), F32)],
        compiler_params=_cparams("parallel", "parallel"),
        name="moba",
    )(q, k, v)


def _swa_kernel(q_ref, k_ref, v_ref, sink_ref, o_ref):
    S = q_ref.shape[0]
    W = SW_WINDOW
    lane = lax.broadcasted_iota(jnp.int32, (1, LANES), 1)
    qi = lax.broadcasted_iota(jnp.int32, (W, 2 * W), 0) + W
    ki = lax.broadcasted_iota(jnp.int32, (W, 2 * W), 1)
    band = (qi - ki >= 0) & (qi - ki < W)
    qi0 = lax.broadcasted_iota(jnp.int32, (W, W), 0)
    ki0 = lax.broadcasted_iota(jnp.int32, (W, W), 1)
    causal = ki0 <= qi0

    head_lanes = [(lane // HEAD_DIM) == hh for hh in range(LANES // HEAD_DIM)]
    sinks = [sink_ref[0, hh:hh + 1, 0:1] for hh in range(LANES // HEAD_DIM)]
    group = 4

    for g0 in range(0, S // W, group):
        work = []
        for n in range(g0, g0 + group):
            k0 = max(n - 1, 0) * W
            k1 = (n + 1) * W
            mask = causal if n == 0 else band
            q_t = q_ref[n * W:(n + 1) * W, :]
            k_w = k_ref[k0:k1, :]
            for hh in range(LANES // HEAD_DIM):
                q_h = jnp.where(head_lanes[hh], q_t, jnp.zeros_like(q_t))
                work.append(dict(n=n, hh=hh, k0=k0, k1=k1, mask=mask, s=_dot_nt(q_h, k_w)))
        for d in work:
            s = jnp.where(d["mask"], d["s"], MASK_VALUE)
            sink = sinks[d["hh"]]
            m_row = jnp.maximum(jnp.max(s, axis=1, keepdims=True), sink)
            p = jnp.exp(s - m_row)
            d["denom"] = jnp.sum(p, axis=1, keepdims=True) + jnp.exp(sink - m_row)
            d["p"] = p.astype(BF16)
        for d in work:
            d["o"] = _dot(d["p"], v_ref[d["k0"]:d["k1"], :]) / d["denom"]
        for a, b in zip(work[0::2], work[1::2]):
            n = a["n"]
            o_ref[n * W:(n + 1) * W, :] = jnp.where(head_lanes[0], a["o"], b["o"]).astype(o_ref.dtype)


def _swa(q, k, v, sinks, B, S):
    T = q.shape[0]
    npair = BRANCH // LANES
    spec = pl.BlockSpec((S, LANES), lambda b, p: (b, p))
    kv_spec = pl.BlockSpec((S, LANES), lambda b, p: (b, 0))
    return pl.pallas_call(
        _swa_kernel,
        out_shape=jax.ShapeDtypeStruct((T, BRANCH), BF16),
        grid=(B, npair),
        in_specs=[spec, kv_spec, kv_spec, pl.BlockSpec((1, 2, LANES), lambda b, p: (p, 0, 0))],
        out_specs=spec,
        compiler_params=_cparams("parallel", "parallel"),
        name="swa",
    )(q, k, v, sinks)


def _mix_out_kernel(x_ref, orw_ref, omb_ref, osw_ref, gate_ref, wb_ref, wo_ref, g_ref, out_ref):
    y = None
    for n, o_ref in enumerate((orw_ref, omb_ref, osw_ref)):
        t = gate_ref[:, n * D_MODEL:(n + 1) * D_MODEL] * _dot(o_ref[...], wb_ref[n])
        y = t if y is None else y + t
    y = _dot(y.astype(BF16), wo_ref[...])
    out_ref[...] = x_ref[...] + _rms(y, g_ref[...])


def _mix_out(x2, o_rw, o_mb, o_sw, gates, w_branch, w_out, g):
    T = x2.shape[0]
    tm = 256
    row = lambda i: (i, 0)
    br = pl.BlockSpec((tm, BRANCH), row)
    return pl.pallas_call(
        _mix_out_kernel,
        out_shape=jax.ShapeDtypeStruct((T, D_MODEL), F32),
        grid=(T // tm,),
        in_specs=[pl.BlockSpec((tm, D_MODEL), row), br, br, br,
                  pl.BlockSpec((tm, 3 * D_MODEL), row),
                  _const_spec((3, BRANCH, D_MODEL)), _const_spec((D_MODEL, D_MODEL)),
                  _const_spec((1, D_MODEL))],
        out_specs=pl.BlockSpec((tm, D_MODEL), row),
        compiler_params=_cparams("parallel"),
        name="mix_out",
    )(x2, o_rw, o_mb, o_sw, gates, w_branch, w_out, g)


def _mem_kv_kernel(m_ref, g_ref, wk_ref, wv_ref, k_ref, v_ref):
    m = _rms(m_ref[...], g_ref[...]).astype(BF16)
    k_ref[...] = _dot(m, wk_ref[...]).astype(k_ref.dtype)
    v_ref[...] = _dot(m, wv_ref[...]).astype(v_ref.dtype)


def _mem_kv(mem2, g, wk, wv, M):
    R = mem2.shape[0]
    row = lambda i: (i, 0)
    return pl.pallas_call(
        _mem_kv_kernel,
        out_shape=(jax.ShapeDtypeStruct((R, XA_WIDTH), BF16),) * 2,
        grid=(R // M,),
        in_specs=[pl.BlockSpec((M, D_MODEL), row), _const_spec((1, D_MODEL)),
                  _const_spec((D_MODEL, XA_WIDTH)), _const_spec((D_MODEL, XA_WIDTH))],
        out_specs=(pl.BlockSpec((M, XA_WIDTH), row),) * 2,
        compiler_params=_cparams("parallel"),
        name="mem_kv",
    )(mem2, g, wk, wv)


def _xattn_kernel(x_ref, k_ref, v_ref, gpre_ref, wq_ref, wo_ref, gpost_ref, out_ref):
    x = x_ref[...]
    h = _rms(x, gpre_ref[...]).astype(BF16)
    q = (_dot(h, wq_ref[...]) * XA_HEAD_DIM ** -0.5).astype(BF16)
    outs = []
    for hd in range(XA_HEADS):
        sl = slice(hd * XA_HEAD_DIM, (hd + 1) * XA_HEAD_DIM)
        s = _dot_nt(q[:, sl], k_ref[:, sl])
        m_row = jnp.max(s, axis=1, keepdims=True)
        p = jnp.exp(s - m_row)
        p = p / jnp.sum(p, axis=1, keepdims=True)
        outs.append(_dot(p.astype(BF16), v_ref[:, sl]))
    o = jnp.concatenate(outs, axis=1).astype(BF16)
    out_ref[...] = x + _rms(_dot(o, wo_ref[...]), gpost_ref[...])


def _xattn(x2, k_mem, v_mem, g_pre, wq, wo, g_post, B, S, M):
    T = x2.shape[0]
    tm = 512
    nt = S // tm
    row = lambda b, i: (b * nt + i, 0)
    kv = pl.BlockSpec((M, XA_WIDTH), lambda b, i: (b, 0))
    return pl.pallas_call(
        _xattn_kernel,
        out_shape=jax.ShapeDtypeStruct((T, D_MODEL), F32),
        grid=(B, nt),
        in_specs=[pl.BlockSpec((tm, D_MODEL), row), kv, kv, _const_spec((1, D_MODEL)),
                  _const_spec((D_MODEL, XA_WIDTH)), _const_spec((XA_WIDTH, D_MODEL)),
                  _const_spec((1, D_MODEL))],
        out_specs=pl.BlockSpec((tm, D_MODEL), row),
        compiler_params=_cparams("parallel", "parallel"),
        name="xattn",
    )(x2, k_mem, v_mem, g_pre, wq, wo, g_post)


def _mlp_kernel(x_ref, gpre_ref, wup_ref, wdn_ref, gpost_ref, out_ref, h_ref, acc_ref):
    j = pl.program_id(1)

    @pl.when(j == 0)
    def _():
        h_ref[...] = _rms(x_ref[...], gpre_ref[...]).astype(BF16)
        acc_ref[...] = jnp.zeros_like(acc_ref)

    h = h_ref[...]
    n_sub = wup_ref.shape[1] // MLP_SUB

    def up(c):
        return _dot(h, wup_ref[:, c * MLP_SUB:(c + 1) * MLP_SUB])

    a_next = up(0)
    acc = None
    for c in range(n_sub):
        a = jnp.maximum(a_next, 0.0)
        if c + 1 < n_sub:
            a_next = up(c + 1)
        t = _dot((a * a).astype(BF16), wdn_ref[c * MLP_SUB:(c + 1) * MLP_SUB, :])
        acc = t if acc is None else acc + t
    acc_ref[...] += acc

    @pl.when(j == pl.num_programs(1) - 1)
    def _():
        out_ref[...] = x_ref[...] + _rms(acc_ref[...], gpost_ref[...])


def _mlp(x2, g_pre, w_up, w_down, g_post):
    T = x2.shape[0]
    tm = MLP_ROWS
    tf = MLP_COLS
    row = lambda i, j: (i, 0)
    return pl.pallas_call(
        _mlp_kernel,
        out_shape=jax.ShapeDtypeStruct((T, D_MODEL), F32),
        grid=(T // tm, D_FF // tf),
        in_specs=[pl.BlockSpec((tm, D_MODEL), row), _const_spec((1, D_MODEL)),
                  pl.BlockSpec((D_MODEL, tf), lambda i, j: (0, j)),
                  pl.BlockSpec((tf, D_MODEL), lambda i, j: (j, 0)),
                  _const_spec((1, D_MODEL))],
        out_specs=pl.BlockSpec((tm, D_MODEL), row),
        scratch_shapes=[pltpu.VMEM((tm, D_MODEL), BF16), pltpu.VMEM((tm, D_MODEL), F32)],
        compiler_params=_cparams("parallel", "arbitrary"),
        name="mlp",
    )(x2, g_pre, w_up, w_down, g_post)


def _sw_head_order():
    per_kv = N_HEADS // SW_KV_HEADS
    return [h for p in range(per_kv) for h in (p, p + per_kv)]


def _head_cols(order):
    return jnp.concatenate([jnp.arange(h * HEAD_DIM, (h + 1) * HEAD_DIM) for h in order])


def _pad_rows(w, offset):
    out = jnp.zeros((LORA_PAD, w.shape[1]), w.dtype)
    return out.at[offset:offset + w.shape[0]].set(w)


def kernel(x, mem, positions, norm_mix_pre, norm_mix_post, norm_xattn_pre, norm_xattn_post, norm_mem, norm_mlp_pre, norm_mlp_post, w_in, rw_mu, rw_w0, rw_w_up, rw_a0, rw_a_up, rw_g_up, rw_k_k, rw_k_a, rw_r_k, rw_lnx_w, rw_lnx_b, rw_vres_down, rw_vres_mu, rw_v0, rw_vres_up, sw_sinks, w_branch, w_out, w_xq, w_xk, w_xv, w_xo, w_up, w_down):
    B, S, D = x.shape
    M = mem.shape[1]
    depth = w_in.shape[0]
    T = B * S
    x2 = x.reshape(T, D)
    mem2 = mem.reshape(B * M, D)
    cos_t, sin_t = _rope_tables(positions)

    sw_order = _sw_head_order()
    sw_cols = _head_cols(sw_order)
    head_of_col = jnp.arange(MXU_TILE) // HEAD_DIM
    bd = (head_of_col[:, None] == head_of_col[None, :]).astype(BF16)
    mb0 = RW_COLS
    sw0 = mb0 + 3 * BRANCH
    gt0 = sw0 + BRANCH + 2 * SW_KV_HEADS * HEAD_DIM
    vec = lambda t: t.reshape(1, -1)

    v_first = None
    for l in range(depth):
        wl = w_in[l]
        segs = [wl[:, :RW_COLS]]
        if l > 0:
            segs.append(jnp.pad(rw_vres_down[l - 1], ((0, 0), (0, LORA_PAD - rw_vres_down.shape[2]))))
        segs += [wl[:, mb0:sw0], wl[:, sw0:sw0 + BRANCH][:, sw_cols], wl[:, sw0 + BRANCH:gt0],
                 wl[:, gt0:]]
        w_cat = jnp.concatenate(segs, axis=1).astype(BF16)
        outs = _mix_in(x2, vec(norm_mix_pre[l]), w_cat, cos_t, sin_t, l > 0)
        if l > 0:
            u_rw, vm, mq, mk, mv, sq, sk, sv, gates = outs
        else:
            u_rw, mq, mk, mv, sq, sk, sv, gates = outs

        rw_p = dict(
            mu=vec(rw_mu[l]), w0=vec(rw_w0[l]), w_up=_pad_rows(rw_w_up[l], 0).astype(BF16),
            a0=vec(rw_a0[l]), a_up=_pad_rows(rw_a_up[l], rw_w_up.shape[1]).astype(BF16),
            g_up=rw_g_up[l].astype(BF16), k_k=vec(rw_k_k[l]), k_a=vec(rw_k_a[l]),
            r_k=vec(rw_r_k[l]), lnx_w=vec(rw_lnx_w[l]), lnx_b=vec(rw_lnx_b[l]), bd=bd)
        if l == 0:
            o_rw, v_first = _rwkv(u_rw, rw_p, B, S)
        else:
            vres = dict(vm=vm, v_first=v_first,
                        mu=jnp.pad(vec(rw_vres_mu[l - 1]), ((0, 0), (0, LORA_PAD - rw_vres_mu.shape[1]))),
                        v0=vec(rw_v0[l - 1]), up=_pad_rows(rw_vres_up[l - 1], 0).astype(BF16))
            o_rw = _rwkv(u_rw, rw_p, B, S, vres)

        o_mb = _moba(mq, mk, mv, B, S)
        sinks = jnp.broadcast_to(sw_sinks[l][jnp.array(sw_order)].reshape(-1, 2, 1),
                                 (BRANCH // LANES, 2, LANES)).astype(F32)
        o_sw = _swa(sq, sk, sv, sinks, B, S)

        wb = jnp.stack([w_branch[l, 0], w_branch[l, 1], w_branch[l, 2][sw_cols]]).astype(BF16)
        x2 = _mix_out(x2, o_rw, o_mb, o_sw, gates, wb, w_out[l].astype(BF16), vec(norm_mix_post[l]))

        k_mem, v_mem = _mem_kv(mem2, vec(norm_mem[l]), w_xk[l].astype(BF16), w_xv[l].astype(BF16), M)
        x2 = _xattn(x2, k_mem, v_mem, vec(norm_xattn_pre[l]), w_xq[l].astype(BF16),
                    w_xo[l].astype(BF16), vec(norm_xattn_post[l]), B, S, M)

        x2 = _mlp(x2, vec(norm_mlp_pre[l]), w_up[l].astype(BF16), w_down[l].astype(BF16),
                  vec(norm_mlp_post[l]))
    return x2.reshape(B, S, D)
```

```python
import functools

import jax
import jax.numpy as jnp
from jax import lax
from jax.experimental import pallas as pl
from jax.experimental.pallas import tpu as pltpu

F32 = jnp.float32
BF16 = jnp.bfloat16

D_MODEL = 1024
HEAD_DIM = 64
BRANCH = 512
N_HEADS = BRANCH // HEAD_DIM
RW_COLS = 3 * BRANCH + 64 + 64 + 128
LORA_PAD = 128
MB_BLOCK = 256
MB_TOPK = 3
SW_WINDOW = 128
SW_KV_HEADS = 2
XA_HEADS = 4
XA_HEAD_DIM = 128
XA_WIDTH = XA_HEADS * XA_HEAD_DIM
D_FF = 4 * D_MODEL
ROPE_THETA = 10000.0
NORM_EPS = 1e-6
RW_LNX_EPS = 1e-5 * HEAD_DIM
MASK_VALUE = -1e30
LOG2E = 1.4426950408889634

LANES = 128
MXU_TILE = 256
RW_CHUNK = 64
MLP_ROWS = 1024
MLP_COLS = 2048
MLP_SUB = 512
RW_BATCH = 4
VMEM_LIMIT = 56 * 1024 * 1024


def _cparams(*sem):
    return pltpu.CompilerParams(dimension_semantics=sem, vmem_limit_bytes=VMEM_LIMIT)


def _dot(a, b, precision=None):
    return jnp.dot(a, b, preferred_element_type=F32, precision=precision)


def _dot_nt(a, b, precision=None):
    return lax.dot_general(a, b, (((1,), (1,)), ((), ())),
                           preferred_element_type=F32, precision=precision)


def _dot_tn(a, b, precision=None):
    return lax.dot_general(a, b, (((0,), (0,)), ((), ())),
                           preferred_element_type=F32, precision=precision)


def _rms(xf, g):
    ms = jnp.mean(xf * xf, axis=-1, keepdims=True)
    return xf * lax.rsqrt(ms + NORM_EPS) * g


def _const_spec(shape):
    nd = len(shape)
    return pl.BlockSpec(shape, lambda *_: (0,) * nd, pipeline_mode=pl.Buffered(1))


def _rope_kernel(pos_ref, invf_ref, sign_ref, cos_ref, sin_ref):
    ang = pos_ref[...].astype(F32) * invf_ref[...]
    cos_ref[...] = jnp.cos(ang)
    sin_ref[...] = jnp.sin(ang) * sign_ref[...]


def _rope_tables(positions):
    T = positions.size
    tm = 2048
    inv_freq = 1.0 / (ROPE_THETA ** (jnp.arange(0, HEAD_DIM, 2, dtype=F32) / HEAD_DIM))
    invf = jnp.tile(inv_freq, LANES // (HEAD_DIM // 2))[None, :]
    half = jnp.concatenate([-jnp.ones((HEAD_DIM // 2,), F32), jnp.ones((HEAD_DIM // 2,), F32)])
    sign = jnp.tile(half, LANES // HEAD_DIM)[None, :]
    return pl.pallas_call(
        _rope_kernel,
        out_shape=(jax.ShapeDtypeStruct((T, LANES), F32),) * 2,
        grid=(T // tm,),
        in_specs=[pl.BlockSpec((tm, 1), lambda i: (i, 0)),
                  _const_spec((1, LANES)), _const_spec((1, LANES))],
        out_specs=(pl.BlockSpec((tm, LANES), lambda i: (i, 0)),) * 2,
        compiler_params=_cparams("parallel"),
        name="rope_tables",
    )(positions.reshape(T, 1), invf, sign)


def _rope_tile(t, cos, sin, lane_lt_half):
    fwd = pltpu.roll(t, LANES - HEAD_DIM // 2, axis=1)
    bwd = pltpu.roll(t, HEAD_DIM // 2, axis=1)
    partner = jnp.where(lane_lt_half, fwd, bwd)
    return t * cos + partner * sin


def _mix_in_kernel(has_vm, x_ref, g_ref, w_ref, cos_ref, sin_ref, *outs):
    if has_vm:
        urw_ref, vm_ref, mq_ref, mk_ref, mv_ref, sq_ref, sk_ref, sv_ref, gate_ref = outs
    else:
        urw_ref, mq_ref, mk_ref, mv_ref, sq_ref, sk_ref, sv_ref, gate_ref = outs
        vm_ref = None
    h = _rms(x_ref[...], g_ref[...]).astype(BF16)
    cos = cos_ref[...]
    sin = sin_ref[...]
    lane = lax.broadcasted_iota(jnp.int32, cos.shape, 1)
    lt_half = (lane % HEAD_DIM) < (HEAD_DIM // 2)

    col = [0]

    def seg(width):
        a = col[0]
        col[0] += width
        return _dot(h, w_ref[:, a:a + width])

    def rope_store(ref, width, scale):
        acc = seg(width)
        for c in range(width // LANES):
            t = _rope_tile(acc[:, c * LANES:(c + 1) * LANES], cos, sin, lt_half)
            ref[:, c * LANES:(c + 1) * LANES] = (t * scale).astype(ref.dtype)

    urw_ref[...] = seg(RW_COLS).astype(urw_ref.dtype)
    if has_vm:
        vm_ref[...] = seg(LORA_PAD)
    rope_store(mq_ref, BRANCH, LOG2E * HEAD_DIM ** -0.5)
    rope_store(mk_ref, BRANCH, 1.0)
    mv_ref[...] = seg(BRANCH).astype(mv_ref.dtype)
    rope_store(sq_ref, BRANCH, LOG2E * HEAD_DIM ** -0.5)
    rope_store(sk_ref, SW_KV_HEADS * HEAD_DIM, 1.0)
    sv_ref[...] = seg(SW_KV_HEADS * HEAD_DIM).astype(sv_ref.dtype)
    for n in range(3):
        gate_ref[:, n * D_MODEL:(n + 1) * D_MODEL] = jax.nn.sigmoid(seg(D_MODEL)).astype(gate_ref.dtype)


def _mix_in(x2, g, w_cat, cos_t, sin_t, has_vm):
    T = x2.shape[0]
    tm = 512
    ntot = w_cat.shape[1]
    kv_w = SW_KV_HEADS * HEAD_DIM
    widths = [(RW_COLS, BF16)]
    if has_vm:
        widths.append((LORA_PAD, F32))
    widths += [(BRANCH, BF16), (BRANCH, BF16), (BRANCH, BF16), (BRANCH, BF16),
               (kv_w, BF16), (kv_w, BF16), (3 * D_MODEL, BF16)]
    assert sum(w for w, _ in widths) == ntot
    row = lambda i: (i, 0)
    return pl.pallas_call(
        functools.partial(_mix_in_kernel, has_vm),
        out_shape=tuple(jax.ShapeDtypeStruct((T, w), dt) for w, dt in widths),
        grid=(T // tm,),
        in_specs=[pl.BlockSpec((tm, D_MODEL), row), _const_spec((1, D_MODEL)),
                  _const_spec((D_MODEL, ntot)),
                  pl.BlockSpec((tm, LANES), row), pl.BlockSpec((tm, LANES), row)],
        out_specs=tuple(pl.BlockSpec((tm, w), row) for w, _ in widths),
        compiler_params=_cparams("parallel"),
        name="mix_in",
    )(x2, g, w_cat, cos_t, sin_t)


def _shift_prev(cur, carry_ref, bi):
    rolled = pltpu.roll(cur, 1, axis=0)
    row = lax.broadcasted_iota(jnp.int32, cur.shape, 0)
    prev = jnp.where(row == 0, carry_ref[bi, 0:1, :], rolled)
    carry_ref[bi, 0:1, :] = cur[cur.shape[0] - 1:, :]
    return prev


def _split_bf16(x, terms):
    pieces = []
    for _ in range(terms):
        hi = x.astype(BF16)
        pieces.append(hi)
        x = x - hi.astype(F32)
    return pieces


def _head_sums(x, bd, terms):
    w = bd.shape[0]
    halves = []
    for c in range(x.shape[1] // w):
        pieces = _split_bf16(x[:, c * w:(c + 1) * w], terms)
        halves.append(sum(_dot(piece, bd) for piece in pieces))
    return jnp.concatenate(halves, axis=1)


def _dot_split_rhs(a, b, terms):
    return sum(_dot(a, piece) for piece in _split_bf16(b, terms))


def _rwkv_kernel(has_vres, nb, *refs):
    if has_vres:
        (u_ref, vm_ref, vf_ref, mu_ref, w0_ref, wup_ref, a0_ref, aup_ref, gup_ref, kk_ref,
         ka_ref, rk_ref, lw_ref, lb_ref, bd_ref, vmu_ref, v0_ref, vup_ref,
         o_ref, s_ref, cu_ref, cvm_ref, hout_ref) = refs
        vout_ref = None
    else:
        (u_ref, mu_ref, w0_ref, wup_ref, a0_ref, aup_ref, gup_ref, kk_ref,
         ka_ref, rk_ref, lw_ref, lb_ref, bd_ref,
         o_ref, vout_ref, s_ref, cu_ref, hout_ref) = refs
    C = RW_CHUNK
    P2 = 2 * C
    npair = BRANCH // LANES

    @pl.when(pl.program_id(1) == 0)
    def _():
        s_ref[...] = jnp.zeros_like(s_ref)
        cu_ref[...] = jnp.zeros_like(cu_ref)
        if has_vres:
            cvm_ref[...] = jnp.zeros_like(cvm_ref)

    bd = bd_ref[...]
    ti = lax.broadcasted_iota(jnp.int32, (C, C), 0)
    si = lax.broadcasted_iota(jnp.int32, (C, C), 1)
    tri = (si <= ti).astype(BF16)
    lane = lax.broadcasted_iota(jnp.int32, (1, LANES), 1)
    first_head = lane < HEAD_DIM
    pr = lax.broadcasted_iota(jnp.int32, (P2, P2), 0)
    pc = lax.broadcasted_iota(jnp.int32, (P2, P2), 1)
    same_head = (pr // C) == (pc // C)
    strict = same_head & ((pc % C) < (pr % C))
    incl = same_head & ((pc % C) <= (pr % C))
    eye = (pr == pc).astype(F32)

    def stack(t):
        return jnp.concatenate([jnp.where(first_head, t, 0.0), jnp.where(first_head, 0.0, t)], axis=0)

    pre = []
    inst = []
    for bi in range(nb):
        u = u_ref[bi].astype(F32)
        ul = u + mu_ref[...] * (_shift_prev(u, cu_ref, bi) - u)
        r = ul[:, 0:BRANCH]
        k = ul[:, BRANCH:2 * BRANCH]
        v = ul[:, 2 * BRANCH:3 * BRANCH]
        x_wa = ul[:, 3 * BRANCH:3 * BRANCH + LANES]
        x_g = ul[:, 3 * BRANCH + LANES:]

        z = w0_ref[...] + _dot(jnp.tanh(x_wa).astype(BF16), wup_ref[...])
        a = jax.nn.sigmoid(a0_ref[...] + _dot(x_wa.astype(BF16), aup_ref[...]))
        g = _dot(jax.nn.sigmoid(x_g).astype(BF16), gup_ref[...])
        softplus = jnp.maximum(-z, 0.0) + jnp.log1p(jnp.exp(-jnp.abs(z)))
        ell = -jnp.exp(-softplus - 0.5)

        if has_vres:
            vm = vm_ref[bi]
            vml = vm + vmu_ref[...] * (_shift_prev(vm, cvm_ref, bi) - vm)
            vgate = jax.nn.sigmoid(v0_ref[...] + _dot(vml.astype(BF16), vup_ref[...]))
            v = v + (vf_ref[bi] - v) * vgate
        else:
            vout_ref[bi] = v

        kk = k * kk_ref[...]
        kk = kk / jnp.maximum(jnp.sqrt(_head_sums(kk * kk, bd, 1)), 1e-12)
        k_eff = k * (1.0 + (a - 1.0) * ka_ref[...])
        b = kk * a
        cum = _dot_split_rhs(tri, ell, 3)
        cum_last = cum[C - 1:, :]
        g_inv = jnp.exp(-cum)
        g_tail = jnp.exp(cum_last - cum)
        kap_t = kk * jnp.exp(cum - ell)
        r_t = r * jnp.exp(cum)
        b_t = b * g_inv
        k_t = k_eff * g_inv
        b_g = b * g_tail
        k_g = k_eff * g_tail
        g_all = jnp.exp(cum_last)
        pre.append((r, k_eff, v, g))
        for p in range(npair):
            sl = slice(p * LANES, (p + 1) * LANES)
            inst.append(dict(bi=bi, p=p, sl=sl, kap=stack(kap_t[:, sl]), r=stack(r_t[:, sl]),
                             b=stack(b_t[:, sl]), k=stack(k_t[:, sl]), v=stack(v[:, sl]),
                             bg=stack(b_g[:, sl]), kg=stack(k_g[:, sl]), gall=g_all[:, sl]))

    def mm(a, b):
        return _dot(a.astype(BF16), b.astype(BF16))

    def mm_nt(a, b):
        return _dot_nt(a.astype(BF16), b.astype(BF16))

    def mm_tn(a, b):
        return _dot_tn(a.astype(BF16), b.astype(BF16))

    for d in inst:
        aa = mm_nt(jnp.concatenate([d["kap"], d["r"]], axis=0),
                   jnp.concatenate([d["b"], d["k"]], axis=0))
        d["n"] = jnp.where(strict, -aa[:P2, :P2], 0.0)
        d["a_ak"] = jnp.where(strict, aa[:P2, P2:], 0.0)
        d["a_r"] = jnp.where(jnp.concatenate([incl, incl], axis=1), aa[P2:, :], 0.0)
    for d in inst:
        d["akv"] = mm(d["a_ak"], d["v"])
    for d in inst:
        d["t"] = eye + d["n"]
        d["pw"] = mm(d["n"], d["n"])
    for _ in range(4):
        for d in inst:
            pt = mm(d["pw"], jnp.concatenate([d["pw"], d["t"]], axis=1))
            d["pw"] = pt[:, :P2]
            d["t"] = d["t"] + pt[:, P2:]
    for d in inst:
        d["t"] = d["t"] + mm(d["pw"], d["t"])
    for d in inst:
        ty = mm(d["t"], jnp.concatenate([d["kap"], d["akv"]], axis=1))
        d["w_til"], d["u_til"] = ty[:, :LANES], ty[:, LANES:]
    for d in inst:
        d["state"] = s_ref[d["bi"] * npair + d["p"]]
        xr = mm_nt(jnp.concatenate([d["w_til"], d["r"]], axis=0), d["state"])
        sa = -(xr[:P2] + d["u_til"])
        d["sv"] = jnp.concatenate([sa, d["v"]], axis=0)
        d["rs"] = xr[P2:]
    for d in inst:
        s_ref[d["bi"] * npair + d["p"]] = (
            d["state"] * d["gall"] + mm_tn(d["sv"], jnp.concatenate([d["bg"], d["kg"]], axis=0)))
    for d in inst:
        o_st = d["rs"] + mm(d["a_r"], d["sv"])
        hout_ref[d["bi"], :, d["sl"]] = o_st[:C] + o_st[C:]

    inv_n = 1.0 / HEAD_DIM
    for bi in range(nb):
        r, k_eff, v, g = pre[bi]
        out = hout_ref[bi]
        mean = _head_sums(out, bd, 2) * inv_n
        cen = out - mean
        var = _head_sums(cen * cen, bd, 1) * inv_n
        normed = cen * lax.rsqrt(var + RW_LNX_EPS) * lw_ref[...] + lb_ref[...]
        bonus = _head_sums(r * k_eff * rk_ref[...], bd, 2) * v
        o_ref[bi] = ((normed + bonus) * g).astype(o_ref.dtype)


def _rwkv(u_rw, p, B, S, vres=None):
    T = u_rw.shape[0]
    C = RW_CHUNK
    nb = RW_BATCH
    nc = S // C
    has_vres = vres is not None
    blk = lambda w: pl.BlockSpec((nb, C, w), lambda b, c: (b, c, 0))
    vec = lambda n: _const_spec((1, n))
    in_specs = [blk(RW_COLS)]
    args = [u_rw.reshape(B, S, RW_COLS)]
    if has_vres:
        in_specs += [blk(LORA_PAD), blk(BRANCH)]
        args += [vres["vm"].reshape(B, S, LORA_PAD), vres["v_first"].reshape(B, S, BRANCH)]
    in_specs += [vec(RW_COLS), vec(BRANCH), _const_spec((LORA_PAD, BRANCH)), vec(BRANCH),
                 _const_spec((LORA_PAD, BRANCH)), _const_spec((LORA_PAD, BRANCH)),
                 vec(BRANCH), vec(BRANCH), vec(BRANCH), vec(BRANCH), vec(BRANCH),
                 _const_spec((MXU_TILE, MXU_TILE))]
    args += [p["mu"], p["w0"], p["w_up"], p["a0"], p["a_up"], p["g_up"], p["k_k"], p["k_a"],
             p["r_k"], p["lnx_w"], p["lnx_b"], p["bd"]]
    scratch = [pltpu.VMEM((nb * (BRANCH // LANES), 2 * C, LANES), F32),
               pltpu.VMEM((nb, 8, RW_COLS), F32)]
    if has_vres:
        in_specs += [vec(LORA_PAD), vec(BRANCH), _const_spec((LORA_PAD, BRANCH))]
        args += [vres["mu"], vres["v0"], vres["up"]]
        out_shape = jax.ShapeDtypeStruct((B, S, BRANCH), BF16)
        out_specs = blk(BRANCH)
        scratch.append(pltpu.VMEM((nb, 8, LORA_PAD), F32))
    else:
        out_shape = (jax.ShapeDtypeStruct((B, S, BRANCH), BF16),
                     jax.ShapeDtypeStruct((B, S, BRANCH), F32))
        out_specs = (blk(BRANCH), blk(BRANCH))
    scratch.append(pltpu.VMEM((nb, C, BRANCH), F32))
    outs = pl.pallas_call(
        functools.partial(_rwkv_kernel, has_vres, nb),
        out_shape=out_shape,
        grid=(B // nb, nc),
        in_specs=in_specs,
        out_specs=out_specs,
        scratch_shapes=scratch,
        compiler_params=_cparams("parallel", "arbitrary"),
        name="rwkv",
    )(*args)
    if has_vres:
        return outs.reshape(T, BRANCH)
    return outs[0].reshape(T, BRANCH), outs[1].reshape(T, BRANCH)


def _moba_kernel(q_ref, k_ref, v_ref, o_ref, kmean_ref, vt_ref):
    S = q_ref.shape[0]
    nb = S // MB_BLOCK
    tq = MB_BLOCK
    n_heads = LANES // HEAD_DIM
    for n in range(nb):
        blk = k_ref[n * MB_BLOCK:(n + 1) * MB_BLOCK, :].astype(F32)
        kmean_ref[n:n + 1, :] = jnp.mean(blk, axis=0, keepdims=True)
    kmean = kmean_ref[...]
    vt_ref[...] = v_ref[...].astype(F32).T.astype(BF16)
    lane = lax.broadcasted_iota(jnp.int32, (1, LANES), 1)
    head_lanes = [(lane // HEAD_DIM) == hh for hh in range(n_heads)]

    blk_id = lax.broadcasted_iota(jnp.int32, (nb, S), 0)
    own = lax.broadcasted_iota(jnp.int32, (nb, S), 1) // MB_BLOCK
    valid = blk_id < own
    q_all = q_ref[...]
    kmean_rep = jnp.concatenate(
        [kmean] + [jnp.broadcast_to(kmean[m:m + 1, :], (nb, LANES)) for m in range(nb - 1)], axis=0)
    sels = []
    for hh in range(n_heads):
        pieces = _split_bf16(jnp.where(head_lanes[hh], kmean_rep, 0.0), 3)
        gates = sum(_dot_nt(piece, q_all) for piece in pieces)
        gate = gates[:nb]
        cnt = jnp.zeros((nb, S), jnp.int32)
        for m in range(nb - 1):
            gm = gates[nb * (m + 1):nb * (m + 2)]
            beats = (own > m) & ((gm > gate) | ((gm == gate) & (m < blk_id)))
            cnt = cnt + beats.astype(jnp.int32)
        sels.append((valid & (cnt < MB_TOPK)).astype(F32))

    ki = lax.broadcasted_iota(jnp.int32, (tq, tq), 0)
    qi = lax.broadcasted_iota(jnp.int32, (tq, tq), 1)
    causal = ki <= qi

    def scores(i):
        q_t = q_ref[i * tq:(i + 1) * tq, :]
        k_all = k_ref[0:(i + 1) * MB_BLOCK, :]
        return [_dot_nt(k_all, jnp.where(head_lanes[hh], q_t, jnp.zeros_like(q_t)))
                for hh in range(n_heads)]

    s_cur = scores(0)
    for i in range(nb):
        s_next = scores(i + 1) if i + 1 < nb else None
        probs = []
        for hh in range(n_heads):
            s = s_cur[hh]
            parts = []
            for j in range(i):
                keep = sels[hh][j:j + 1, i * tq:(i + 1) * tq] > 0.5
                parts.append(jnp.where(keep, s[j * MB_BLOCK:(j + 1) * MB_BLOCK, :], MASK_VALUE))
            parts.append(jnp.where(causal, s[i * MB_BLOCK:, :], MASK_VALUE))
            s = parts[0] if len(parts) == 1 else jnp.concatenate(parts, axis=0)
            p = jnp.exp2(s - jnp.max(s, axis=0, keepdims=True))
            probs.append((p.astype(BF16), jnp.sum(p, axis=0, keepdims=True)))
        vt = vt_ref[:, 0:(i + 1) * MB_BLOCK]
        ot = [_dot(vt, p) / denom for p, denom in probs]
        o_t = jnp.concatenate([ot[0][:HEAD_DIM], ot[1][HEAD_DIM:]], axis=0).T
        o_ref[i * tq:(i + 1) * tq, :] = o_t.astype(o_ref.dtype)
        s_cur = s_next


def _moba(q, k, v, B, S):
    T = q.shape[0]
    npair = BRANCH // LANES
    spec = pl.BlockSpec((S, LANES), lambda b, p: (b, p))
    return pl.pallas_call(
        _moba_kernel,
        out_shape=jax.ShapeDtypeStruct((T, BRANCH), BF16),
        grid=(B, npair),
        in_specs=[spec, spec, spec],
        out_specs=spec,
        scratch_shapes=[pltpu.VMEM((S // MB_BLOCK, LANES), F32), pltpu.VMEM((LANES, S), BF16)],
        compiler_params=_cparams("parallel", "parallel"),
        name="moba",
    )(q, k, v)


def _swa_kernel(q_ref, k_ref, v_ref, sink_ref, o_ref, vt_ref):
    S = q_ref.shape[0]
    W = SW_WINDOW
    n_heads = LANES // HEAD_DIM
    vt_ref[...] = v_ref[...].astype(F32).T.astype(BF16)
    lane = lax.broadcasted_iota(jnp.int32, (1, LANES), 1)
    dist = (lax.broadcasted_iota(jnp.int32, (2 * W, W), 1) + W
            - lax.broadcasted_iota(jnp.int32, (2 * W, W), 0))
    band = (dist >= 0) & (dist < W)
    causal = (lax.broadcasted_iota(jnp.int32, (W, W), 0)
              <= lax.broadcasted_iota(jnp.int32, (W, W), 1))
    head_lanes = [(lane // HEAD_DIM) == hh for hh in range(n_heads)]
    sinks = [sink_ref[0, hh:hh + 1, 0:1] * LOG2E for hh in range(n_heads)]
    group = 4

    for g0 in range(0, S // W, group):
        work = []
        for n in range(g0, g0 + group):
            k0 = max(n - 1, 0) * W
            k1 = (n + 1) * W
            mask = causal if n == 0 else band
            q_t = q_ref[n * W:(n + 1) * W, :]
            k_w = k_ref[k0:k1, :]
            for hh in range(n_heads):
                q_h = jnp.where(head_lanes[hh], q_t, jnp.zeros_like(q_t))
                work.append(dict(n=n, hh=hh, k0=k0, k1=k1, mask=mask, s=_dot_nt(k_w, q_h)))
        for d in work:
            s = jnp.where(d["mask"], d["s"], MASK_VALUE)
            sink = sinks[d["hh"]]
            m_col = jnp.maximum(jnp.max(s, axis=0, keepdims=True), sink)
            p = jnp.exp2(s - m_col)
            d["denom"] = jnp.sum(p, axis=0, keepdims=True) + jnp.exp2(sink - m_col)
            d["p"] = p.astype(BF16)
        for d in work:
            d["ot"] = _dot(vt_ref[:, d["k0"]:d["k1"]], d["p"]) / d["denom"]
        o_t = jnp.concatenate(
            [jnp.concatenate([a["ot"][:HEAD_DIM], b["ot"][HEAD_DIM:]], axis=0)
             for a, b in zip(work[0::2], work[1::2])], axis=1)
        o_ref[g0 * W:(g0 + group) * W, :] = o_t.T.astype(o_ref.dtype)


def _swa(q, k, v, sinks, B, S):
    T = q.shape[0]
    npair = BRANCH // LANES
    spec = pl.BlockSpec((S, LANES), lambda b, p: (b, p))
    kv_spec = pl.BlockSpec((S, LANES), lambda b, p: (b, 0))
    return pl.pallas_call(
        _swa_kernel,
        out_shape=jax.ShapeDtypeStruct((T, BRANCH), BF16),
        grid=(B, npair),
        in_specs=[spec, kv_spec, kv_spec, pl.BlockSpec((1, 2, LANES), lambda b, p: (p, 0, 0))],
        out_specs=spec,
        scratch_shapes=[pltpu.VMEM((LANES, S), BF16)],
        compiler_params=_cparams("parallel", "parallel"),
        name="swa",
    )(q, k, v, sinks)


def _mix_out_kernel(x_ref, orw_ref, omb_ref, osw_ref, gate_ref, wb_ref, wo_ref, g_ref, out_ref):
    y = None
    for n, o_ref in enumerate((orw_ref, omb_ref, osw_ref)):
        t = gate_ref[:, n * D_MODEL:(n + 1) * D_MODEL] * _dot(o_ref[...], wb_ref[n])
        y = t if y is None else y + t
    y = _dot(y.astype(BF16), wo_ref[...])
    out_ref[...] = x_ref[...] + _rms(y, g_ref[...])


def _mix_out(x2, o_rw, o_mb, o_sw, gates, w_branch, w_out, g):
    T = x2.shape[0]
    tm = 256
    row = lambda i: (i, 0)
    br = pl.BlockSpec((tm, BRANCH), row)
    return pl.pallas_call(
        _mix_out_kernel,
        out_shape=jax.ShapeDtypeStruct((T, D_MODEL), F32),
        grid=(T // tm,),
        in_specs=[pl.BlockSpec((tm, D_MODEL), row), br, br, br,
                  pl.BlockSpec((tm, 3 * D_MODEL), row),
                  _const_spec((3, BRANCH, D_MODEL)), _const_spec((D_MODEL, D_MODEL)),
                  _const_spec((1, D_MODEL))],
        out_specs=pl.BlockSpec((tm, D_MODEL), row),
        compiler_params=_cparams("parallel"),
        name="mix_out",
    )(x2, o_rw, o_mb, o_sw, gates, w_branch, w_out, g)


def _mem_kv_kernel(m_ref, g_ref, wk_ref, wv_ref, k_ref, v_ref):
    m = _rms(m_ref[...], g_ref[...]).astype(BF16)
    k_ref[...] = _dot(m, wk_ref[...]).astype(k_ref.dtype)
    v_ref[...] = _dot(m, wv_ref[...]).astype(v_ref.dtype)


def _mem_kv(mem2, g, wk, wv, M):
    R = mem2.shape[0]
    row = lambda i: (i, 0)
    return pl.pallas_call(
        _mem_kv_kernel,
        out_shape=(jax.ShapeDtypeStruct((R, XA_WIDTH), BF16),) * 2,
        grid=(R // M,),
        in_specs=[pl.BlockSpec((M, D_MODEL), row), _const_spec((1, D_MODEL)),
                  _const_spec((D_MODEL, XA_WIDTH)), _const_spec((D_MODEL, XA_WIDTH))],
        out_specs=(pl.BlockSpec((M, XA_WIDTH), row),) * 2,
        compiler_params=_cparams("parallel"),
        name="mem_kv",
    )(mem2, g, wk, wv)


def _xattn_kernel(x_ref, k_ref, v_ref, gpre_ref, wq_ref, wo_ref, gpost_ref, out_ref):
    x = x_ref[...]
    h = _rms(x, gpre_ref[...]).astype(BF16)
    q = (_dot(h, wq_ref[...]) * XA_HEAD_DIM ** -0.5).astype(BF16)
    outs = []
    for hd in range(XA_HEADS):
        sl = slice(hd * XA_HEAD_DIM, (hd + 1) * XA_HEAD_DIM)
        s = _dot_nt(q[:, sl], k_ref[:, sl])
        m_row = jnp.max(s, axis=1, keepdims=True)
        p = jnp.exp(s - m_row)
        p = p / jnp.sum(p, axis=1, keepdims=True)
        outs.append(_dot(p.astype(BF16), v_ref[:, sl]))
    o = jnp.concatenate(outs, axis=1).astype(BF16)
    out_ref[...] = x + _rms(_dot(o, wo_ref[...]), gpost_ref[...])


def _xattn(x2, k_mem, v_mem, g_pre, wq, wo, g_post, B, S, M):
    T = x2.shape[0]
    tm = 512
    nt = S // tm
    row = lambda b, i: (b * nt + i, 0)
    kv = pl.BlockSpec((M, XA_WIDTH), lambda b, i: (b, 0))
    return pl.pallas_call(
        _xattn_kernel,
        out_shape=jax.ShapeDtypeStruct((T, D_MODEL), F32),
        grid=(B, nt),
        in_specs=[pl.BlockSpec((tm, D_MODEL), row), kv, kv, _const_spec((1, D_MODEL)),
                  _const_spec((D_MODEL, XA_WIDTH)), _const_spec((XA_WIDTH, D_MODEL)),
                  _const_spec((1, D_MODEL))],
        out_specs=pl.BlockSpec((tm, D_MODEL), row),
        compiler_params=_cparams("parallel", "parallel"),
        name="xattn",
    )(x2, k_mem, v_mem, g_pre, wq, wo, g_post)


def _mlp_kernel(x_ref, gpre_ref, wup_ref, wdn_ref, gpost_ref, out_ref, h_ref, acc_ref):
    j = pl.program_id(1)

    @pl.when(j == 0)
    def _():
        h_ref[...] = _rms(x_ref[...], gpre_ref[...]).astype(BF16)
        acc_ref[...] = jnp.zeros_like(acc_ref)

    h = h_ref[...]
    n_sub = wup_ref.shape[1] // MLP_SUB

    def up(c):
        return _dot(h, wup_ref[:, c * MLP_SUB:(c + 1) * MLP_SUB])

    a_next = up(0)
    acc = None
    for c in range(n_sub):
        a = jnp.maximum(a_next, 0.0)
        if c + 1 < n_sub:
            a_next = up(c + 1)
        t = _dot((a * a).astype(BF16), wdn_ref[c * MLP_SUB:(c + 1) * MLP_SUB, :])
        acc = t if acc is None else acc + t
    acc_ref[...] += acc

    @pl.when(j == pl.num_programs(1) - 1)
    def _():
        out_ref[...] = x_ref[...] + _rms(acc_ref[...], gpost_ref[...])


def _mlp(x2, g_pre, w_up, w_down, g_post):
    T = x2.shape[0]
    tm = MLP_ROWS
    tf = MLP_COLS
    row = lambda i, j: (i, 0)
    return pl.pallas_call(
        _mlp_kernel,
        out_shape=jax.ShapeDtypeStruct((T, D_MODEL), F32),
        grid=(T // tm, D_FF // tf),
        in_specs=[pl.BlockSpec((tm, D_MODEL), row), _const_spec((1, D_MODEL)),
                  pl.BlockSpec((D_MODEL, tf), lambda i, j: (0, j)),
                  pl.BlockSpec((tf, D_MODEL), lambda i, j: (j, 0)),
                  _const_spec((1, D_MODEL))],
        out_specs=pl.BlockSpec((tm, D_MODEL), row),
        scratch_shapes=[pltpu.VMEM((tm, D_MODEL), BF16), pltpu.VMEM((tm, D_MODEL), F32)],
        compiler_params=_cparams("parallel", "arbitrary"),
        name="mlp",
    )(x2, g_pre, w_up, w_down, g_post)


def _sw_head_order():
    per_kv = N_HEADS // SW_KV_HEADS
    return [h for p in range(per_kv) for h in (p, p + per_kv)]


def _head_cols(order):
    return jnp.concatenate([jnp.arange(h * HEAD_DIM, (h + 1) * HEAD_DIM) for h in order])


def _pad_rows(w, offset):
    out = jnp.zeros((LORA_PAD, w.shape[1]), w.dtype)
    return out.at[offset:offset + w.shape[0]].set(w)


def kernel(x, mem, positions, norm_mix_pre, norm_mix_post, norm_xattn_pre, norm_xattn_post, norm_mem, norm_mlp_pre, norm_mlp_post, w_in, rw_mu, rw_w0, rw_w_up, rw_a0, rw_a_up, rw_g_up, rw_k_k, rw_k_a, rw_r_k, rw_lnx_w, rw_lnx_b, rw_vres_down, rw_vres_mu, rw_v0, rw_vres_up, sw_sinks, w_branch, w_out, w_xq, w_xk, w_xv, w_xo, w_up, w_down):
    B, S, D = x.shape
    M = mem.shape[1]
    depth = w_in.shape[0]
    T = B * S
    x2 = x.reshape(T, D)
    mem2 = mem.reshape(B * M, D)
    cos_t, sin_t = _rope_tables(positions)

    sw_order = _sw_head_order()
    sw_cols = _head_cols(sw_order)
    head_of_col = jnp.arange(MXU_TILE) // HEAD_DIM
    bd = (head_of_col[:, None] == head_of_col[None, :]).astype(BF16)
    mb0 = RW_COLS
    sw0 = mb0 + 3 * BRANCH
    gt0 = sw0 + BRANCH + 2 * SW_KV_HEADS * HEAD_DIM
    vec = lambda t: t.reshape(1, -1)

    v_first = None
    for l in range(depth):
        wl = w_in[l]
        segs = [wl[:, :RW_COLS]]
        if l > 0:
            segs.append(jnp.pad(rw_vres_down[l - 1], ((0, 0), (0, LORA_PAD - rw_vres_down.shape[2]))))
        segs += [wl[:, mb0:sw0], wl[:, sw0:sw0 + BRANCH][:, sw_cols], wl[:, sw0 + BRANCH:gt0],
                 wl[:, gt0:]]
        w_cat = jnp.concatenate(segs, axis=1).astype(BF16)
        outs = _mix_in(x2, vec(norm_mix_pre[l]), w_cat, cos_t, sin_t, l > 0)
        if l > 0:
            u_rw, vm, mq, mk, mv, sq, sk, sv, gates = outs
        else:
            u_rw, mq, mk, mv, sq, sk, sv, gates = outs

        rw_p = dict(
            mu=vec(rw_mu[l]), w0=vec(rw_w0[l]), w_up=_pad_rows(rw_w_up[l], 0).astype(BF16),
            a0=vec(rw_a0[l]), a_up=_pad_rows(rw_a_up[l], rw_w_up.shape[1]).astype(BF16),
            g_up=rw_g_up[l].astype(BF16), k_k=vec(rw_k_k[l]), k_a=vec(rw_k_a[l]),
            r_k=vec(rw_r_k[l]), lnx_w=vec(rw_lnx_w[l]), lnx_b=vec(rw_lnx_b[l]), bd=bd)
        if l == 0:
            o_rw, v_first = _rwkv(u_rw, rw_p, B, S)
        else:
            vres = dict(vm=vm, v_first=v_first,
                        mu=jnp.pad(vec(rw_vres_mu[l - 1]), ((0, 0), (0, LORA_PAD - rw_vres_mu.shape[1]))),
                        v0=vec(rw_v0[l - 1]), up=_pad_rows(rw_vres_up[l - 1], 0).astype(BF16))
            o_rw = _rwkv(u_rw, rw_p, B, S, vres)

        o_mb = _moba(mq, mk, mv, B, S)
        sinks = jnp.broadcast_to(sw_sinks[l][jnp.array(sw_order)].reshape(-1, 2, 1),
                                 (BRANCH // LANES, 2, LANES)).astype(F32)
        o_sw = _swa(sq, sk, sv, sinks, B, S)

        wb = jnp.stack([w_branch[l, 0], w_branch[l, 1], w_branch[l, 2][sw_cols]]).astype(BF16)
        x2 = _mix_out(x2, o_rw, o_mb, o_sw, gates, wb, w_out[l].astype(BF16), vec(norm_mix_post[l]))

        k_mem, v_mem = _mem_kv(mem2, vec(norm_mem[l]), w_xk[l].astype(BF16), w_xv[l].astype(BF16), M)
        x2 = _xattn(x2, k_mem, v_mem, vec(norm_xattn_pre[l]), w_xq[l].astype(BF16),
                    w_xo[l].astype(BF16), vec(norm_xattn_post[l]), B, S, M)

        x2 = _mlp(x2, vec(norm_mlp_pre[l]), w_up[l].astype(BF16), w_down[l].astype(BF16),
                  vec(norm_mlp_post[l]))
    return x2.reshape(B, S, D)
```

```python
import functools

import jax
import jax.numpy as jnp
from jax import lax
from jax.experimental import pallas as pl
from jax.experimental.pallas import tpu as pltpu

F32 = jnp.float32
BF16 = jnp.bfloat16

D_MODEL = 1024
HEAD_DIM = 64
BRANCH = 512
N_HEADS = BRANCH // HEAD_DIM
RW_COLS = 3 * BRANCH + 64 + 64 + 128
LORA_PAD = 128
MB_BLOCK = 256
MB_TOPK = 3
SW_WINDOW = 128
SW_KV_HEADS = 2
XA_HEADS = 4
XA_HEAD_DIM = 128
XA_WIDTH = XA_HEADS * XA_HEAD_DIM
D_FF = 4 * D_MODEL
ROPE_THETA = 10000.0
NORM_EPS = 1e-6
RW_LNX_EPS = 1e-5 * HEAD_DIM
MASK_VALUE = -1e30
LOG2E = 1.4426950408889634

LANES = 128
MXU_TILE = 256
RW_CHUNK = 64
MIX_OUT_SUB = 256
MLP_ROWS = 1024
MLP_COLS = 2048
MLP_SUB = 512
RW_BATCH = 4
VMEM_LIMIT = 56 * 1024 * 1024


def _cparams(*sem):
    return pltpu.CompilerParams(dimension_semantics=sem, vmem_limit_bytes=VMEM_LIMIT)


def _dot(a, b, precision=None):
    return jnp.dot(a, b, preferred_element_type=F32, precision=precision)


def _dot_nt(a, b, precision=None):
    return lax.dot_general(a, b, (((1,), (1,)), ((), ())),
                           preferred_element_type=F32, precision=precision)


def _dot_tn(a, b, precision=None):
    return lax.dot_general(a, b, (((0,), (0,)), ((), ())),
                           preferred_element_type=F32, precision=precision)


def _rms(xf, g):
    ms = jnp.mean(xf * xf, axis=-1, keepdims=True)
    return xf * lax.rsqrt(ms + NORM_EPS) * g


def _const_spec(shape):
    nd = len(shape)
    return pl.BlockSpec(shape, lambda *_: (0,) * nd, pipeline_mode=pl.Buffered(1))


def _layer_spec(shape, layer):
    nd = len(shape)
    return pl.BlockSpec((None,) + tuple(shape), lambda *_: (layer,) + (0,) * nd,
                        pipeline_mode=pl.Buffered(1))


def _rope_kernel(pos_ref, invf_ref, sign_ref, cos_ref, sin_ref):
    ang = pos_ref[...].astype(F32) * invf_ref[...]
    cos_ref[...] = jnp.cos(ang)
    sin_ref[...] = jnp.sin(ang) * sign_ref[...]


def _rope_tables(positions):
    T = positions.size
    tm = 2048
    inv_freq = 1.0 / (ROPE_THETA ** (jnp.arange(0, HEAD_DIM, 2, dtype=F32) / HEAD_DIM))
    invf = jnp.tile(inv_freq, LANES // (HEAD_DIM // 2))[None, :]
    half = jnp.concatenate([-jnp.ones((HEAD_DIM // 2,), F32), jnp.ones((HEAD_DIM // 2,), F32)])
    sign = jnp.tile(half, LANES // HEAD_DIM)[None, :]
    return pl.pallas_call(
        _rope_kernel,
        out_shape=(jax.ShapeDtypeStruct((T, LANES), F32),) * 2,
        grid=(T // tm,),
        in_specs=[pl.BlockSpec((tm, 1), lambda i: (i, 0)),
                  _const_spec((1, LANES)), _const_spec((1, LANES))],
        out_specs=(pl.BlockSpec((tm, LANES), lambda i: (i, 0)),) * 2,
        compiler_params=_cparams("parallel"),
        name="rope_tables",
    )(positions.reshape(T, 1), invf, sign)


def _rope_tile(t, cos, sin, lane_lt_half):
    fwd = pltpu.roll(t, LANES - HEAD_DIM // 2, axis=1)
    bwd = pltpu.roll(t, HEAD_DIM // 2, axis=1)
    partner = jnp.where(lane_lt_half, fwd, bwd)
    return t * cos + partner * sin


def _mix_in_kernel(has_vm, x_ref, g_ref, w_ref, *rest):
    if has_vm:
        (wvm_ref, cos_ref, sin_ref,
         urw_ref, vm_ref, mq_ref, mk_ref, mv_ref, sq_ref, sk_ref, sv_ref, gate_ref) = rest
    else:
        cos_ref, sin_ref, urw_ref, mq_ref, mk_ref, mv_ref, sq_ref, sk_ref, sv_ref, gate_ref = rest
    h = _rms(x_ref[...], g_ref[...]).astype(BF16)
    cos = cos_ref[...]
    sin = sin_ref[...]
    lane = lax.broadcasted_iota(jnp.int32, cos.shape, 1)
    lt_half = (lane % HEAD_DIM) < (HEAD_DIM // 2)
    first_head = lane < HEAD_DIM
    q_scale = LOG2E * HEAD_DIM ** -0.5

    col = [0]

    def seg(width):
        a = col[0]
        col[0] += width
        return _dot(h, w_ref[:, a:a + width])

    def rope_chunks(width, scale):
        acc = seg(width)
        return [_rope_tile(acc[:, c * LANES:(c + 1) * LANES], cos, sin, lt_half) * scale
                for c in range(width // LANES)]

    def store_chunks(ref, chunks):
        for c, t in enumerate(chunks):
            ref[:, c * LANES:(c + 1) * LANES] = t.astype(ref.dtype)

    urw_ref[...] = seg(RW_COLS).astype(urw_ref.dtype)
    if has_vm:
        vm_ref[...] = _dot(h, wvm_ref[...])
    store_chunks(mq_ref, rope_chunks(BRANCH, q_scale))
    store_chunks(mk_ref, rope_chunks(BRANCH, 1.0))
    mv_ref[...] = seg(BRANCH).astype(mv_ref.dtype)
    nat = rope_chunks(BRANCH, q_scale)
    half_tiles = len(nat) // 2
    paired = []
    for p in range(len(nat)):
        a, b = nat[p // 2], nat[p // 2 + half_tiles]
        if p % 2 == 0:
            paired.append(jnp.where(first_head, a, pltpu.roll(b, HEAD_DIM, axis=1)))
        else:
            paired.append(jnp.where(first_head, pltpu.roll(a, HEAD_DIM, axis=1), b))
    store_chunks(sq_ref, paired)
    store_chunks(sk_ref, rope_chunks(SW_KV_HEADS * HEAD_DIM, 1.0))
    sv_ref[...] = seg(SW_KV_HEADS * HEAD_DIM).astype(sv_ref.dtype)
    for n in range(3):
        gate_ref[:, n * D_MODEL:(n + 1) * D_MODEL] = jax.nn.sigmoid(seg(D_MODEL)).astype(gate_ref.dtype)


def _mix_in(x2, g, w_in, w_vm, cos_t, sin_t, layer):
    T = x2.shape[0]
    tm = 512
    has_vm = w_vm is not None
    kv_w = SW_KV_HEADS * HEAD_DIM
    widths = [(RW_COLS, BF16)]
    if has_vm:
        widths.append((LORA_PAD, F32))
    widths += [(BRANCH, BF16), (BRANCH, BF16), (BRANCH, BF16), (BRANCH, BF16),
               (kv_w, BF16), (kv_w, BF16), (3 * D_MODEL, BF16)]
    assert sum(w for w, _ in widths) - (LORA_PAD if has_vm else 0) == w_in.shape[2]
    row = lambda i: (i, 0)
    in_specs = [pl.BlockSpec((tm, D_MODEL), row), _layer_spec((1, D_MODEL), layer),
                _layer_spec(w_in.shape[1:], layer)]
    args = [x2, g, w_in]
    if has_vm:
        in_specs.append(_layer_spec(w_vm.shape[1:], layer - 1))
        args.append(w_vm)
    in_specs += [pl.BlockSpec((tm, LANES), row), pl.BlockSpec((tm, LANES), row)]
    args += [cos_t, sin_t]
    return pl.pallas_call(
        functools.partial(_mix_in_kernel, has_vm),
        out_shape=tuple(jax.ShapeDtypeStruct((T, w), dt) for w, dt in widths),
        grid=(T // tm,),
        in_specs=in_specs,
        out_specs=tuple(pl.BlockSpec((tm, w), row) for w, _ in widths),
        compiler_params=_cparams("parallel"),
        name="mix_in",
    )(*args)


def _shift_prev(cur, carry_ref, bi):
    rolled = pltpu.roll(cur, 1, axis=0)
    row = lax.broadcasted_iota(jnp.int32, cur.shape, 0)
    prev = jnp.where(row == 0, carry_ref[bi, 0:1, :], rolled)
    carry_ref[bi, 0:1, :] = cur[cur.shape[0] - 1:, :]
    return prev


def _split_bf16(x, terms):
    pieces = []
    for _ in range(terms):
        hi = x.astype(BF16)
        pieces.append(hi)
        x = x - hi.astype(F32)
    return pieces


def _head_sums(x, bd, terms):
    w = bd.shape[0]
    halves = []
    for c in range(x.shape[1] // w):
        pieces = _split_bf16(x[:, c * w:(c + 1) * w], terms)
        halves.append(sum(_dot(piece, bd) for piece in pieces))
    return jnp.concatenate(halves, axis=1)


def _dot_split_rhs(a, b, terms):
    return sum(_dot(a, piece) for piece in _split_bf16(b, terms))


def _rwkv_kernel(has_vres, nb, *refs):
    if has_vres:
        (u_ref, vm_ref, vf_ref, mu_ref, w0_ref, wup_ref, a0_ref, aup_ref, gup_ref, kk_ref,
         ka_ref, rk_ref, lw_ref, lb_ref, bd_ref, vmu_ref, v0_ref, vup_ref,
         o_ref, s_ref, cu_ref, cvm_ref, hout_ref) = refs
        vout_ref = None
    else:
        (u_ref, mu_ref, w0_ref, wup_ref, a0_ref, aup_ref, gup_ref, kk_ref,
         ka_ref, rk_ref, lw_ref, lb_ref, bd_ref,
         o_ref, vout_ref, s_ref, cu_ref, hout_ref) = refs
    C = RW_CHUNK
    P2 = 2 * C
    npair = BRANCH // LANES

    @pl.when(pl.program_id(1) == 0)
    def _():
        s_ref[...] = jnp.zeros_like(s_ref)
        cu_ref[...] = jnp.zeros_like(cu_ref)
        if has_vres:
            cvm_ref[...] = jnp.zeros_like(cvm_ref)

    bd = bd_ref[...]
    ti = lax.broadcasted_iota(jnp.int32, (C, C), 0)
    si = lax.broadcasted_iota(jnp.int32, (C, C), 1)
    tri = (si <= ti).astype(BF16)
    lane = lax.broadcasted_iota(jnp.int32, (1, LANES), 1)
    first_head = lane < HEAD_DIM
    pr = lax.broadcasted_iota(jnp.int32, (P2, P2), 0)
    pc = lax.broadcasted_iota(jnp.int32, (P2, P2), 1)
    same_head = (pr // C) == (pc // C)
    strict = same_head & ((pc % C) < (pr % C))
    incl = same_head & ((pc % C) <= (pr % C))
    eye = (pr == pc).astype(F32)

    def stack(t):
        return jnp.concatenate([jnp.where(first_head, t, 0.0), jnp.where(first_head, 0.0, t)], axis=0)

    pre = []
    inst = []
    for bi in range(nb):
        u = u_ref[bi].astype(F32)
        ul = u + mu_ref[...] * (_shift_prev(u, cu_ref, bi) - u)
        r = ul[:, 0:BRANCH]
        k = ul[:, BRANCH:2 * BRANCH]
        v = ul[:, 2 * BRANCH:3 * BRANCH]
        x_wa = ul[:, 3 * BRANCH:3 * BRANCH + LANES]
        x_g = ul[:, 3 * BRANCH + LANES:]

        z = w0_ref[...] + _dot(jnp.tanh(x_wa).astype(BF16), wup_ref[...])
        a = jax.nn.sigmoid(a0_ref[...] + _dot(x_wa.astype(BF16), aup_ref[...]))
        g = _dot(jax.nn.sigmoid(x_g).astype(BF16), gup_ref[...])
        softplus = jnp.maximum(-z, 0.0) + jnp.log1p(jnp.exp(-jnp.abs(z)))
        ell = -jnp.exp(-softplus - 0.5)

        if has_vres:
            vm = vm_ref[bi]
            vml = vm + vmu_ref[...] * (_shift_prev(vm, cvm_ref, bi) - vm)
            vgate = jax.nn.sigmoid(v0_ref[...] + _dot(vml.astype(BF16), vup_ref[...]))
            v = v + (vf_ref[bi] - v) * vgate
        else:
            vout_ref[bi] = v

        kk = k * kk_ref[...]
        kk = kk / jnp.maximum(jnp.sqrt(_head_sums(kk * kk, bd, 1)), 1e-12)
        k_eff = k * (1.0 + (a - 1.0) * ka_ref[...])
        b = kk * a
        cum = _dot_split_rhs(tri, ell, 3)
        cum_last = cum[C - 1:, :]
        g_inv = jnp.exp(-cum)
        g_tail = jnp.exp(cum_last - cum)
        kap_t = kk * jnp.exp(cum - ell)
        r_t = r * jnp.exp(cum)
        b_t = b * g_inv
        k_t = k_eff * g_inv
        b_g = b * g_tail
        k_g = k_eff * g_tail
        g_all = jnp.exp(cum_last)
        pre.append((r, k_eff, v, g))
        for p in range(npair):
            sl = slice(p * LANES, (p + 1) * LANES)
            inst.append(dict(bi=bi, p=p, sl=sl, kap=stack(kap_t[:, sl]), r=stack(r_t[:, sl]),
                             b=stack(b_t[:, sl]), k=stack(k_t[:, sl]), v=stack(v[:, sl]),
                             bg=stack(b_g[:, sl]), kg=stack(k_g[:, sl]), gall=g_all[:, sl]))

    def mm(a, b):
        return _dot(a.astype(BF16), b.astype(BF16))

    def mm_nt(a, b):
        return _dot_nt(a.astype(BF16), b.astype(BF16))

    def mm_tn(a, b):
        return _dot_tn(a.astype(BF16), b.astype(BF16))

    for d in inst:
        aa = mm_nt(jnp.concatenate([d["kap"], d["r"]], axis=0),
                   jnp.concatenate([d["b"], d["k"]], axis=0))
        d["n"] = jnp.where(strict, -aa[:P2, :P2], 0.0)
        d["a_ak"] = jnp.where(strict, aa[:P2, P2:], 0.0)
        d["a_r"] = jnp.where(jnp.concatenate([incl, incl], axis=1), aa[P2:, :], 0.0)
    for d in inst:
        d["akv"] = mm(d["a_ak"], d["v"])
    for d in inst:
        d["t"] = eye + d["n"]
        d["pw"] = mm(d["n"], d["n"])
    for _ in range(4):
        for d in inst:
            pt = mm(d["pw"], jnp.concatenate([d["pw"], d["t"]], axis=1))
            d["pw"] = pt[:, :P2]
            d["t"] = d["t"] + pt[:, P2:]
    for d in inst:
        d["t"] = d["t"] + mm(d["pw"], d["t"])
    for d in inst:
        ty = mm(d["t"], jnp.concatenate([d["kap"], d["akv"]], axis=1))
        d["w_til"], d["u_til"] = ty[:, :LANES], ty[:, LANES:]
    for d in inst:
        d["state"] = s_ref[d["bi"] * npair + d["p"]]
        xr = mm_nt(jnp.concatenate([d["w_til"], d["r"]], axis=0), d["state"])
        sa = -(xr[:P2] + d["u_til"])
        d["sv"] = jnp.concatenate([sa, d["v"]], axis=0)
        d["rs"] = xr[P2:]
    for d in inst:
        s_ref[d["bi"] * npair + d["p"]] = (
            d["state"] * d["gall"] + mm_tn(d["sv"], jnp.concatenate([d["bg"], d["kg"]], axis=0)))
    for d in inst:
        o_st = d["rs"] + mm(d["a_r"], d["sv"])
        hout_ref[d["bi"], :, d["sl"]] = o_st[:C] + o_st[C:]

    inv_n = 1.0 / HEAD_DIM
    for bi in range(nb):
        r, k_eff, v, g = pre[bi]
        out = hout_ref[bi]
        mean = _head_sums(out, bd, 2) * inv_n
        cen = out - mean
        var = _head_sums(cen * cen, bd, 1) * inv_n
        normed = cen * lax.rsqrt(var + RW_LNX_EPS) * lw_ref[...] + lb_ref[...]
        bonus = _head_sums(r * k_eff * rk_ref[...], bd, 2) * v
        o_ref[bi] = ((normed + bonus) * g).astype(o_ref.dtype)


def _rwkv(u_rw, p, layer, B, S, vres=None):
    T = u_rw.shape[0]
    C = RW_CHUNK
    nb = RW_BATCH
    nc = S // C
    has_vres = vres is not None
    blk = lambda w: pl.BlockSpec((nb, C, w), lambda b, c: (b, c, 0))
    names = ("mu", "w0", "w_up", "a0", "a_up", "g_up", "k_k", "k_a", "r_k", "lnx_w", "lnx_b")
    in_specs = [blk(RW_COLS)]
    args = [u_rw.reshape(B, S, RW_COLS)]
    if has_vres:
        in_specs += [blk(LORA_PAD), blk(BRANCH)]
        args += [vres["vm"].reshape(B, S, LORA_PAD), vres["v_first"].reshape(B, S, BRANCH)]
    in_specs += [_layer_spec(p[n].shape[1:], layer) for n in names]
    in_specs.append(_const_spec((MXU_TILE, MXU_TILE)))
    args += [p[n] for n in names] + [p["bd"]]
    scratch = [pltpu.VMEM((nb * (BRANCH // LANES), 2 * C, LANES), F32),
               pltpu.VMEM((nb, 8, RW_COLS), F32)]
    if has_vres:
        vnames = ("mu", "v0", "up")
        in_specs += [_layer_spec(vres[n].shape[1:], layer - 1) for n in vnames]
        args += [vres[n] for n in vnames]
        out_shape = jax.ShapeDtypeStruct((B, S, BRANCH), BF16)
        out_specs = blk(BRANCH)
        scratch.append(pltpu.VMEM((nb, 8, LORA_PAD), F32))
    else:
        out_shape = (jax.ShapeDtypeStruct((B, S, BRANCH), BF16),
                     jax.ShapeDtypeStruct((B, S, BRANCH), F32))
        out_specs = (blk(BRANCH), blk(BRANCH))
    scratch.append(pltpu.VMEM((nb, C, BRANCH), F32))
    outs = pl.pallas_call(
        functools.partial(_rwkv_kernel, has_vres, nb),
        out_shape=out_shape,
        grid=(B // nb, nc),
        in_specs=in_specs,
        out_specs=out_specs,
        scratch_shapes=scratch,
        compiler_params=_cparams("parallel", "arbitrary"),
        name="rwkv",
    )(*args)
    if has_vres:
        return outs.reshape(T, BRANCH)
    return outs[0].reshape(T, BRANCH), outs[1].reshape(T, BRANCH)


def _moba_kernel(q_ref, k_ref, v_ref, o_ref, kmean_ref, vt_ref):
    S = q_ref.shape[0]
    nb = S // MB_BLOCK
    tq = MB_BLOCK
    n_heads = LANES // HEAD_DIM
    for n in range(nb):
        blk = k_ref[n * MB_BLOCK:(n + 1) * MB_BLOCK, :].astype(F32)
        kmean_ref[n:n + 1, :] = jnp.mean(blk, axis=0, keepdims=True)
    kmean = kmean_ref[...]
    vt_ref[...] = v_ref[...].astype(F32).T.astype(BF16)
    lane = lax.broadcasted_iota(jnp.int32, (1, LANES), 1)
    head_lanes = [(lane // HEAD_DIM) == hh for hh in range(n_heads)]

    blk_id = lax.broadcasted_iota(jnp.int32, (nb, S), 0)
    own = lax.broadcasted_iota(jnp.int32, (nb, S), 1) // MB_BLOCK
    valid = blk_id < own
    q_all = q_ref[...]
    kmean_rep = jnp.concatenate(
        [kmean] + [jnp.broadcast_to(kmean[m:m + 1, :], (nb, LANES)) for m in range(nb - 1)], axis=0)
    sels = []
    for hh in range(n_heads):
        pieces = _split_bf16(jnp.where(head_lanes[hh], kmean_rep, 0.0), 3)
        gates = sum(_dot_nt(piece, q_all) for piece in pieces)
        gate = gates[:nb]
        cnt = jnp.zeros((nb, S), jnp.int32)
        for m in range(nb - 1):
            gm = gates[nb * (m + 1):nb * (m + 2)]
            beats = (own > m) & ((gm > gate) | ((gm == gate) & (m < blk_id)))
            cnt = cnt + beats.astype(jnp.int32)
        sels.append((valid & (cnt < MB_TOPK)).astype(F32))

    ki = lax.broadcasted_iota(jnp.int32, (tq, tq), 0)
    qi = lax.broadcasted_iota(jnp.int32, (tq, tq), 1)
    causal = ki <= qi

    def scores(i):
        q_t = q_ref[i * tq:(i + 1) * tq, :]
        k_all = k_ref[0:(i + 1) * MB_BLOCK, :]
        return [_dot_nt(k_all, jnp.where(head_lanes[hh], q_t, jnp.zeros_like(q_t)))
                for hh in range(n_heads)]

    s_cur = scores(0)
    for i in range(nb):
        s_next = scores(i + 1) if i + 1 < nb else None
        probs = []
        for hh in range(n_heads):
            s = s_cur[hh]
            parts = []
            for j in range(i):
                keep = sels[hh][j:j + 1, i * tq:(i + 1) * tq] > 0.5
                parts.append(jnp.where(keep, s[j * MB_BLOCK:(j + 1) * MB_BLOCK, :], MASK_VALUE))
            parts.append(jnp.where(causal, s[i * MB_BLOCK:, :], MASK_VALUE))
            s = parts[0] if len(parts) == 1 else jnp.concatenate(parts, axis=0)
            p = jnp.exp2(s - jnp.max(s, axis=0, keepdims=True))
            probs.append((p.astype(BF16), jnp.sum(p, axis=0, keepdims=True)))
        vt = vt_ref[:, 0:(i + 1) * MB_BLOCK]
        ot = [_dot(vt, p) / denom for p, denom in probs]
        o_t = jnp.concatenate([ot[0][:HEAD_DIM], ot[1][HEAD_DIM:]], axis=0).T
        o_ref[i * tq:(i + 1) * tq, :] = o_t.astype(o_ref.dtype)
        s_cur = s_next


def _moba(q, k, v, B, S):
    T = q.shape[0]
    npair = BRANCH // LANES
    spec = pl.BlockSpec((S, LANES), lambda b, p: (b, p))
    return pl.pallas_call(
        _moba_kernel,
        out_shape=jax.ShapeDtypeStruct((T, BRANCH), BF16),
        grid=(B, npair),
        in_specs=[spec, spec, spec],
        out_specs=spec,
        scratch_shapes=[pltpu.VMEM((S // MB_BLOCK, LANES), F32), pltpu.VMEM((LANES, S), BF16)],
        compiler_params=_cparams("parallel", "parallel"),
        name="moba",
    )(q, k, v)


def _swa_kernel(q_ref, k_ref, v_ref, sink_ref, o_ref, vt_ref):
    S = q_ref.shape[0]
    W = SW_WINDOW
    n_heads = LANES // HEAD_DIM
    vt_ref[...] = v_ref[...].astype(F32).T.astype(BF16)
    lane = lax.broadcasted_iota(jnp.int32, (1, LANES), 1)
    dist = (lax.broadcasted_iota(jnp.int32, (2 * W, W), 1) + W
            - lax.broadcasted_iota(jnp.int32, (2 * W, W), 0))
    band = (dist >= 0) & (dist < W)
    causal = (lax.broadcasted_iota(jnp.int32, (W, W), 0)
              <= lax.broadcasted_iota(jnp.int32, (W, W), 1))
    head_lanes = [(lane // HEAD_DIM) == hh for hh in range(n_heads)]
    sinks = [sink_ref[0, hh:hh + 1, 0:1] * LOG2E for hh in range(n_heads)]
    group = 4

    for g0 in range(0, S // W, group):
        work = []
        for n in range(g0, g0 + group):
            k0 = max(n - 1, 0) * W
            k1 = (n + 1) * W
            mask = causal if n == 0 else band
            q_t = q_ref[n * W:(n + 1) * W, :]
            k_w = k_ref[k0:k1, :]
            for hh in range(n_heads):
                q_h = jnp.where(head_lanes[hh], q_t, jnp.zeros_like(q_t))
                work.append(dict(n=n, hh=hh, k0=k0, k1=k1, mask=mask, s=_dot_nt(k_w, q_h)))
        for d in work:
            s = jnp.where(d["mask"], d["s"], MASK_VALUE)
            sink = sinks[d["hh"]]
            m_col = jnp.maximum(jnp.max(s, axis=0, keepdims=True), sink)
            p = jnp.exp2(s - m_col)
            d["denom"] = jnp.sum(p, axis=0, keepdims=True) + jnp.exp2(sink - m_col)
            d["p"] = p.astype(BF16)
        for d in work:
            d["ot"] = _dot(vt_ref[:, d["k0"]:d["k1"]], d["p"]) / d["denom"]
        o_t = jnp.concatenate(
            [jnp.concatenate([a["ot"][:HEAD_DIM], b["ot"][HEAD_DIM:]], axis=0)
             for a, b in zip(work[0::2], work[1::2])], axis=1)
        o_ref[g0 * W:(g0 + group) * W, :] = o_t.T.astype(o_ref.dtype)


def _swa(q, k, v, sinks, B, S, layer):
    T = q.shape[0]
    npair = BRANCH // LANES
    spec = pl.BlockSpec((S, LANES), lambda b, p: (b, p))
    kv_spec = pl.BlockSpec((S, LANES), lambda b, p: (b, 0))
    return pl.pallas_call(
        _swa_kernel,
        out_shape=jax.ShapeDtypeStruct((T, BRANCH), BF16),
        grid=(B, npair),
        in_specs=[spec, kv_spec, kv_spec,
                  pl.BlockSpec((None, 1, 2, LANES), lambda b, p: (layer, p, 0, 0))],
        out_specs=spec,
        scratch_shapes=[pltpu.VMEM((LANES, S), BF16)],
        compiler_params=_cparams("parallel", "parallel"),
        name="swa",
    )(q, k, v, sinks)


def _mix_out_kernel(x_ref, orw_ref, omb_ref, osw_ref, gate_ref, wb_ref, wo_ref, g_ref, out_ref):
    tm = x_ref.shape[0]
    subs = [slice(r, r + MIX_OUT_SUB) for r in range(0, tm, MIX_OUT_SUB)]

    def gated(rows):
        y = None
        for n, o_ref in enumerate((orw_ref, omb_ref, osw_ref)):
            t = gate_ref[rows, n * D_MODEL:(n + 1) * D_MODEL] * _dot(o_ref[rows, :], wb_ref[n])
            y = t if y is None else y + t
        return y.astype(BF16)

    y_next = gated(subs[0])
    for i, rows in enumerate(subs):
        y = y_next
        if i + 1 < len(subs):
            y_next = gated(subs[i + 1])
        out_ref[rows, :] = x_ref[rows, :] + _rms(_dot(y, wo_ref[...]), g_ref[...])


def _mix_out(x2, o_rw, o_mb, o_sw, gates, w_branch, w_out, g, layer):
    T = x2.shape[0]
    tm = 512
    row = lambda i: (i, 0)
    br = pl.BlockSpec((tm, BRANCH), row)
    return pl.pallas_call(
        _mix_out_kernel,
        out_shape=jax.ShapeDtypeStruct((T, D_MODEL), F32),
        grid=(T // tm,),
        in_specs=[pl.BlockSpec((tm, D_MODEL), row), br, br, br,
                  pl.BlockSpec((tm, 3 * D_MODEL), row),
                  _layer_spec(w_branch.shape[1:], layer), _layer_spec(w_out.shape[1:], layer),
                  _layer_spec((1, D_MODEL), layer)],
        out_specs=pl.BlockSpec((tm, D_MODEL), row),
        compiler_params=_cparams("parallel"),
        name="mix_out",
    )(x2, o_rw, o_mb, o_sw, gates, w_branch, w_out, g)


def _mem_kv_kernel(m_ref, g_ref, wk_ref, wv_ref, k_ref, vt_ref):
    m = _rms(m_ref[...], g_ref[...]).astype(BF16)
    k_ref[...] = _dot(m, wk_ref[...]).astype(k_ref.dtype)
    vt_ref[...] = _dot(m, wv_ref[...]).T.astype(vt_ref.dtype)


def _mem_kv(mem2, g, wk, wv, M, layer):
    R = mem2.shape[0]
    row = lambda i: (i, 0)
    return pl.pallas_call(
        _mem_kv_kernel,
        out_shape=(jax.ShapeDtypeStruct((R, XA_WIDTH), BF16),
                   jax.ShapeDtypeStruct((R // M, XA_WIDTH, M), BF16)),
        grid=(R // M,),
        in_specs=[pl.BlockSpec((M, D_MODEL), row), _layer_spec((1, D_MODEL), layer),
                  _layer_spec(wk.shape[1:], layer), _layer_spec(wv.shape[1:], layer)],
        out_specs=(pl.BlockSpec((M, XA_WIDTH), row),
                   pl.BlockSpec((None, XA_WIDTH, M), lambda i: (i, 0, 0))),
        compiler_params=_cparams("parallel"),
        name="mem_kv",
    )(mem2, g, wk, wv)


def _xattn_kernel(x_ref, k_ref, vt_ref, gpre_ref, wq_ref, wo_ref, gpost_ref, out_ref):
    x = x_ref[...]
    h = _rms(x, gpre_ref[...]).astype(BF16)
    q = (_dot(h, wq_ref[...]) * (LOG2E * XA_HEAD_DIM ** -0.5)).astype(BF16)
    scores = [_dot_nt(k_ref[:, hd * XA_HEAD_DIM:(hd + 1) * XA_HEAD_DIM],
                      q[:, hd * XA_HEAD_DIM:(hd + 1) * XA_HEAD_DIM]) for hd in range(XA_HEADS)]
    probs = []
    for s in scores:
        p = jnp.exp2(s - jnp.max(s, axis=0, keepdims=True))
        probs.append((p.astype(BF16), jnp.sum(p, axis=0, keepdims=True)))
    o_t = jnp.concatenate(
        [_dot(vt_ref[hd * XA_HEAD_DIM:(hd + 1) * XA_HEAD_DIM, :], p) / denom
         for hd, (p, denom) in enumerate(probs)], axis=0)
    out_ref[...] = x + _rms(_dot(o_t.T.astype(BF16), wo_ref[...]), gpost_ref[...])


def _xattn(x2, k_mem, vt_mem, g_pre, wq, wo, g_post, B, S, M, layer):
    T = x2.shape[0]
    tm = 512
    nt = S // tm
    row = lambda b, i: (b * nt + i, 0)
    return pl.pallas_call(
        _xattn_kernel,
        out_shape=jax.ShapeDtypeStruct((T, D_MODEL), F32),
        grid=(B, nt),
        in_specs=[pl.BlockSpec((tm, D_MODEL), row),
                  pl.BlockSpec((M, XA_WIDTH), lambda b, i: (b, 0)),
                  pl.BlockSpec((None, XA_WIDTH, M), lambda b, i: (b, 0, 0)),
                  _layer_spec((1, D_MODEL), layer), _layer_spec(wq.shape[1:], layer),
                  _layer_spec(wo.shape[1:], layer), _layer_spec((1, D_MODEL), layer)],
        out_specs=pl.BlockSpec((tm, D_MODEL), row),
        compiler_params=_cparams("parallel", "parallel"),
        name="xattn",
    )(x2, k_mem, vt_mem, g_pre, wq, wo, g_post)


def _mlp_kernel(x_ref, gpre_ref, wup_ref, wdn_ref, gpost_ref, out_ref, h_ref, acc_ref):
    j = pl.program_id(1)

    @pl.when(j == 0)
    def _():
        h_ref[...] = _rms(x_ref[...], gpre_ref[...]).astype(BF16)
        acc_ref[...] = jnp.zeros_like(acc_ref)

    h = h_ref[...]
    n_sub = wup_ref.shape[1] // MLP_SUB

    def up(c):
        return _dot(h, wup_ref[:, c * MLP_SUB:(c + 1) * MLP_SUB])

    a_next = up(0)
    acc = None
    for c in range(n_sub):
        a = jnp.maximum(a_next, 0.0)
        if c + 1 < n_sub:
            a_next = up(c + 1)
        t = _dot((a * a).astype(BF16), wdn_ref[c * MLP_SUB:(c + 1) * MLP_SUB, :])
        acc = t if acc is None else acc + t
    acc_ref[...] += acc

    @pl.when(j == pl.num_programs(1) - 1)
    def _():
        out_ref[...] = x_ref[...] + _rms(acc_ref[...], gpost_ref[...])


def _mlp(x2, g_pre, w_up, w_down, g_post, layer):
    T = x2.shape[0]
    tm = MLP_ROWS
    tf = MLP_COLS
    row = lambda i, j: (i, 0)
    return pl.pallas_call(
        _mlp_kernel,
        out_shape=jax.ShapeDtypeStruct((T, D_MODEL), F32),
        grid=(T // tm, D_FF // tf),
        in_specs=[pl.BlockSpec((tm, D_MODEL), row), _layer_spec((1, D_MODEL), layer),
                  pl.BlockSpec((None, D_MODEL, tf), lambda i, j: (layer, 0, j)),
                  pl.BlockSpec((None, tf, D_MODEL), lambda i, j: (layer, j, 0)),
                  _layer_spec((1, D_MODEL), layer)],
        out_specs=pl.BlockSpec((tm, D_MODEL), row),
        scratch_shapes=[pltpu.VMEM((tm, D_MODEL), BF16), pltpu.VMEM((tm, D_MODEL), F32)],
        compiler_params=_cparams("parallel", "arbitrary"),
        name="mlp",
    )(x2, g_pre, w_up, w_down, g_post)


def kernel(x, mem, positions, norm_mix_pre, norm_mix_post, norm_xattn_pre, norm_xattn_post, norm_mem, norm_mlp_pre, norm_mlp_post, w_in, rw_mu, rw_w0, rw_w_up, rw_a0, rw_a_up, rw_g_up, rw_k_k, rw_k_a, rw_r_k, rw_lnx_w, rw_lnx_b, rw_vres_down, rw_vres_mu, rw_v0, rw_vres_up, sw_sinks, w_branch, w_out, w_xq, w_xk, w_xv, w_xo, w_up, w_down):
    B, S, D = x.shape
    M = mem.shape[1]
    depth = w_in.shape[0]
    T = B * S
    x2 = x.reshape(T, D)
    mem2 = mem.reshape(B * M, D)
    cos_t, sin_t = _rope_tables(positions)

    rows = lambda t: t.reshape(t.shape[0], 1, -1)
    bf = lambda t: t.astype(BF16)
    lora_in = rw_w_up.shape[1]
    per_kv = N_HEADS // SW_KV_HEADS
    head_of_col = jnp.arange(MXU_TILE) // HEAD_DIM
    rw_p = dict(
        mu=rows(rw_mu), w0=rows(rw_w0), a0=rows(rw_a0), k_k=rows(rw_k_k), k_a=rows(rw_k_a),
        r_k=rows(rw_r_k), lnx_w=rows(rw_lnx_w), lnx_b=rows(rw_lnx_b),
        w_up=bf(jnp.pad(rw_w_up, ((0, 0), (0, LORA_PAD - lora_in), (0, 0)))),
        a_up=bf(jnp.pad(rw_a_up, ((0, 0), (lora_in, LORA_PAD - lora_in - rw_a_up.shape[1]), (0, 0)))),
        g_up=bf(rw_g_up),
        bd=(head_of_col[:, None] == head_of_col[None, :]).astype(BF16))
    mv_pad = LORA_PAD - rw_vres_down.shape[2]
    w_vm = bf(jnp.pad(rw_vres_down, ((0, 0), (0, 0), (0, mv_pad))))
    vres_p = dict(mu=rows(jnp.pad(rw_vres_mu, ((0, 0), (0, mv_pad)))), v0=rows(rw_v0),
                  up=bf(jnp.pad(rw_vres_up, ((0, 0), (0, mv_pad), (0, 0)))))
    sinks = jnp.broadcast_to(
        sw_sinks.reshape(depth, SW_KV_HEADS, per_kv).transpose(0, 2, 1)[..., None].astype(F32),
        (depth, per_kv, SW_KV_HEADS, LANES))
    wb_sw = w_branch[:, 2].reshape(depth, SW_KV_HEADS, per_kv, HEAD_DIM, D)
    wb_sw = wb_sw.transpose(0, 2, 1, 3, 4).reshape(depth, 1, BRANCH, D)
    wb = bf(jnp.concatenate([w_branch[:, :2], wb_sw], axis=1))
    w_in_b, w_out_b, w_up_b, w_down_b = bf(w_in), bf(w_out), bf(w_up), bf(w_down)
    w_xq_b, w_xk_b, w_xv_b, w_xo_b = bf(w_xq), bf(w_xk), bf(w_xv), bf(w_xo)
    g_mix_pre, g_mix_post = rows(norm_mix_pre), rows(norm_mix_post)
    g_xa_pre, g_xa_post, g_mem = rows(norm_xattn_pre), rows(norm_xattn_post), rows(norm_mem)
    g_mlp_pre, g_mlp_post = rows(norm_mlp_pre), rows(norm_mlp_post)

    v_first = None
    for l in range(depth):
        outs = _mix_in(x2, g_mix_pre, w_in_b, w_vm if l > 0 else None, cos_t, sin_t, l)
        if l > 0:
            u_rw, vm, mq, mk, mv, sq, sk, sv, gates = outs
            o_rw = _rwkv(u_rw, rw_p, l, B, S, dict(vres_p, vm=vm, v_first=v_first))
        else:
            u_rw, mq, mk, mv, sq, sk, sv, gates = outs
            o_rw, v_first = _rwkv(u_rw, rw_p, l, B, S)
        o_mb = _moba(mq, mk, mv, B, S)
        o_sw = _swa(sq, sk, sv, sinks, B, S, l)
        x2 = _mix_out(x2, o_rw, o_mb, o_sw, gates, wb, w_out_b, g_mix_post, l)
        k_mem, vt_mem = _mem_kv(mem2, g_mem, w_xk_b, w_xv_b, M, l)
        x2 = _xattn(x2, k_mem, vt_mem, g_xa_pre, w_xq_b, w_xo_b, g_xa_post, B, S, M, l)
        x2 = _mlp(x2, g_mlp_pre, w_up_b, w_down_b, g_mlp_post, l)
    return x2.reshape(B, S, D)
```

```python
import functools

import jax
import jax.numpy as jnp
from jax import lax
from jax.experimental import pallas as pl
from jax.experimental.pallas import tpu as pltpu

F32 = jnp.float32
BF16 = jnp.bfloat16

D_MODEL = 1024
HEAD_DIM = 64
BRANCH = 512
N_HEADS = BRANCH // HEAD_DIM
RW_COLS = 3 * BRANCH + 64 + 64 + 128
LORA_PAD = 128
MB_BLOCK = 256
MB_TOPK = 3
SW_WINDOW = 128
SW_KV_HEADS = 2
XA_HEADS = 4
XA_HEAD_DIM = 128
XA_WIDTH = XA_HEADS * XA_HEAD_DIM
D_FF = 4 * D_MODEL
ROPE_THETA = 10000.0
NORM_EPS = 1e-6
RW_LNX_EPS = 1e-5 * HEAD_DIM
MASK_VALUE = -1e30
LOG2E = 1.4426950408889634

LANES = 128
MXU_TILE = 256
RW_CHUNK = 64
MIX_OUT_SUB = 256
MLP_ROWS = 1024
MLP_COLS = 2048
MLP_SUB = 512
RW_BATCH = 4
VMEM_LIMIT = 56 * 1024 * 1024


def _cparams(*sem):
    return pltpu.CompilerParams(dimension_semantics=sem, vmem_limit_bytes=VMEM_LIMIT)


def _dot(a, b, precision=None):
    return jnp.dot(a, b, preferred_element_type=F32, precision=precision)


def _dot_nt(a, b, precision=None):
    return lax.dot_general(a, b, (((1,), (1,)), ((), ())),
                           preferred_element_type=F32, precision=precision)


def _dot_tn(a, b, precision=None):
    return lax.dot_general(a, b, (((0,), (0,)), ((), ())),
                           preferred_element_type=F32, precision=precision)


def _rms(xf, g):
    ms = jnp.mean(xf * xf, axis=-1, keepdims=True)
    return xf * lax.rsqrt(ms + NORM_EPS) * g


def _const_spec(shape):
    nd = len(shape)
    return pl.BlockSpec(shape, lambda *_: (0,) * nd, pipeline_mode=pl.Buffered(1))


def _layer_spec(shape, layer):
    nd = len(shape)
    return pl.BlockSpec((None,) + tuple(shape), lambda *_: (layer,) + (0,) * nd,
                        pipeline_mode=pl.Buffered(1))


def _rope_kernel(pos_ref, invf_ref, cos_ref, sin_ref):
    ang = invf_ref[...] * pos_ref[...].astype(F32)
    c = jnp.cos(ang)
    s = jnp.sin(ang)
    reps = LANES // HEAD_DIM
    cos_ref[...] = jnp.concatenate([c, c] * reps, axis=0).T
    sin_ref[...] = jnp.concatenate([-s, s] * reps, axis=0).T


def _rope_tables(positions):
    T = positions.size
    tm = 2048
    inv_freq = 1.0 / (ROPE_THETA ** (jnp.arange(0, HEAD_DIM, 2, dtype=F32) / HEAD_DIM))
    return pl.pallas_call(
        _rope_kernel,
        out_shape=(jax.ShapeDtypeStruct((T, LANES), F32),) * 2,
        grid=(T // tm,),
        in_specs=[pl.BlockSpec((1, tm), lambda i: (0, i)), _const_spec((HEAD_DIM // 2, 1))],
        out_specs=(pl.BlockSpec((tm, LANES), lambda i: (i, 0)),) * 2,
        compiler_params=_cparams("parallel"),
        name="rope_tables",
    )(positions.reshape(1, T), inv_freq[:, None])


def _rope_tile(t, cos, sin, lane_lt_half):
    fwd = pltpu.roll(t, LANES - HEAD_DIM // 2, axis=1)
    bwd = pltpu.roll(t, HEAD_DIM // 2, axis=1)
    partner = jnp.where(lane_lt_half, fwd, bwd)
    return t * cos + partner * sin


def _mix_in_kernel(has_vm, x_ref, g_ref, w_ref, *rest):
    if has_vm:
        (wvm_ref, cos_ref, sin_ref,
         urw_ref, vm_ref, mq_ref, mk_ref, mv_ref, sq_ref, sk_ref, sv_ref, gate_ref) = rest
    else:
        cos_ref, sin_ref, urw_ref, mq_ref, mk_ref, mv_ref, sq_ref, sk_ref, sv_ref, gate_ref = rest
    h = _rms(x_ref[...], g_ref[...]).astype(BF16)
    cos = cos_ref[...]
    sin = sin_ref[...]
    lane = lax.broadcasted_iota(jnp.int32, cos.shape, 1)
    lt_half = (lane % HEAD_DIM) < (HEAD_DIM // 2)
    first_head = lane < HEAD_DIM
    q_scale = LOG2E * HEAD_DIM ** -0.5

    col = [0]

    def seg(width):
        a = col[0]
        col[0] += width
        return _dot(h, w_ref[:, a:a + width])

    def rope_chunks(width, scale):
        acc = seg(width)
        return [_rope_tile(acc[:, c * LANES:(c + 1) * LANES], cos, sin, lt_half) * scale
                for c in range(width // LANES)]

    def store_chunks(ref, chunks):
        for c, t in enumerate(chunks):
            ref[:, c * LANES:(c + 1) * LANES] = t.astype(ref.dtype)

    urw_ref[...] = seg(RW_COLS).astype(urw_ref.dtype)
    if has_vm:
        vm_ref[...] = _dot(h, wvm_ref[...])
    store_chunks(mq_ref, rope_chunks(BRANCH, q_scale))
    store_chunks(mk_ref, rope_chunks(BRANCH, 1.0))
    mv_ref[...] = seg(BRANCH).astype(mv_ref.dtype)
    nat = rope_chunks(BRANCH, q_scale)
    half_tiles = len(nat) // 2
    paired = []
    for p in range(len(nat)):
        a, b = nat[p // 2], nat[p // 2 + half_tiles]
        if p % 2 == 0:
            paired.append(jnp.where(first_head, a, pltpu.roll(b, HEAD_DIM, axis=1)))
        else:
            paired.append(jnp.where(first_head, pltpu.roll(a, HEAD_DIM, axis=1), b))
    store_chunks(sq_ref, paired)
    store_chunks(sk_ref, rope_chunks(SW_KV_HEADS * HEAD_DIM, 1.0))
    sv_ref[...] = seg(SW_KV_HEADS * HEAD_DIM).astype(sv_ref.dtype)
    for n in range(3):
        gate_ref[:, n * D_MODEL:(n + 1) * D_MODEL] = jax.nn.sigmoid(seg(D_MODEL)).astype(gate_ref.dtype)


def _mix_in(x2, g, w_in, w_vm, cos_t, sin_t, layer):
    T = x2.shape[0]
    tm = 512
    has_vm = w_vm is not None
    kv_w = SW_KV_HEADS * HEAD_DIM
    widths = [(RW_COLS, BF16)]
    if has_vm:
        widths.append((LORA_PAD, F32))
    widths += [(BRANCH, BF16), (BRANCH, BF16), (BRANCH, BF16), (BRANCH, BF16),
               (kv_w, BF16), (kv_w, BF16), (3 * D_MODEL, BF16)]
    assert sum(w for w, _ in widths) - (LORA_PAD if has_vm else 0) == w_in.shape[2]
    row = lambda i: (i, 0)
    in_specs = [pl.BlockSpec((tm, D_MODEL), row), _layer_spec((1, D_MODEL), layer),
                _layer_spec(w_in.shape[1:], layer)]
    args = [x2, g, w_in]
    if has_vm:
        in_specs.append(_layer_spec(w_vm.shape[1:], layer - 1))
        args.append(w_vm)
    in_specs += [pl.BlockSpec((tm, LANES), row), pl.BlockSpec((tm, LANES), row)]
    args += [cos_t, sin_t]
    return pl.pallas_call(
        functools.partial(_mix_in_kernel, has_vm),
        out_shape=tuple(jax.ShapeDtypeStruct((T, w), dt) for w, dt in widths),
        grid=(T // tm,),
        in_specs=in_specs,
        out_specs=tuple(pl.BlockSpec((tm, w), row) for w, _ in widths),
        compiler_params=_cparams("parallel"),
        name="mix_in",
    )(*args)


def _shift_prev(cur, carry_ref, bi):
    rolled = pltpu.roll(cur, 1, axis=0)
    row = lax.broadcasted_iota(jnp.int32, cur.shape, 0)
    prev = jnp.where(row == 0, carry_ref[bi, 0:1, :], rolled)
    carry_ref[bi, 0:1, :] = cur[cur.shape[0] - 1:, :]
    return prev


def _split_bf16(x, terms):
    pieces = []
    for _ in range(terms):
        hi = x.astype(BF16)
        pieces.append(hi)
        x = x - hi.astype(F32)
    return pieces


def _head_sums(x, bd, terms):
    w = bd.shape[0]
    halves = []
    for c in range(x.shape[1] // w):
        pieces = _split_bf16(x[:, c * w:(c + 1) * w], terms)
        halves.append(sum(_dot(piece, bd) for piece in pieces))
    return jnp.concatenate(halves, axis=1)


def _dot_split_rhs(a, b, terms):
    return sum(_dot(a, piece) for piece in _split_bf16(b, terms))


def _rwkv_kernel(has_vres, nb, *refs):
    if has_vres:
        (u_ref, vm_ref, vf_ref, mu_ref, w0_ref, wup_ref, a0_ref, aup_ref, gup_ref, kk_ref,
         ka_ref, rk_ref, lw_ref, lb_ref, bd_ref, vmu_ref, v0_ref, vup_ref,
         o_ref, s_ref, cu_ref, cvm_ref, hout_ref) = refs
        vout_ref = None
    else:
        (u_ref, mu_ref, w0_ref, wup_ref, a0_ref, aup_ref, gup_ref, kk_ref,
         ka_ref, rk_ref, lw_ref, lb_ref, bd_ref,
         o_ref, vout_ref, s_ref, cu_ref, hout_ref) = refs
    C = RW_CHUNK
    P2 = 2 * C
    npair = BRANCH // LANES

    @pl.when(pl.program_id(1) == 0)
    def _():
        s_ref[...] = jnp.zeros_like(s_ref)
        cu_ref[...] = jnp.zeros_like(cu_ref)
        if has_vres:
            cvm_ref[...] = jnp.zeros_like(cvm_ref)

    bd = bd_ref[...]
    ti = lax.broadcasted_iota(jnp.int32, (C, C), 0)
    si = lax.broadcasted_iota(jnp.int32, (C, C), 1)
    tri = (si <= ti).astype(BF16)
    lane = lax.broadcasted_iota(jnp.int32, (1, LANES), 1)
    first_head = lane < HEAD_DIM
    pr = lax.broadcasted_iota(jnp.int32, (P2, P2), 0)
    pc = lax.broadcasted_iota(jnp.int32, (P2, P2), 1)
    same_head = (pr // C) == (pc // C)
    strict = same_head & ((pc % C) < (pr % C))
    incl = same_head & ((pc % C) <= (pr % C))
    eye = (pr == pc).astype(F32)

    def stack(t):
        return jnp.concatenate([jnp.where(first_head, t, 0.0), jnp.where(first_head, 0.0, t)], axis=0)

    pre = []
    inst = []
    for bi in range(nb):
        u = u_ref[bi].astype(F32)
        ul = u + mu_ref[...] * (_shift_prev(u, cu_ref, bi) - u)
        r = ul[:, 0:BRANCH]
        k = ul[:, BRANCH:2 * BRANCH]
        v = ul[:, 2 * BRANCH:3 * BRANCH]
        x_wa = ul[:, 3 * BRANCH:3 * BRANCH + LANES]
        x_g = ul[:, 3 * BRANCH + LANES:]

        z = w0_ref[...] + _dot(jnp.tanh(x_wa).astype(BF16), wup_ref[...])
        a = jax.nn.sigmoid(a0_ref[...] + _dot(x_wa.astype(BF16), aup_ref[...]))
        g = _dot(jax.nn.sigmoid(x_g).astype(BF16), gup_ref[...])
        softplus = jnp.maximum(-z, 0.0) + jnp.log1p(jnp.exp(-jnp.abs(z)))
        ell = -jnp.exp(-softplus - 0.5)

        if has_vres:
            vm = vm_ref[bi]
            vml = vm + vmu_ref[...] * (_shift_prev(vm, cvm_ref, bi) - vm)
            vgate = jax.nn.sigmoid(v0_ref[...] + _dot(vml.astype(BF16), vup_ref[...]))
            v = v + (vf_ref[bi] - v) * vgate
        else:
            vout_ref[bi] = v

        kk = k * kk_ref[...]
        kk = kk / jnp.maximum(jnp.sqrt(_head_sums(kk * kk, bd, 1)), 1e-12)
        k_eff = k * (1.0 + (a - 1.0) * ka_ref[...])
        b = kk * a
        cum = _dot_split_rhs(tri, ell, 3)
        cum_last = cum[C - 1:, :]
        g_inv = jnp.exp(-cum)
        g_tail = jnp.exp(cum_last - cum)
        kap_t = kk * jnp.exp(cum - ell)
        r_t = r * jnp.exp(cum)
        b_t = b * g_inv
        k_t = k_eff * g_inv
        b_g = b * g_tail
        k_g = k_eff * g_tail
        g_all = jnp.exp(cum_last)
        pre.append((r, k_eff, v, g))
        for p in range(npair):
            sl = slice(p * LANES, (p + 1) * LANES)
            inst.append(dict(bi=bi, p=p, sl=sl, kap=stack(kap_t[:, sl]), r=stack(r_t[:, sl]),
                             b=stack(b_t[:, sl]), k=stack(k_t[:, sl]), v=stack(v[:, sl]),
                             bg=stack(b_g[:, sl]), kg=stack(k_g[:, sl]), gall=g_all[:, sl]))

    def mm(a, b):
        return _dot(a.astype(BF16), b.astype(BF16))

    def mm_nt(a, b):
        return _dot_nt(a.astype(BF16), b.astype(BF16))

    def mm_tn(a, b):
        return _dot_tn(a.astype(BF16), b.astype(BF16))

    for d in inst:
        aa = mm_nt(jnp.concatenate([d["kap"], d["r"]], axis=0),
                   jnp.concatenate([d["b"], d["k"]], axis=0))
        d["n"] = jnp.where(strict, -aa[:P2, :P2], 0.0)
        d["a_ak"] = jnp.where(strict, aa[:P2, P2:], 0.0)
        d["a_r"] = jnp.where(jnp.concatenate([incl, incl], axis=1), aa[P2:, :], 0.0)
    for d in inst:
        d["akv"] = mm(d["a_ak"], d["v"])
    for d in inst:
        d["t"] = eye + d["n"]
        d["pw"] = mm(d["n"], d["n"])
    for _ in range(4):
        for d in inst:
            pt = mm(d["pw"], jnp.concatenate([d["pw"], d["t"]], axis=1))
            d["pw"] = pt[:, :P2]
            d["t"] = d["t"] + pt[:, P2:]
    for d in inst:
        d["t"] = d["t"] + mm(d["pw"], d["t"])
    for d in inst:
        ty = mm(d["t"], jnp.concatenate([d["kap"], d["akv"]], axis=1))
        d["w_til"], d["u_til"] = ty[:, :LANES], ty[:, LANES:]
    for d in inst:
        d["state"] = s_ref[d["bi"] * npair + d["p"]]
        xr = mm_nt(jnp.concatenate([d["w_til"], d["r"]], axis=0), d["state"])
        sa = -(xr[:P2] + d["u_til"])
        d["sv"] = jnp.concatenate([sa, d["v"]], axis=0)
        d["rs"] = xr[P2:]
    for d in inst:
        s_ref[d["bi"] * npair + d["p"]] = (
            d["state"] * d["gall"] + mm_tn(d["sv"], jnp.concatenate([d["bg"], d["kg"]], axis=0)))
    for d in inst:
        o_st = d["rs"] + mm(d["a_r"], d["sv"])
        hout_ref[d["bi"], :, d["sl"]] = o_st[:C] + o_st[C:]

    inv_n = 1.0 / HEAD_DIM
    for bi in range(nb):
        r, k_eff, v, g = pre[bi]
        out = hout_ref[bi]
        mean = _head_sums(out, bd, 2) * inv_n
        cen = out - mean
        var = _head_sums(cen * cen, bd, 1) * inv_n
        normed = cen * lax.rsqrt(var + RW_LNX_EPS) * lw_ref[...] + lb_ref[...]
        bonus = _head_sums(r * k_eff * rk_ref[...], bd, 2) * v
        o_ref[bi] = ((normed + bonus) * g).astype(o_ref.dtype)


def _rwkv(u_rw, p, layer, B, S, vres=None):
    T = u_rw.shape[0]
    C = RW_CHUNK
    nb = RW_BATCH
    nc = S // C
    has_vres = vres is not None
    blk = lambda w: pl.BlockSpec((nb, C, w), lambda b, c: (b, c, 0))
    names = ("mu", "w0", "w_up", "a0", "a_up", "g_up", "k_k", "k_a", "r_k", "lnx_w", "lnx_b")
    in_specs = [blk(RW_COLS)]
    args = [u_rw.reshape(B, S, RW_COLS)]
    if has_vres:
        in_specs += [blk(LORA_PAD), blk(BRANCH)]
        args += [vres["vm"].reshape(B, S, LORA_PAD), vres["v_first"].reshape(B, S, BRANCH)]
    in_specs += [_layer_spec(p[n].shape[1:], layer) for n in names]
    in_specs.append(_const_spec((MXU_TILE, MXU_TILE)))
    args += [p[n] for n in names] + [p["bd"]]
    scratch = [pltpu.VMEM((nb * (BRANCH // LANES), 2 * C, LANES), F32),
               pltpu.VMEM((nb, 8, RW_COLS), F32)]
    if has_vres:
        vnames = ("mu", "v0", "up")
        in_specs += [_layer_spec(vres[n].shape[1:], layer - 1) for n in vnames]
        args += [vres[n] for n in vnames]
        out_shape = jax.ShapeDtypeStruct((B, S, BRANCH), BF16)
        out_specs = blk(BRANCH)
        scratch.append(pltpu.VMEM((nb, 8, LORA_PAD), F32))
    else:
        out_shape = (jax.ShapeDtypeStruct((B, S, BRANCH), BF16),
                     jax.ShapeDtypeStruct((B, S, BRANCH), F32))
        out_specs = (blk(BRANCH), blk(BRANCH))
    scratch.append(pltpu.VMEM((nb, C, BRANCH), F32))
    outs = pl.pallas_call(
        functools.partial(_rwkv_kernel, has_vres, nb),
        out_shape=out_shape,
        grid=(B // nb, nc),
        in_specs=in_specs,
        out_specs=out_specs,
        scratch_shapes=scratch,
        compiler_params=_cparams("parallel", "arbitrary"),
        name="rwkv",
    )(*args)
    if has_vres:
        return outs.reshape(T, BRANCH)
    return outs[0].reshape(T, BRANCH), outs[1].reshape(T, BRANCH)


def _moba_kernel(q_ref, k_ref, v_ref, o_ref, kmean_ref, vt_ref):
    S = q_ref.shape[0]
    nb = S // MB_BLOCK
    tq = MB_BLOCK
    n_heads = LANES // HEAD_DIM
    for n in range(nb):
        blk = k_ref[n * MB_BLOCK:(n + 1) * MB_BLOCK, :].astype(F32)
        kmean_ref[n:n + 1, :] = jnp.mean(blk, axis=0, keepdims=True)
    kmean = kmean_ref[...]
    vt_ref[...] = v_ref[...].astype(F32).T.astype(BF16)
    lane = lax.broadcasted_iota(jnp.int32, (1, LANES), 1)
    head_lanes = [(lane // HEAD_DIM) == hh for hh in range(n_heads)]

    blk_id = lax.broadcasted_iota(jnp.int32, (nb, S), 0)
    own = lax.broadcasted_iota(jnp.int32, (nb, S), 1) // MB_BLOCK
    valid = blk_id < own
    q_all = q_ref[...]
    kmean_rep = jnp.concatenate(
        [kmean] + [jnp.broadcast_to(kmean[m:m + 1, :], (nb, LANES)) for m in range(nb - 1)], axis=0)
    sels = []
    for hh in range(n_heads):
        pieces = _split_bf16(jnp.where(head_lanes[hh], kmean_rep, 0.0), 3)
        gates = sum(_dot_nt(piece, q_all) for piece in pieces)
        gate = gates[:nb]
        cnt = jnp.zeros((nb, S), jnp.int32)
        for m in range(nb - 1):
            gm = gates[nb * (m + 1):nb * (m + 2)]
            beats = (own > m) & ((gm > gate) | ((gm == gate) & (m < blk_id)))
            cnt = cnt + beats.astype(jnp.int32)
        sels.append((valid & (cnt < MB_TOPK)).astype(F32))

    ki = lax.broadcasted_iota(jnp.int32, (tq, tq), 0)
    qi = lax.broadcasted_iota(jnp.int32, (tq, tq), 1)
    causal = ki <= qi

    def scores(i):
        q_t = q_ref[i * tq:(i + 1) * tq, :]
        k_all = k_ref[0:(i + 1) * MB_BLOCK, :]
        return [_dot_nt(k_all, jnp.where(head_lanes[hh], q_t, jnp.zeros_like(q_t)))
                for hh in range(n_heads)]

    s_cur = scores(0)
    for i in range(nb):
        s_next = scores(i + 1) if i + 1 < nb else None
        probs = []
        for hh in range(n_heads):
            s = s_cur[hh]
            s_own = jnp.where(causal, s[i * MB_BLOCK:, :], MASK_VALUE)
            m_col = jnp.max(s_own, axis=0, keepdims=True)
            keeps = []
            for j in range(i):
                keep = sels[hh][j:j + 1, i * tq:(i + 1) * tq] > 0.5
                blk_max = jnp.max(s[j * MB_BLOCK:(j + 1) * MB_BLOCK, :], axis=0, keepdims=True)
                m_col = jnp.maximum(m_col, jnp.where(keep, blk_max, MASK_VALUE))
                keeps.append(keep)
            parts = [jnp.exp2(s[j * MB_BLOCK:(j + 1) * MB_BLOCK, :] - jnp.where(keeps[j], m_col, -MASK_VALUE))
                     for j in range(i)]
            parts.append(jnp.exp2(s_own - m_col))
            p = parts[0] if len(parts) == 1 else jnp.concatenate(parts, axis=0)
            probs.append((p.astype(BF16), jnp.sum(p, axis=0, keepdims=True)))
        vt = vt_ref[:, 0:(i + 1) * MB_BLOCK]
        ot = [_dot(vt, p) / denom for p, denom in probs]
        o_t = jnp.concatenate([ot[0][:HEAD_DIM], ot[1][HEAD_DIM:]], axis=0).T
        o_ref[i * tq:(i + 1) * tq, :] = o_t.astype(o_ref.dtype)
        s_cur = s_next


def _moba(q, k, v, B, S):
    T = q.shape[0]
    npair = BRANCH // LANES
    spec = pl.BlockSpec((S, LANES), lambda b, p: (b, p))
    return pl.pallas_call(
        _moba_kernel,
        out_shape=jax.ShapeDtypeStruct((T, BRANCH), BF16),
        grid=(B, npair),
        in_specs=[spec, spec, spec],
        out_specs=spec,
        scratch_shapes=[pltpu.VMEM((S // MB_BLOCK, LANES), F32), pltpu.VMEM((LANES, S), BF16)],
        compiler_params=_cparams("parallel", "parallel"),
        name="moba",
    )(q, k, v)


def _swa_kernel(q_ref, k_ref, v_ref, sink_ref, o_ref, vt_ref):
    S = q_ref.shape[0]
    W = SW_WINDOW
    n_heads = LANES // HEAD_DIM
    vt_ref[...] = v_ref[...].astype(F32).T.astype(BF16)
    lane = lax.broadcasted_iota(jnp.int32, (1, LANES), 1)
    dist = (lax.broadcasted_iota(jnp.int32, (2 * W, W), 1) + W
            - lax.broadcasted_iota(jnp.int32, (2 * W, W), 0))
    band = (dist >= 0) & (dist < W)
    causal = (lax.broadcasted_iota(jnp.int32, (W, W), 0)
              <= lax.broadcasted_iota(jnp.int32, (W, W), 1))
    head_lanes = [(lane // HEAD_DIM) == hh for hh in range(n_heads)]
    sinks = [sink_ref[0, hh:hh + 1, 0:1] * LOG2E for hh in range(n_heads)]
    group = 4

    def scores(g0):
        work = []
        for n in range(g0, g0 + group):
            k0 = max(n - 1, 0) * W
            k1 = (n + 1) * W
            mask = causal if n == 0 else band
            q_t = q_ref[n * W:(n + 1) * W, :]
            k_w = k_ref[k0:k1, :]
            for hh in range(n_heads):
                q_h = jnp.where(head_lanes[hh], q_t, jnp.zeros_like(q_t))
                work.append(dict(n=n, hh=hh, k0=k0, k1=k1, mask=mask, s=_dot_nt(k_w, q_h)))
        return work

    work_next = scores(0)
    for g0 in range(0, S // W, group):
        work = work_next
        if g0 + group < S // W:
            work_next = scores(g0 + group)
        for d in work:
            s = jnp.where(d["mask"], d["s"], MASK_VALUE)
            sink = sinks[d["hh"]]
            m_col = jnp.maximum(jnp.max(s, axis=0, keepdims=True), sink)
            p = jnp.exp2(s - m_col)
            d["denom"] = jnp.sum(p, axis=0, keepdims=True) + jnp.exp2(sink - m_col)
            d["p"] = p.astype(BF16)
        for d in work:
            d["ot"] = _dot(vt_ref[:, d["k0"]:d["k1"]], d["p"]) / d["denom"]
        o_t = jnp.concatenate(
            [jnp.concatenate([a["ot"][:HEAD_DIM], b["ot"][HEAD_DIM:]], axis=0)
             for a, b in zip(work[0::2], work[1::2])], axis=1)
        o_ref[g0 * W:(g0 + group) * W, :] = o_t.T.astype(o_ref.dtype)


def _swa(q, k, v, sinks, B, S, layer):
    T = q.shape[0]
    npair = BRANCH // LANES
    spec = pl.BlockSpec((S, LANES), lambda b, p: (b, p))
    kv_spec = pl.BlockSpec((S, LANES), lambda b, p: (b, 0))
    return pl.pallas_call(
        _swa_kernel,
        out_shape=jax.ShapeDtypeStruct((T, BRANCH), BF16),
        grid=(B, npair),
        in_specs=[spec, kv_spec, kv_spec,
                  pl.BlockSpec((None, 1, 2, LANES), lambda b, p: (layer, p, 0, 0))],
        out_specs=spec,
        scratch_shapes=[pltpu.VMEM((LANES, S), BF16)],
        compiler_params=_cparams("parallel", "parallel"),
        name="swa",
    )(q, k, v, sinks)


def _mix_out_kernel(x_ref, orw_ref, omb_ref, osw_ref, gate_ref, wb_ref, wo_ref, g_ref, out_ref):
    tm = x_ref.shape[0]
    subs = [slice(r, r + MIX_OUT_SUB) for r in range(0, tm, MIX_OUT_SUB)]

    def gated(rows):
        y = None
        for n, o_ref in enumerate((orw_ref, omb_ref, osw_ref)):
            t = gate_ref[rows, n * D_MODEL:(n + 1) * D_MODEL] * _dot(o_ref[rows, :], wb_ref[n])
            y = t if y is None else y + t
        return y.astype(BF16)

    y_next = gated(subs[0])
    for i, rows in enumerate(subs):
        y = y_next
        if i + 1 < len(subs):
            y_next = gated(subs[i + 1])
        out_ref[rows, :] = x_ref[rows, :] + _rms(_dot(y, wo_ref[...]), g_ref[...])


def _mix_out(x2, o_rw, o_mb, o_sw, gates, w_branch, w_out, g, layer):
    T = x2.shape[0]
    tm = 512
    row = lambda i: (i, 0)
    br = pl.BlockSpec((tm, BRANCH), row)
    return pl.pallas_call(
        _mix_out_kernel,
        out_shape=jax.ShapeDtypeStruct((T, D_MODEL), F32),
        grid=(T // tm,),
        in_specs=[pl.BlockSpec((tm, D_MODEL), row), br, br, br,
                  pl.BlockSpec((tm, 3 * D_MODEL), row),
                  _layer_spec(w_branch.shape[1:], layer), _layer_spec(w_out.shape[1:], layer),
                  _layer_spec((1, D_MODEL), layer)],
        out_specs=pl.BlockSpec((tm, D_MODEL), row),
        compiler_params=_cparams("parallel"),
        name="mix_out",
    )(x2, o_rw, o_mb, o_sw, gates, w_branch, w_out, g)


def _mem_kv_kernel(m_ref, g_ref, wk_ref, wv_ref, k_ref, vt_ref):
    m = _rms(m_ref[...], g_ref[...]).astype(BF16)
    k_ref[...] = _dot(m, wk_ref[...]).astype(k_ref.dtype)
    vt_ref[...] = _dot(m, wv_ref[...]).T.astype(vt_ref.dtype)


def _mem_kv(mem2, g, wk, wv, M, layer):
    R = mem2.shape[0]
    row = lambda i: (i, 0)
    return pl.pallas_call(
        _mem_kv_kernel,
        out_shape=(jax.ShapeDtypeStruct((R, XA_WIDTH), BF16),
                   jax.ShapeDtypeStruct((R // M, XA_WIDTH, M), BF16)),
        grid=(R // M,),
        in_specs=[pl.BlockSpec((M, D_MODEL), row), _layer_spec((1, D_MODEL), layer),
                  _layer_spec(wk.shape[1:], layer), _layer_spec(wv.shape[1:], layer)],
        out_specs=(pl.BlockSpec((M, XA_WIDTH), row),
                   pl.BlockSpec((None, XA_WIDTH, M), lambda i: (i, 0, 0))),
        compiler_params=_cparams("parallel"),
        name="mem_kv",
    )(mem2, g, wk, wv)


def _xattn_kernel(x_ref, k_ref, vt_ref, gpre_ref, wq_ref, wo_ref, gpost_ref, out_ref):
    tm = x_ref.shape[0]
    groups = [slice(r, r + tm // 2) for r in range(0, tm, tm // 2)]
    heads = [slice(hd * XA_HEAD_DIM, (hd + 1) * XA_HEAD_DIM) for hd in range(XA_HEADS)]
    qs = []
    for rows in groups:
        h = _rms(x_ref[rows, :], gpre_ref[...]).astype(BF16)
        qs.append((_dot(h, wq_ref[...]) * (LOG2E * XA_HEAD_DIM ** -0.5)).astype(BF16))
    scores = [[_dot_nt(k_ref[:, sl], q[:, sl]) for sl in heads] for q in qs]
    outs = []
    for grp in scores:
        probs = []
        for s in grp:
            p = jnp.exp2(s - jnp.max(s, axis=0, keepdims=True))
            probs.append((p.astype(BF16), jnp.sum(p, axis=0, keepdims=True)))
        o_t = jnp.concatenate([_dot(vt_ref[sl, :], p) / denom
                               for sl, (p, denom) in zip(heads, probs)], axis=0)
        outs.append(o_t.T.astype(BF16))
    ys = [_dot(o, wo_ref[...]) for o in outs]
    for rows, y in zip(groups, ys):
        out_ref[rows, :] = x_ref[rows, :] + _rms(y, gpost_ref[...])


def _xattn(x2, k_mem, vt_mem, g_pre, wq, wo, g_post, B, S, M, layer):
    T = x2.shape[0]
    tm = 512
    nt = S // tm
    row = lambda b, i: (b * nt + i, 0)
    return pl.pallas_call(
        _xattn_kernel,
        out_shape=jax.ShapeDtypeStruct((T, D_MODEL), F32),
        grid=(B, nt),
        in_specs=[pl.BlockSpec((tm, D_MODEL), row),
                  pl.BlockSpec((M, XA_WIDTH), lambda b, i: (b, 0)),
                  pl.BlockSpec((None, XA_WIDTH, M), lambda b, i: (b, 0, 0)),
                  _layer_spec((1, D_MODEL), layer), _layer_spec(wq.shape[1:], layer),
                  _layer_spec(wo.shape[1:], layer), _layer_spec((1, D_MODEL), layer)],
        out_specs=pl.BlockSpec((tm, D_MODEL), row),
        compiler_params=_cparams("parallel", "parallel"),
        name="xattn",
    )(x2, k_mem, vt_mem, g_pre, wq, wo, g_post)


def _mlp_kernel(x_ref, gpre_ref, wup_ref, wdn_ref, gpost_ref, out_ref, h_ref, acc_ref):
    j = pl.program_id(1)

    @pl.when(j == 0)
    def _():
        h_ref[...] = _rms(x_ref[...], gpre_ref[...]).astype(BF16)
        acc_ref[...] = jnp.zeros_like(acc_ref)

    h = h_ref[...]
    n_sub = wup_ref.shape[1] // MLP_SUB

    def up(c):
        return _dot(h, wup_ref[:, c * MLP_SUB:(c + 1) * MLP_SUB])

    a_next = up(0)
    acc = None
    for c in range(n_sub):
        a = jnp.maximum(a_next, 0.0)
        if c + 1 < n_sub:
            a_next = up(c + 1)
        t = _dot((a * a).astype(BF16), wdn_ref[c * MLP_SUB:(c + 1) * MLP_SUB, :])
        acc = t if acc is None else acc + t
    acc_ref[...] += acc

    @pl.when(j == pl.num_programs(1) - 1)
    def _():
        out_ref[...] = x_ref[...] + _rms(acc_ref[...], gpost_ref[...])


def _mlp(x2, g_pre, w_up, w_down, g_post, layer):
    T = x2.shape[0]
    tm = MLP_ROWS
    tf = MLP_COLS
    row = lambda i, j: (i, 0)
    return pl.pallas_call(
        _mlp_kernel,
        out_shape=jax.ShapeDtypeStruct((T, D_MODEL), F32),
        grid=(T // tm, D_FF // tf),
        in_specs=[pl.BlockSpec((tm, D_MODEL), row), _layer_spec((1, D_MODEL), layer),
                  pl.BlockSpec((None, D_MODEL, tf), lambda i, j: (layer, 0, j)),
                  pl.BlockSpec((None, tf, D_MODEL), lambda i, j: (layer, j, 0)),
                  _layer_spec((1, D_MODEL), layer)],
        out_specs=pl.BlockSpec((tm, D_MODEL), row),
        scratch_shapes=[pltpu.VMEM((tm, D_MODEL), BF16), pltpu.VMEM((tm, D_MODEL), F32)],
        compiler_params=_cparams("parallel", "arbitrary"),
        name="mlp",
    )(x2, g_pre, w_up, w_down, g_post)


def kernel(x, mem, positions, norm_mix_pre, norm_mix_post, norm_xattn_pre, norm_xattn_post, norm_mem, norm_mlp_pre, norm_mlp_post, w_in, rw_mu, rw_w0, rw_w_up, rw_a0, rw_a_up, rw_g_up, rw_k_k, rw_k_a, rw_r_k, rw_lnx_w, rw_lnx_b, rw_vres_down, rw_vres_mu, rw_v0, rw_vres_up, sw_sinks, w_branch, w_out, w_xq, w_xk, w_xv, w_xo, w_up, w_down):
    B, S, D = x.shape
    M = mem.shape[1]
    depth = w_in.shape[0]
    T = B * S
    x2 = x.reshape(T, D)
    mem2 = mem.reshape(B * M, D)
    cos_t, sin_t = _rope_tables(positions)

    rows = lambda t: t.reshape(t.shape[0], 1, -1)
    bf = lambda t: t.astype(BF16)
    lora_in = rw_w_up.shape[1]
    per_kv = N_HEADS // SW_KV_HEADS
    head_of_col = jnp.arange(MXU_TILE) // HEAD_DIM
    rw_p = dict(
        mu=rows(rw_mu), w0=rows(rw_w0), a0=rows(rw_a0), k_k=rows(rw_k_k), k_a=rows(rw_k_a),
        r_k=rows(rw_r_k), lnx_w=rows(rw_lnx_w), lnx_b=rows(rw_lnx_b),
        w_up=bf(jnp.pad(rw_w_up, ((0, 0), (0, LORA_PAD - lora_in), (0, 0)))),
        a_up=bf(jnp.pad(rw_a_up, ((0, 0), (lora_in, LORA_PAD - lora_in - rw_a_up.shape[1]), (0, 0)))),
        g_up=bf(rw_g_up),
        bd=(head_of_col[:, None] == head_of_col[None, :]).astype(BF16))
    mv_pad = LORA_PAD - rw_vres_down.shape[2]
    w_vm = bf(jnp.pad(rw_vres_down, ((0, 0), (0, 0), (0, mv_pad))))
    vres_p = dict(mu=rows(jnp.pad(rw_vres_mu, ((0, 0), (0, mv_pad)))), v0=rows(rw_v0),
                  up=bf(jnp.pad(rw_vres_up, ((0, 0), (0, mv_pad), (0, 0)))))
    sinks = jnp.broadcast_to(
        sw_sinks.reshape(depth, SW_KV_HEADS, per_kv).transpose(0, 2, 1)[..., None].astype(F32),
        (depth, per_kv, SW_KV_HEADS, LANES))
    wb_sw = w_branch[:, 2].reshape(depth, SW_KV_HEADS, per_kv, HEAD_DIM, D)
    wb_sw = wb_sw.transpose(0, 2, 1, 3, 4).reshape(depth, 1, BRANCH, D)
    wb = bf(jnp.concatenate([w_branch[:, :2], wb_sw], axis=1))
    w_in_b, w_out_b, w_up_b, w_down_b = bf(w_in), bf(w_out), bf(w_up), bf(w_down)
    w_xq_b, w_xk_b, w_xv_b, w_xo_b = bf(w_xq), bf(w_xk), bf(w_xv), bf(w_xo)
    g_mix_pre, g_mix_post = rows(norm_mix_pre), rows(norm_mix_post)
    g_xa_pre, g_xa_post, g_mem = rows(norm_xattn_pre), rows(norm_xattn_post), rows(norm_mem)
    g_mlp_pre, g_mlp_post = rows(norm_mlp_pre), rows(norm_mlp_post)

    v_first = None
    for l in range(depth):
        outs = _mix_in(x2, g_mix_pre, w_in_b, w_vm if l > 0 else None, cos_t, sin_t, l)
        if l > 0:
            u_rw, vm, mq, mk, mv, sq, sk, sv, gates = outs
            o_rw = _rwkv(u_rw, rw_p, l, B, S, dict(vres_p, vm=vm, v_first=v_first))
        else:
            u_rw, mq, mk, mv, sq, sk, sv, gates = outs
            o_rw, v_first = _rwkv(u_rw, rw_p, l, B, S)
        o_mb = _moba(mq, mk, mv, B, S)
        o_sw = _swa(sq, sk, sv, sinks, B, S, l)
        x2 = _mix_out(x2, o_rw, o_mb, o_sw, gates, wb, w_out_b, g_mix_post, l)
        k_mem, vt_mem = _mem_kv(mem2, g_mem, w_xk_b, w_xv_b, M, l)
        x2 = _xattn(x2, k_mem, vt_mem, g_xa_pre, w_xq_b, w_xo_b, g_xa_post, B, S, M, l)
        x2 = _mlp(x2, g_mlp_pre, w_up_b, w_down_b, g_mlp_post, l)
    return x2.reshape(B, S, D)
```

```python
import functools

import jax
import jax.numpy as jnp
from jax import lax
from jax.experimental import pallas as pl
from jax.experimental.pallas import tpu as pltpu

F32 = jnp.float32
BF16 = jnp.bfloat16

D_MODEL = 1024
HEAD_DIM = 64
BRANCH = 512
N_HEADS = BRANCH // HEAD_DIM
RW_COLS = 3 * BRANCH + 64 + 64 + 128
LORA_PAD = 128
MB_BLOCK = 256
MB_TOPK = 3
SW_WINDOW = 128
SW_KV_HEADS = 2
XA_HEADS = 4
XA_HEAD_DIM = 128
XA_WIDTH = XA_HEADS * XA_HEAD_DIM
D_FF = 4 * D_MODEL
ROPE_THETA = 10000.0
NORM_EPS = 1e-6
RW_LNX_EPS = 1e-5 * HEAD_DIM
MASK_VALUE = -1e30
LOG2E = 1.4426950408889634

LANES = 128
MXU_TILE = 256
RW_CHUNK = 64
MIX_OUT_SUB = 512
MLP_ROWS = 1024
MLP_COLS = 2048
MLP_SUB = 512
RW_BATCH = 4
VMEM_LIMIT = 56 * 1024 * 1024


def _cparams(*sem):
    return pltpu.CompilerParams(dimension_semantics=sem, vmem_limit_bytes=VMEM_LIMIT)


def _dot(a, b, precision=None):
    return jnp.dot(a, b, preferred_element_type=F32, precision=precision)


def _dot_nt(a, b, precision=None):
    return lax.dot_general(a, b, (((1,), (1,)), ((), ())),
                           preferred_element_type=F32, precision=precision)


def _dot_tn(a, b, precision=None):
    return lax.dot_general(a, b, (((0,), (0,)), ((), ())),
                           preferred_element_type=F32, precision=precision)


def _rms(xf, g):
    ms = jnp.mean(xf * xf, axis=-1, keepdims=True)
    return xf * lax.rsqrt(ms + NORM_EPS) * g


def _const_spec(shape):
    nd = len(shape)
    return pl.BlockSpec(shape, lambda *_: (0,) * nd, pipeline_mode=pl.Buffered(1))


def _layer_spec(shape, layer):
    nd = len(shape)
    return pl.BlockSpec((None,) + tuple(shape), lambda *_: (layer,) + (0,) * nd,
                        pipeline_mode=pl.Buffered(1))


def _rope_kernel(pos_ref, invf_ref, cos_ref, sin_ref):
    ang = invf_ref[...] * pos_ref[...].astype(F32)
    c = jnp.cos(ang)
    s = jnp.sin(ang)
    reps = LANES // HEAD_DIM
    cos_ref[...] = jnp.concatenate([c, c] * reps, axis=0).T
    sin_ref[...] = jnp.concatenate([-s, s] * reps, axis=0).T


def _rope_tables(positions):
    T = positions.size
    tm = 2048
    inv_freq = 1.0 / (ROPE_THETA ** (jnp.arange(0, HEAD_DIM, 2, dtype=F32) / HEAD_DIM))
    return pl.pallas_call(
        _rope_kernel,
        out_shape=(jax.ShapeDtypeStruct((T, LANES), F32),) * 2,
        grid=(T // tm,),
        in_specs=[pl.BlockSpec((1, tm), lambda i: (0, i)), _const_spec((HEAD_DIM // 2, 1))],
        out_specs=(pl.BlockSpec((tm, LANES), lambda i: (i, 0)),) * 2,
        compiler_params=_cparams("parallel"),
        name="rope_tables",
    )(positions.reshape(1, T), inv_freq[:, None])


def _rope_tile(t, cos, sin, lane_lt_half):
    fwd = pltpu.roll(t, LANES - HEAD_DIM // 2, axis=1)
    bwd = pltpu.roll(t, HEAD_DIM // 2, axis=1)
    partner = jnp.where(lane_lt_half, fwd, bwd)
    return t * cos + partner * sin


def _mix_in_kernel(has_vm, x_ref, g_ref, w_ref, *rest):
    if has_vm:
        (wvm_ref, cos_ref, sin_ref,
         urw_ref, vm_ref, mq_ref, mk_ref, mv_ref, sq_ref, sk_ref, sv_ref, gate_ref) = rest
    else:
        cos_ref, sin_ref, urw_ref, mq_ref, mk_ref, mv_ref, sq_ref, sk_ref, sv_ref, gate_ref = rest
    h = _rms(x_ref[...], g_ref[...]).astype(BF16)
    cos = cos_ref[...]
    sin = sin_ref[...]
    lane = lax.broadcasted_iota(jnp.int32, cos.shape, 1)
    lt_half = (lane % HEAD_DIM) < (HEAD_DIM // 2)
    first_head = lane < HEAD_DIM
    q_scale = LOG2E * HEAD_DIM ** -0.5

    col = [0]

    def seg(width):
        a = col[0]
        col[0] += width
        return _dot(h, w_ref[:, a:a + width])

    def rope_chunks(width, scale):
        acc = seg(width)
        return [_rope_tile(acc[:, c * LANES:(c + 1) * LANES], cos, sin, lt_half) * scale
                for c in range(width // LANES)]

    def store_chunks(ref, chunks):
        for c, t in enumerate(chunks):
            ref[:, c * LANES:(c + 1) * LANES] = t.astype(ref.dtype)

    urw_ref[...] = seg(RW_COLS).astype(urw_ref.dtype)
    if has_vm:
        vm_ref[...] = _dot(h, wvm_ref[...])
    store_chunks(mq_ref, rope_chunks(BRANCH, q_scale))
    store_chunks(mk_ref, rope_chunks(BRANCH, 1.0))
    mv_ref[...] = seg(BRANCH).astype(mv_ref.dtype)
    nat = rope_chunks(BRANCH, q_scale)
    half_tiles = len(nat) // 2
    paired = []
    for p in range(len(nat)):
        a, b = nat[p // 2], nat[p // 2 + half_tiles]
        if p % 2 == 0:
            paired.append(jnp.where(first_head, a, pltpu.roll(b, HEAD_DIM, axis=1)))
        else:
            paired.append(jnp.where(first_head, pltpu.roll(a, HEAD_DIM, axis=1), b))
    store_chunks(sq_ref, paired)
    store_chunks(sk_ref, rope_chunks(SW_KV_HEADS * HEAD_DIM, 1.0))
    sv_ref[...] = seg(SW_KV_HEADS * HEAD_DIM).astype(sv_ref.dtype)
    for n in range(3):
        gate_ref[:, n * D_MODEL:(n + 1) * D_MODEL] = jax.nn.sigmoid(seg(D_MODEL)).astype(gate_ref.dtype)


def _mix_in(x2, g, w_in, w_vm, cos_t, sin_t, layer):
    T = x2.shape[0]
    tm = 512
    has_vm = w_vm is not None
    kv_w = SW_KV_HEADS * HEAD_DIM
    widths = [(RW_COLS, BF16)]
    if has_vm:
        widths.append((LORA_PAD, F32))
    widths += [(BRANCH, BF16), (BRANCH, BF16), (BRANCH, BF16), (BRANCH, BF16),
               (kv_w, BF16), (kv_w, BF16), (3 * D_MODEL, BF16)]
    assert sum(w for w, _ in widths) - (LORA_PAD if has_vm else 0) == w_in.shape[2]
    row = lambda i: (i, 0)
    in_specs = [pl.BlockSpec((tm, D_MODEL), row), _layer_spec((1, D_MODEL), layer),
                _layer_spec(w_in.shape[1:], layer)]
    args = [x2, g, w_in]
    if has_vm:
        in_specs.append(_layer_spec(w_vm.shape[1:], layer - 1))
        args.append(w_vm)
    in_specs += [pl.BlockSpec((tm, LANES), row), pl.BlockSpec((tm, LANES), row)]
    args += [cos_t, sin_t]
    return pl.pallas_call(
        functools.partial(_mix_in_kernel, has_vm),
        out_shape=tuple(jax.ShapeDtypeStruct((T, w), dt) for w, dt in widths),
        grid=(T // tm,),
        in_specs=in_specs,
        out_specs=tuple(pl.BlockSpec((tm, w), row) for w, _ in widths),
        compiler_params=_cparams("parallel"),
        name="mix_in",
    )(*args)


def _shift_prev(cur, carry_ref, bi):
    rolled = pltpu.roll(cur, 1, axis=0)
    row = lax.broadcasted_iota(jnp.int32, cur.shape, 0)
    prev = jnp.where(row == 0, carry_ref[bi, 0:1, :], rolled)
    carry_ref[bi, 0:1, :] = cur[cur.shape[0] - 1:, :]
    return prev


def _split_bf16(x, terms):
    pieces = []
    for _ in range(terms):
        hi = x.astype(BF16)
        pieces.append(hi)
        x = x - hi.astype(F32)
    return pieces


def _head_sums(x, bd, terms):
    w = bd.shape[0]
    halves = []
    for c in range(x.shape[1] // w):
        pieces = _split_bf16(x[:, c * w:(c + 1) * w], terms)
        halves.append(sum(_dot(piece, bd) for piece in pieces))
    return jnp.concatenate(halves, axis=1)


def _dot_split_rhs(a, b, terms):
    return sum(_dot(a, piece) for piece in _split_bf16(b, terms))


def _rwkv_kernel(has_vres, nb, *refs):
    if has_vres:
        (u_ref, vm_ref, vf_ref, mu_ref, w0_ref, wup_ref, a0_ref, aup_ref, gup_ref, kk_ref,
         ka_ref, rk_ref, lw_ref, lb_ref, bd_ref, vmu_ref, v0_ref, vup_ref,
         o_ref, s_ref, cu_ref, cvm_ref, hout_ref) = refs
        vout_ref = None
    else:
        (u_ref, mu_ref, w0_ref, wup_ref, a0_ref, aup_ref, gup_ref, kk_ref,
         ka_ref, rk_ref, lw_ref, lb_ref, bd_ref,
         o_ref, vout_ref, s_ref, cu_ref, hout_ref) = refs
    C = RW_CHUNK
    P2 = 2 * C
    npair = BRANCH // LANES

    @pl.when(pl.program_id(1) == 0)
    def _():
        s_ref[...] = jnp.zeros_like(s_ref)
        cu_ref[...] = jnp.zeros_like(cu_ref)
        if has_vres:
            cvm_ref[...] = jnp.zeros_like(cvm_ref)

    bd = bd_ref[...]
    ti = lax.broadcasted_iota(jnp.int32, (C, C), 0)
    si = lax.broadcasted_iota(jnp.int32, (C, C), 1)
    tri = (si <= ti).astype(BF16)
    lane = lax.broadcasted_iota(jnp.int32, (1, LANES), 1)
    first_head = lane < HEAD_DIM
    pr = lax.broadcasted_iota(jnp.int32, (P2, P2), 0)
    pc = lax.broadcasted_iota(jnp.int32, (P2, P2), 1)
    same_head = (pr // C) == (pc // C)
    strict = same_head & ((pc % C) < (pr % C))
    incl = same_head & ((pc % C) <= (pr % C))
    eye = (pr == pc).astype(F32)

    def stack(t):
        t = t.astype(BF16)
        zero = jnp.zeros_like(t)
        return jnp.concatenate([jnp.where(first_head, t, zero), jnp.where(first_head, zero, t)], axis=0)

    pre = []
    inst = []
    for bi in range(nb):
        u = u_ref[bi].astype(F32)
        ul = u + mu_ref[...] * (_shift_prev(u, cu_ref, bi) - u)
        r = ul[:, 0:BRANCH]
        k = ul[:, BRANCH:2 * BRANCH]
        v = ul[:, 2 * BRANCH:3 * BRANCH]
        x_wa = ul[:, 3 * BRANCH:3 * BRANCH + LANES]
        x_g = ul[:, 3 * BRANCH + LANES:]

        z = w0_ref[...] + _dot(jnp.tanh(x_wa).astype(BF16), wup_ref[...])
        a = jax.nn.sigmoid(a0_ref[...] + _dot(x_wa.astype(BF16), aup_ref[...]))
        g = _dot(jax.nn.sigmoid(x_g).astype(BF16), gup_ref[...])
        softplus = jnp.maximum(-z, 0.0) + jnp.log1p(jnp.exp(-jnp.abs(z)))
        ell = -jnp.exp(-softplus - 0.5)

        if has_vres:
            vm = vm_ref[bi]
            vml = vm + vmu_ref[...] * (_shift_prev(vm, cvm_ref, bi) - vm)
            vgate = jax.nn.sigmoid(v0_ref[...] + _dot(vml.astype(BF16), vup_ref[...]))
            v = v + (vf_ref[bi] - v) * vgate
        else:
            vout_ref[bi] = v

        kk = k * kk_ref[...]
        kk = kk / jnp.maximum(jnp.sqrt(_head_sums(kk * kk, bd, 1)), 1e-12)
        k_eff = k * (1.0 + (a - 1.0) * ka_ref[...])
        b = kk * a
        cum = _dot_split_rhs(tri, ell, 3)
        cum_last = cum[C - 1:, :]
        g_inv = jnp.exp(-cum)
        g_tail = jnp.exp(cum_last - cum)
        kap_t = kk * jnp.exp(cum - ell)
        r_t = r * jnp.exp(cum)
        b_t = b * g_inv
        k_t = k_eff * g_inv
        b_g = b * g_tail
        k_g = k_eff * g_tail
        g_all = jnp.exp(cum_last)
        pre.append((r, k_eff, v, g))
        for p in range(npair):
            sl = slice(p * LANES, (p + 1) * LANES)
            inst.append(dict(bi=bi, p=p, sl=sl, kap=stack(kap_t[:, sl]), r=stack(r_t[:, sl]),
                             b=stack(b_t[:, sl]), k=stack(k_t[:, sl]), v=stack(v[:, sl]),
                             bg=stack(b_g[:, sl]), kg=stack(k_g[:, sl]), gall=g_all[:, sl]))

    incl2 = jnp.concatenate([incl, incl], axis=1)
    for d in inst:
        aa = _dot_nt(jnp.concatenate([d["kap"], d["r"]], axis=0),
                     jnp.concatenate([d["b"], d["k"]], axis=0))
        n = jnp.where(strict, -aa[:P2, :P2], 0.0)
        d["n"] = n.astype(BF16)
        d["t"] = eye + n
        d["a_ak"] = jnp.where(strict, aa[:P2, P2:], 0.0).astype(BF16)
        d["a_r"] = jnp.where(incl2, aa[P2:, :], 0.0).astype(BF16)
    for d in inst:
        d["akv"] = _dot(d["a_ak"], d["v"]).astype(BF16)
    for d in inst:
        d["pw"] = _dot(d["n"], d["n"]).astype(BF16)
    for _ in range(4):
        for d in inst:
            pt = _dot(d["pw"], jnp.concatenate([d["pw"], d["t"].astype(BF16)], axis=1))
            d["pw"] = pt[:, :P2].astype(BF16)
            d["t"] = d["t"] + pt[:, P2:]
    for d in inst:
        d["t"] = d["t"] + _dot(d["pw"], d["t"].astype(BF16))
    for d in inst:
        ty = _dot(d["t"].astype(BF16), jnp.concatenate([d["kap"], d["akv"]], axis=1))
        d["w_til"], d["u_til"] = ty[:, :LANES].astype(BF16), ty[:, LANES:]
    for d in inst:
        d["state"] = s_ref[d["bi"] * npair + d["p"]]
        xr = _dot_nt(jnp.concatenate([d["w_til"], d["r"]], axis=0), d["state"].astype(BF16))
        sa = -(xr[:P2] + d["u_til"])
        d["sv"] = jnp.concatenate([sa.astype(BF16), d["v"]], axis=0)
        d["rs"] = xr[P2:]
    for d in inst:
        s_ref[d["bi"] * npair + d["p"]] = (
            d["state"] * d["gall"] + _dot_tn(d["sv"], jnp.concatenate([d["bg"], d["kg"]], axis=0)))
    for d in inst:
        o_st = d["rs"] + _dot(d["a_r"], d["sv"])
        hout_ref[d["bi"], :, d["sl"]] = o_st[:C] + o_st[C:]

    inv_n = 1.0 / HEAD_DIM
    for bi in range(nb):
        r, k_eff, v, g = pre[bi]
        out = hout_ref[bi]
        mean = _head_sums(out, bd, 2) * inv_n
        cen = out - mean
        var = _head_sums(cen * cen, bd, 1) * inv_n
        normed = cen * lax.rsqrt(var + RW_LNX_EPS) * lw_ref[...] + lb_ref[...]
        bonus = _head_sums(r * k_eff * rk_ref[...], bd, 2) * v
        o_ref[bi] = ((normed + bonus) * g).astype(o_ref.dtype)


def _rwkv(u_rw, p, layer, B, S, vres=None):
    T = u_rw.shape[0]
    C = RW_CHUNK
    nb = RW_BATCH
    nc = S // C
    has_vres = vres is not None
    blk = lambda w: pl.BlockSpec((nb, C, w), lambda b, c: (b, c, 0))
    names = ("mu", "w0", "w_up", "a0", "a_up", "g_up", "k_k", "k_a", "r_k", "lnx_w", "lnx_b")
    in_specs = [blk(RW_COLS)]
    args = [u_rw.reshape(B, S, RW_COLS)]
    if has_vres:
        in_specs += [blk(LORA_PAD), blk(BRANCH)]
        args += [vres["vm"].reshape(B, S, LORA_PAD), vres["v_first"].reshape(B, S, BRANCH)]
    in_specs += [_layer_spec(p[n].shape[1:], layer) for n in names]
    in_specs.append(_const_spec((MXU_TILE, MXU_TILE)))
    args += [p[n] for n in names] + [p["bd"]]
    scratch = [pltpu.VMEM((nb * (BRANCH // LANES), 2 * C, LANES), F32),
               pltpu.VMEM((nb, 8, RW_COLS), F32)]
    if has_vres:
        vnames = ("mu", "v0", "up")
        in_specs += [_layer_spec(vres[n].shape[1:], layer - 1) for n in vnames]
        args += [vres[n] for n in vnames]
        out_shape = jax.ShapeDtypeStruct((B, S, BRANCH), BF16)
        out_specs = blk(BRANCH)
        scratch.append(pltpu.VMEM((nb, 8, LORA_PAD), F32))
    else:
        out_shape = (jax.ShapeDtypeStruct((B, S, BRANCH), BF16),
                     jax.ShapeDtypeStruct((B, S, BRANCH), F32))
        out_specs = (blk(BRANCH), blk(BRANCH))
    scratch.append(pltpu.VMEM((nb, C, BRANCH), F32))
    outs = pl.pallas_call(
        functools.partial(_rwkv_kernel, has_vres, nb),
        out_shape=out_shape,
        grid=(B // nb, nc),
        in_specs=in_specs,
        out_specs=out_specs,
        scratch_shapes=scratch,
        compiler_params=_cparams("parallel", "arbitrary"),
        name="rwkv",
    )(*args)
    if has_vres:
        return outs.reshape(T, BRANCH)
    return outs[0].reshape(T, BRANCH), outs[1].reshape(T, BRANCH)


def _moba_steps(q_ref, k_ref, v_ref, o_ref, kmean_ref, vt_ref):
    S = q_ref.shape[0]
    nb = S // MB_BLOCK
    tq = MB_BLOCK
    n_heads = LANES // HEAD_DIM
    for n in range(nb):
        blk = k_ref[n * MB_BLOCK:(n + 1) * MB_BLOCK, :].astype(F32)
        kmean_ref[n:n + 1, :] = jnp.mean(blk, axis=0, keepdims=True)
    kmean = kmean_ref[...]
    vt_ref[...] = v_ref[...].astype(F32).T.astype(BF16)
    lane = lax.broadcasted_iota(jnp.int32, (1, LANES), 1)
    head_lanes = [(lane // HEAD_DIM) == hh for hh in range(n_heads)]

    blk_id = lax.broadcasted_iota(jnp.int32, (nb, S), 0)
    own = lax.broadcasted_iota(jnp.int32, (nb, S), 1) // MB_BLOCK
    valid = blk_id < own
    q_all = q_ref[...]
    kmean_rep = jnp.concatenate(
        [kmean] + [jnp.broadcast_to(kmean[m:m + 1, :], (nb, LANES)) for m in range(nb - 1)], axis=0)
    sels = []
    for hh in range(n_heads):
        pieces = _split_bf16(jnp.where(head_lanes[hh], kmean_rep, 0.0), 3)
        gates = sum(_dot_nt(piece, q_all) for piece in pieces)
        gate = gates[:nb]
        cnt = jnp.zeros((nb, S), jnp.int32)
        for m in range(nb - 1):
            gm = gates[nb * (m + 1):nb * (m + 2)]
            beats = (own > m) & ((gm > gate) | ((gm == gate) & (m < blk_id)))
            cnt = cnt + beats.astype(jnp.int32)
        sels.append((valid & (cnt < MB_TOPK)).astype(F32))

    ki = lax.broadcasted_iota(jnp.int32, (tq, tq), 0)
    qi = lax.broadcasted_iota(jnp.int32, (tq, tq), 1)
    causal = ki <= qi

    def scores(i):
        q_t = q_ref[i * tq:(i + 1) * tq, :]
        k_all = k_ref[0:(i + 1) * MB_BLOCK, :]
        return [_dot_nt(k_all, jnp.where(head_lanes[hh], q_t, jnp.zeros_like(q_t)))
                for hh in range(n_heads)]

    state = {"s": scores(0)}

    def tile(i):
        s_cur = state["s"]
        if i + 1 < nb:
            state["s"] = scores(i + 1)
        probs = []
        for hh in range(n_heads):
            s = s_cur[hh]
            s_own = jnp.where(causal, s[i * MB_BLOCK:, :], MASK_VALUE)
            m_col = jnp.max(s_own, axis=0, keepdims=True)
            keeps = []
            for j in range(i):
                keep = sels[hh][j:j + 1, i * tq:(i + 1) * tq] > 0.5
                blk_max = jnp.max(s[j * MB_BLOCK:(j + 1) * MB_BLOCK, :], axis=0, keepdims=True)
                m_col = jnp.maximum(m_col, jnp.where(keep, blk_max, MASK_VALUE))
                keeps.append(keep)
            parts = [jnp.exp2(s[j * MB_BLOCK:(j + 1) * MB_BLOCK, :] - jnp.where(keeps[j], m_col, -MASK_VALUE))
                     for j in range(i)]
            parts.append(jnp.exp2(s_own - m_col))
            p = parts[0] if len(parts) == 1 else jnp.concatenate(parts, axis=0)
            probs.append((p.astype(BF16), jnp.sum(p, axis=0, keepdims=True)))
        vt = vt_ref[:, 0:(i + 1) * MB_BLOCK]
        ot = [_dot(vt, p) / denom for p, denom in probs]
        o_t = jnp.concatenate([ot[0][:HEAD_DIM], ot[1][HEAD_DIM:]], axis=0).T
        o_ref[i * tq:(i + 1) * tq, :] = o_t.astype(o_ref.dtype)

    return [functools.partial(tile, i) for i in range(nb)]


def _swa_steps(q_ref, k_ref, v_ref, sink_ref, o_ref, vt_ref):
    S = q_ref.shape[0]
    W = SW_WINDOW
    n_heads = LANES // HEAD_DIM
    vt_ref[...] = v_ref[...].astype(F32).T.astype(BF16)
    lane = lax.broadcasted_iota(jnp.int32, (1, LANES), 1)
    dist = (lax.broadcasted_iota(jnp.int32, (2 * W, W), 1) + W
            - lax.broadcasted_iota(jnp.int32, (2 * W, W), 0))
    band = (dist >= 0) & (dist < W)
    causal = (lax.broadcasted_iota(jnp.int32, (W, W), 0)
              <= lax.broadcasted_iota(jnp.int32, (W, W), 1))
    head_lanes = [(lane // HEAD_DIM) == hh for hh in range(n_heads)]
    sinks = [sink_ref[0, hh:hh + 1, 0:1] * LOG2E for hh in range(n_heads)]
    group = 4

    def scores(g0):
        work = []
        for n in range(g0, g0 + group):
            k0 = max(n - 1, 0) * W
            k1 = (n + 1) * W
            mask = causal if n == 0 else band
            q_t = q_ref[n * W:(n + 1) * W, :]
            k_w = k_ref[k0:k1, :]
            for hh in range(n_heads):
                q_h = jnp.where(head_lanes[hh], q_t, jnp.zeros_like(q_t))
                work.append(dict(n=n, hh=hh, k0=k0, k1=k1, mask=mask, s=_dot_nt(k_w, q_h)))
        return work

    state = {"work": scores(0)}

    def run_group(g0):
        work = state["work"]
        if g0 + group < S // W:
            state["work"] = scores(g0 + group)
        for d in work:
            s = jnp.where(d["mask"], d["s"], MASK_VALUE)
            sink = sinks[d["hh"]]
            m_col = jnp.maximum(jnp.max(s, axis=0, keepdims=True), sink)
            p = jnp.exp2(s - m_col)
            d["denom"] = jnp.sum(p, axis=0, keepdims=True) + jnp.exp2(sink - m_col)
            d["p"] = p.astype(BF16)
        for d in work:
            d["ot"] = _dot(vt_ref[:, d["k0"]:d["k1"]], d["p"]) / d["denom"]
        o_t = jnp.concatenate(
            [jnp.concatenate([a["ot"][:HEAD_DIM], b["ot"][HEAD_DIM:]], axis=0)
             for a, b in zip(work[0::2], work[1::2])], axis=1)
        o_ref[g0 * W:(g0 + group) * W, :] = o_t.T.astype(o_ref.dtype)

    return [functools.partial(run_group, g0) for g0 in range(0, S // W, group)]


def _attn_kernel(mq_ref, mk_ref, mv_ref, sq_ref, sk_ref, sv_ref, sink_ref, omb_ref, osw_ref,
                 kmean_ref, mvt_ref, svt_ref):
    moba = _moba_steps(mq_ref, mk_ref, mv_ref, omb_ref, kmean_ref, mvt_ref)
    swa = _swa_steps(sq_ref, sk_ref, sv_ref, sink_ref, osw_ref, svt_ref)
    per = len(moba) // len(swa)
    for g, swa_group in enumerate(swa):
        for moba_tile in moba[g * per:(g + 1) * per]:
            moba_tile()
        swa_group()


def _attn(mq, mk, mv, sq, sk, sv, sinks, B, S, layer):
    T = mq.shape[0]
    npair = BRANCH // LANES
    spec = pl.BlockSpec((S, LANES), lambda b, p: (b, p))
    kv_spec = pl.BlockSpec((S, LANES), lambda b, p: (b, 0))
    return pl.pallas_call(
        _attn_kernel,
        out_shape=(jax.ShapeDtypeStruct((T, BRANCH), BF16),) * 2,
        grid=(B, npair),
        in_specs=[spec, spec, spec, spec, kv_spec, kv_spec,
                  pl.BlockSpec((None, 1, 2, LANES), lambda b, p: (layer, p, 0, 0))],
        out_specs=(spec, spec),
        scratch_shapes=[pltpu.VMEM((S // MB_BLOCK, LANES), F32), pltpu.VMEM((LANES, S), BF16),
                        pltpu.VMEM((LANES, S), BF16)],
        compiler_params=_cparams("parallel", "parallel"),
        name="attn",
    )(mq, mk, mv, sq, sk, sv, sinks)


def _mix_out_kernel(x_ref, orw_ref, omb_ref, osw_ref, gate_ref, wb_ref, wo_ref, g_ref, out_ref):
    tm = x_ref.shape[0]
    subs = [slice(r, r + MIX_OUT_SUB) for r in range(0, tm, MIX_OUT_SUB)]

    def gated(rows):
        y = None
        for n, o_ref in enumerate((orw_ref, omb_ref, osw_ref)):
            t = gate_ref[rows, n * D_MODEL:(n + 1) * D_MODEL] * _dot(o_ref[rows, :], wb_ref[n])
            y = t if y is None else y + t
        return y.astype(BF16)

    y_next = gated(subs[0])
    for i, rows in enumerate(subs):
        y = y_next
        if i + 1 < len(subs):
            y_next = gated(subs[i + 1])
        out_ref[rows, :] = x_ref[rows, :] + _rms(_dot(y, wo_ref[...]), g_ref[...])


def _mix_out(x2, o_rw, o_mb, o_sw, gates, w_branch, w_out, g, layer):
    T = x2.shape[0]
    tm = 1024
    row = lambda i: (i, 0)
    br = pl.BlockSpec((tm, BRANCH), row)
    return pl.pallas_call(
        _mix_out_kernel,
        out_shape=jax.ShapeDtypeStruct((T, D_MODEL), F32),
        grid=(T // tm,),
        in_specs=[pl.BlockSpec((tm, D_MODEL), row), br, br, br,
                  pl.BlockSpec((tm, 3 * D_MODEL), row),
                  _layer_spec(w_branch.shape[1:], layer), _layer_spec(w_out.shape[1:], layer),
                  _layer_spec((1, D_MODEL), layer)],
        out_specs=pl.BlockSpec((tm, D_MODEL), row),
        compiler_params=_cparams("parallel"),
        name="mix_out",
    )(x2, o_rw, o_mb, o_sw, gates, w_branch, w_out, g)


def _mem_kv_kernel(m_ref, g_ref, wk_ref, wv_ref, k_ref, vt_ref):
    m = _rms(m_ref[...], g_ref[...]).astype(BF16)
    k_ref[...] = _dot(m, wk_ref[...]).astype(k_ref.dtype)
    vt_ref[...] = _dot(m, wv_ref[...]).T.astype(vt_ref.dtype)


def _mem_kv(mem2, g, wk, wv, M, layer):
    R = mem2.shape[0]
    row = lambda i: (i, 0)
    return pl.pallas_call(
        _mem_kv_kernel,
        out_shape=(jax.ShapeDtypeStruct((R, XA_WIDTH), BF16),
                   jax.ShapeDtypeStruct((R // M, XA_WIDTH, M), BF16)),
        grid=(R // M,),
        in_specs=[pl.BlockSpec((M, D_MODEL), row), _layer_spec((1, D_MODEL), layer),
                  _layer_spec(wk.shape[1:], layer), _layer_spec(wv.shape[1:], layer)],
        out_specs=(pl.BlockSpec((M, XA_WIDTH), row),
                   pl.BlockSpec((None, XA_WIDTH, M), lambda i: (i, 0, 0))),
        compiler_params=_cparams("parallel"),
        name="mem_kv",
    )(mem2, g, wk, wv)


def _xattn_kernel(x_ref, k_ref, vt_ref, gpre_ref, wq_ref, wo_ref, gpost_ref, out_ref):
    tm = x_ref.shape[0]
    groups = [slice(r, r + tm // 2) for r in range(0, tm, tm // 2)]
    heads = [slice(hd * XA_HEAD_DIM, (hd + 1) * XA_HEAD_DIM) for hd in range(XA_HEADS)]
    qs = []
    for rows in groups:
        h = _rms(x_ref[rows, :], gpre_ref[...]).astype(BF16)
        qs.append((_dot(h, wq_ref[...]) * (LOG2E * XA_HEAD_DIM ** -0.5)).astype(BF16))
    scores = [[_dot_nt(k_ref[:, sl], q[:, sl]) for sl in heads] for q in qs]
    outs = []
    for grp in scores:
        probs = []
        for s in grp:
            p = jnp.exp2(s - jnp.max(s, axis=0, keepdims=True))
            probs.append((p.astype(BF16), jnp.sum(p, axis=0, keepdims=True)))
        o_t = jnp.concatenate([_dot(vt_ref[sl, :], p) / denom
                               for sl, (p, denom) in zip(heads, probs)], axis=0)
        outs.append(o_t.T.astype(BF16))
    ys = [_dot(o, wo_ref[...]) for o in outs]
    for rows, y in zip(groups, ys):
        out_ref[rows, :] = x_ref[rows, :] + _rms(y, gpost_ref[...])


def _xattn(x2, k_mem, vt_mem, g_pre, wq, wo, g_post, B, S, M, layer):
    T = x2.shape[0]
    tm = 512
    nt = S // tm
    row = lambda b, i: (b * nt + i, 0)
    return pl.pallas_call(
        _xattn_kernel,
        out_shape=jax.ShapeDtypeStruct((T, D_MODEL), F32),
        grid=(B, nt),
        in_specs=[pl.BlockSpec((tm, D_MODEL), row),
                  pl.BlockSpec((M, XA_WIDTH), lambda b, i: (b, 0)),
                  pl.BlockSpec((None, XA_WIDTH, M), lambda b, i: (b, 0, 0)),
                  _layer_spec((1, D_MODEL), layer), _layer_spec(wq.shape[1:], layer),
                  _layer_spec(wo.shape[1:], layer), _layer_spec((1, D_MODEL), layer)],
        out_specs=pl.BlockSpec((tm, D_MODEL), row),
        compiler_params=_cparams("parallel", "parallel"),
        name="xattn",
    )(x2, k_mem, vt_mem, g_pre, wq, wo, g_post)


def _mlp_kernel(x_ref, gpre_ref, wup_ref, wdn_ref, gpost_ref, out_ref, h_ref, acc_ref):
    j = pl.program_id(1)

    @pl.when(j == 0)
    def _():
        h_ref[...] = _rms(x_ref[...], gpre_ref[...]).astype(BF16)
        acc_ref[...] = jnp.zeros_like(acc_ref)

    h = h_ref[...]
    n_sub = wup_ref.shape[1] // MLP_SUB

    def up(c):
        return _dot(h, wup_ref[:, c * MLP_SUB:(c + 1) * MLP_SUB])

    a_next = up(0)
    acc = None
    for c in range(n_sub):
        a = jnp.maximum(a_next, 0.0)
        if c + 1 < n_sub:
            a_next = up(c + 1)
        t = _dot((a * a).astype(BF16), wdn_ref[c * MLP_SUB:(c + 1) * MLP_SUB, :])
        acc = t if acc is None else acc + t
    acc_ref[...] += acc

    @pl.when(j == pl.num_programs(1) - 1)
    def _():
        out_ref[...] = x_ref[...] + _rms(acc_ref[...], gpost_ref[...])


def _mlp(x2, g_pre, w_up, w_down, g_post, layer):
    T = x2.shape[0]
    tm = MLP_ROWS
    tf = MLP_COLS
    row = lambda i, j: (i, 0)
    return pl.pallas_call(
        _mlp_kernel,
        out_shape=jax.ShapeDtypeStruct((T, D_MODEL), F32),
        grid=(T // tm, D_FF // tf),
        in_specs=[pl.BlockSpec((tm, D_MODEL), row), _layer_spec((1, D_MODEL), layer),
                  pl.BlockSpec((None, D_MODEL, tf), lambda i, j: (layer, 0, j)),
                  pl.BlockSpec((None, tf, D_MODEL), lambda i, j: (layer, j, 0)),
                  _layer_spec((1, D_MODEL), layer)],
        out_specs=pl.BlockSpec((tm, D_MODEL), row),
        scratch_shapes=[pltpu.VMEM((tm, D_MODEL), BF16), pltpu.VMEM((tm, D_MODEL), F32)],
        compiler_params=_cparams("parallel", "arbitrary"),
        name="mlp",
    )(x2, g_pre, w_up, w_down, g_post)


def kernel(x, mem, positions, norm_mix_pre, norm_mix_post, norm_xattn_pre, norm_xattn_post, norm_mem, norm_mlp_pre, norm_mlp_post, w_in, rw_mu, rw_w0, rw_w_up, rw_a0, rw_a_up, rw_g_up, rw_k_k, rw_k_a, rw_r_k, rw_lnx_w, rw_lnx_b, rw_vres_down, rw_vres_mu, rw_v0, rw_vres_up, sw_sinks, w_branch, w_out, w_xq, w_xk, w_xv, w_xo, w_up, w_down):
    B, S, D = x.shape
    M = mem.shape[1]
    depth = w_in.shape[0]
    T = B * S
    x2 = x.reshape(T, D)
    mem2 = mem.reshape(B * M, D)
    cos_t, sin_t = _rope_tables(positions)

    rows = lambda t: t.reshape(t.shape[0], 1, -1)
    bf = lambda t: t.astype(BF16)
    lora_in = rw_w_up.shape[1]
    per_kv = N_HEADS // SW_KV_HEADS
    head_of_col = jnp.arange(MXU_TILE) // HEAD_DIM
    rw_p = dict(
        mu=rows(rw_mu), w0=rows(rw_w0), a0=rows(rw_a0), k_k=rows(rw_k_k), k_a=rows(rw_k_a),
        r_k=rows(rw_r_k), lnx_w=rows(rw_lnx_w), lnx_b=rows(rw_lnx_b),
        w_up=bf(jnp.pad(rw_w_up, ((0, 0), (0, LORA_PAD - lora_in), (0, 0)))),
        a_up=bf(jnp.pad(rw_a_up, ((0, 0), (lora_in, LORA_PAD - lora_in - rw_a_up.shape[1]), (0, 0)))),
        g_up=bf(rw_g_up),
        bd=(head_of_col[:, None] == head_of_col[None, :]).astype(BF16))
    mv_pad = LORA_PAD - rw_vres_down.shape[2]
    w_vm = bf(jnp.pad(rw_vres_down, ((0, 0), (0, 0), (0, mv_pad))))
    vres_p = dict(mu=rows(jnp.pad(rw_vres_mu, ((0, 0), (0, mv_pad)))), v0=rows(rw_v0),
                  up=bf(jnp.pad(rw_vres_up, ((0, 0), (0, mv_pad), (0, 0)))))
    sinks = jnp.broadcast_to(
        sw_sinks.reshape(depth, SW_KV_HEADS, per_kv).transpose(0, 2, 1)[..., None].astype(F32),
        (depth, per_kv, SW_KV_HEADS, LANES))
    wb_sw = w_branch[:, 2].reshape(depth, SW_KV_HEADS, per_kv, HEAD_DIM, D)
    wb_sw = wb_sw.transpose(0, 2, 1, 3, 4).reshape(depth, 1, BRANCH, D)
    wb = bf(jnp.concatenate([w_branch[:, :2], wb_sw], axis=1))
    w_in_b, w_out_b, w_up_b, w_down_b = bf(w_in), bf(w_out), bf(w_up), bf(w_down)
    w_xq_b, w_xk_b, w_xv_b, w_xo_b = bf(w_xq), bf(w_xk), bf(w_xv), bf(w_xo)
    g_mix_pre, g_mix_post = rows(norm_mix_pre), rows(norm_mix_post)
    g_xa_pre, g_xa_post, g_mem = rows(norm_xattn_pre), rows(norm_xattn_post), rows(norm_mem)
    g_mlp_pre, g_mlp_post = rows(norm_mlp_pre), rows(norm_mlp_post)

    v_first = None
    for l in range(depth):
        outs = _mix_in(x2, g_mix_pre, w_in_b, w_vm if l > 0 else None, cos_t, sin_t, l)
        if l > 0:
            u_rw, vm, mq, mk, mv, sq, sk, sv, gates = outs
            o_rw = _rwkv(u_rw, rw_p, l, B, S, dict(vres_p, vm=vm, v_first=v_first))
        else:
            u_rw, mq, mk, mv, sq, sk, sv, gates = outs
            o_rw, v_first = _rwkv(u_rw, rw_p, l, B, S)
        o_mb, o_sw = _attn(mq, mk, mv, sq, sk, sv, sinks, B, S, l)
        x2 = _mix_out(x2, o_rw, o_mb, o_sw, gates, wb, w_out_b, g_mix_post, l)
        k_mem, vt_mem = _mem_kv(mem2, g_mem, w_xk_b, w_xv_b, M, l)
        x2 = _xattn(x2, k_mem, vt_mem, g_xa_pre, w_xq_b, w_xo_b, g_xa_post, B, S, M, l)
        x2 = _mlp(x2, g_mlp_pre, w_up_b, w_down_b, g_mlp_post, l)
    return x2.reshape(B, S, D)
```

```python
import functools

import jax
import jax.numpy as jnp
from jax import lax
from jax.experimental import pallas as pl
from jax.experimental.pallas import tpu as pltpu

F32 = jnp.float32
BF16 = jnp.bfloat16

D_MODEL = 1024
HEAD_DIM = 64
BRANCH = 512
N_HEADS = BRANCH // HEAD_DIM
RW_COLS = 3 * BRANCH + 64 + 64 + 128
LORA_PAD = 128
MB_BLOCK = 256
MB_TOPK = 3
SW_WINDOW = 128
SW_KV_HEADS = 2
XA_HEADS = 4
XA_HEAD_DIM = 128
XA_WIDTH = XA_HEADS * XA_HEAD_DIM
D_FF = 4 * D_MODEL
ROPE_THETA = 10000.0
NORM_EPS = 1e-6
RW_LNX_EPS = 1e-5 * HEAD_DIM
MASK_VALUE = -1e30
LOG2E = 1.4426950408889634

LANES = 128
MXU_TILE = 256
BF16_SUBLANES = 16
RW_CHUNK = 64
XATTN_SUB = 256
MIX_OUT_SUB = 512
MLP_ROWS = 1024
MLP_COLS = 2048
MLP_SUB = 512
RW_BATCH = 4
VMEM_LIMIT = 56 * 1024 * 1024


def _cparams(*sem):
    return pltpu.CompilerParams(dimension_semantics=sem, vmem_limit_bytes=VMEM_LIMIT)


def _dot(a, b, precision=None):
    return jnp.dot(a, b, preferred_element_type=F32, precision=precision)


def _dot_nt(a, b, precision=None):
    return lax.dot_general(a, b, (((1,), (1,)), ((), ())),
                           preferred_element_type=F32, precision=precision)


def _dot_tn(a, b, precision=None):
    return lax.dot_general(a, b, (((0,), (0,)), ((), ())),
                           preferred_element_type=F32, precision=precision)


def _rms(xf, g):
    ms = jnp.mean(xf * xf, axis=-1, keepdims=True)
    return xf * lax.rsqrt(ms + NORM_EPS) * g


def _const_spec(shape):
    nd = len(shape)
    return pl.BlockSpec(shape, lambda *_: (0,) * nd, pipeline_mode=pl.Buffered(1))


def _layer_spec(shape, layer):
    nd = len(shape)
    return pl.BlockSpec((None,) + tuple(shape), lambda *_: (layer,) + (0,) * nd,
                        pipeline_mode=pl.Buffered(1))


def _rope_kernel(pos_ref, invf_ref, cos_ref, sin_ref):
    ang = invf_ref[...] * pos_ref[...].astype(F32)
    c = jnp.cos(ang)
    s = jnp.sin(ang)
    reps = LANES // HEAD_DIM
    cos_ref[...] = jnp.concatenate([c, c] * reps, axis=0).T
    sin_ref[...] = jnp.concatenate([-s, s] * reps, axis=0).T


def _rope_tables(positions):
    T = positions.size
    tm = 2048
    inv_freq = 1.0 / (ROPE_THETA ** (jnp.arange(0, HEAD_DIM, 2, dtype=F32) / HEAD_DIM))
    return pl.pallas_call(
        _rope_kernel,
        out_shape=(jax.ShapeDtypeStruct((T, LANES), F32),) * 2,
        grid=(T // tm,),
        in_specs=[pl.BlockSpec((1, tm), lambda i: (0, i)), _const_spec((HEAD_DIM // 2, 1))],
        out_specs=(pl.BlockSpec((tm, LANES), lambda i: (i, 0)),) * 2,
        compiler_params=_cparams("parallel"),
        name="rope_tables",
    )(positions.reshape(1, T), inv_freq[:, None])


def _rope_tile(t, cos, sin, lane_lt_half):
    fwd = pltpu.roll(t, LANES - HEAD_DIM // 2, axis=1)
    bwd = pltpu.roll(t, HEAD_DIM // 2, axis=1)
    partner = jnp.where(lane_lt_half, fwd, bwd)
    return t * cos + partner * sin


def _mix_in_kernel(has_vm, x_ref, g_ref, w_ref, *rest):
    if has_vm:
        (wvm_ref, cos_ref, sin_ref,
         urw_ref, vm_ref, mq_ref, mk_ref, mv_ref, sq_ref, sk_ref, sv_ref, gate_ref) = rest
    else:
        cos_ref, sin_ref, urw_ref, mq_ref, mk_ref, mv_ref, sq_ref, sk_ref, sv_ref, gate_ref = rest
    h = _rms(x_ref[...], g_ref[...]).astype(BF16)
    cos = cos_ref[...]
    sin = sin_ref[...]
    lane = lax.broadcasted_iota(jnp.int32, cos.shape, 1)
    lt_half = (lane % HEAD_DIM) < (HEAD_DIM // 2)
    first_head = lane < HEAD_DIM
    q_scale = LOG2E * HEAD_DIM ** -0.5

    col = [0]

    def seg(width):
        a = col[0]
        col[0] += width
        return _dot(h, w_ref[:, a:a + width])

    def rope_chunks(width, scale):
        acc = seg(width)
        return [_rope_tile(acc[:, c * LANES:(c + 1) * LANES], cos, sin, lt_half) * scale
                for c in range(width // LANES)]

    def store_chunks(ref, chunks):
        for c, t in enumerate(chunks):
            ref[:, c * LANES:(c + 1) * LANES] = t.astype(ref.dtype)

    urw_ref[...] = seg(RW_COLS).astype(urw_ref.dtype)
    if has_vm:
        vm_ref[...] = _dot(h, wvm_ref[...])
    store_chunks(mq_ref, rope_chunks(BRANCH, q_scale))
    store_chunks(mk_ref, rope_chunks(BRANCH, 1.0))
    mv_ref[...] = seg(BRANCH).astype(mv_ref.dtype)
    nat = rope_chunks(BRANCH, q_scale)
    half_tiles = len(nat) // 2
    paired = []
    for p in range(len(nat)):
        a, b = nat[p // 2], nat[p // 2 + half_tiles]
        if p % 2 == 0:
            paired.append(jnp.where(first_head, a, pltpu.roll(b, HEAD_DIM, axis=1)))
        else:
            paired.append(jnp.where(first_head, pltpu.roll(a, HEAD_DIM, axis=1), b))
    store_chunks(sq_ref, paired)
    store_chunks(sk_ref, rope_chunks(SW_KV_HEADS * HEAD_DIM, 1.0))
    sv_ref[...] = seg(SW_KV_HEADS * HEAD_DIM).astype(sv_ref.dtype)
    for n in range(3):
        gate_ref[:, n * D_MODEL:(n + 1) * D_MODEL] = jax.nn.sigmoid(seg(D_MODEL)).astype(gate_ref.dtype)


def _mix_in(x2, g, w_in, w_vm, cos_t, sin_t, layer):
    T = x2.shape[0]
    tm = 512
    has_vm = w_vm is not None
    kv_w = SW_KV_HEADS * HEAD_DIM
    widths = [(RW_COLS, BF16)]
    if has_vm:
        widths.append((LORA_PAD, F32))
    widths += [(BRANCH, BF16), (BRANCH, BF16), (BRANCH, BF16), (BRANCH, BF16),
               (kv_w, BF16), (kv_w, BF16), (3 * D_MODEL, BF16)]
    assert sum(w for w, _ in widths) - (LORA_PAD if has_vm else 0) == w_in.shape[2]
    row = lambda i: (i, 0)
    in_specs = [pl.BlockSpec((tm, D_MODEL), row), _layer_spec((1, D_MODEL), layer),
                _layer_spec(w_in.shape[1:], layer)]
    args = [x2, g, w_in]
    if has_vm:
        in_specs.append(_layer_spec(w_vm.shape[1:], layer - 1))
        args.append(w_vm)
    in_specs += [pl.BlockSpec((tm, LANES), row), pl.BlockSpec((tm, LANES), row)]
    args += [cos_t, sin_t]
    return pl.pallas_call(
        functools.partial(_mix_in_kernel, has_vm),
        out_shape=tuple(jax.ShapeDtypeStruct((T, w), dt) for w, dt in widths),
        grid=(T // tm,),
        in_specs=in_specs,
        out_specs=tuple(pl.BlockSpec((tm, w), row) for w, _ in widths),
        compiler_params=_cparams("parallel"),
        name="mix_in",
    )(*args)


def _shift_prev(cur, carry_ref, bi):
    rolled = pltpu.roll(cur, 1, axis=0)
    row = lax.broadcasted_iota(jnp.int32, cur.shape, 0)
    prev = jnp.where(row == 0, carry_ref[bi, 0:1, :], rolled)
    carry_ref[bi, 0:1, :] = cur[cur.shape[0] - 1:, :]
    return prev


def _split_bf16(x, terms):
    pieces = []
    for _ in range(terms):
        hi = x.astype(BF16)
        pieces.append(hi)
        x = x - hi.astype(F32)
    return pieces


def _head_sums(x, bd, terms):
    w = bd.shape[0]
    halves = []
    for c in range(x.shape[1] // w):
        pieces = _split_bf16(x[:, c * w:(c + 1) * w], terms)
        halves.append(sum(_dot(piece, bd) for piece in pieces))
    return jnp.concatenate(halves, axis=1)


def _dot_split_rhs(a, b, terms):
    return sum(_dot(a, piece) for piece in _split_bf16(b, terms))


def _rwkv_kernel(has_vres, nb, *refs):
    if has_vres:
        (u_ref, vm_ref, vf_ref, mu_ref, w0_ref, wup_ref, a0_ref, aup_ref, gup_ref, kk_ref,
         ka_ref, rk_ref, lw_ref, lb_ref, bd_ref, vmu_ref, v0_ref, vup_ref,
         o_ref, s_ref, cu_ref, cvm_ref, hout_ref) = refs
        vout_ref = None
    else:
        (u_ref, mu_ref, w0_ref, wup_ref, a0_ref, aup_ref, gup_ref, kk_ref,
         ka_ref, rk_ref, lw_ref, lb_ref, bd_ref,
         o_ref, vout_ref, s_ref, cu_ref, hout_ref) = refs
    C = RW_CHUNK
    P2 = 2 * C
    npair = BRANCH // LANES

    @pl.when(pl.program_id(1) == 0)
    def _():
        s_ref[...] = jnp.zeros_like(s_ref)
        cu_ref[...] = jnp.zeros_like(cu_ref)
        if has_vres:
            cvm_ref[...] = jnp.zeros_like(cvm_ref)

    bd = bd_ref[...]
    ti = lax.broadcasted_iota(jnp.int32, (C, C), 0)
    si = lax.broadcasted_iota(jnp.int32, (C, C), 1)
    tri = (si <= ti).astype(BF16)
    lane = lax.broadcasted_iota(jnp.int32, (1, LANES), 1)
    first_head = lane < HEAD_DIM
    pr = lax.broadcasted_iota(jnp.int32, (P2, P2), 0)
    pc = lax.broadcasted_iota(jnp.int32, (P2, P2), 1)
    same_head = (pr // C) == (pc // C)
    strict = same_head & ((pc % C) < (pr % C))
    incl = same_head & ((pc % C) <= (pr % C))
    eye = (pr == pc).astype(F32)

    def stack(t):
        t = t.astype(BF16)
        zero = jnp.zeros_like(t)
        return jnp.concatenate([jnp.where(first_head, t, zero), jnp.where(first_head, zero, t)], axis=0)

    pre = []
    inst = []
    for bi in range(nb):
        u = u_ref[bi].astype(F32)
        ul = u + mu_ref[...] * (_shift_prev(u, cu_ref, bi) - u)
        r = ul[:, 0:BRANCH]
        k = ul[:, BRANCH:2 * BRANCH]
        v = ul[:, 2 * BRANCH:3 * BRANCH]
        x_wa = ul[:, 3 * BRANCH:3 * BRANCH + LANES]
        x_g = ul[:, 3 * BRANCH + LANES:]

        z = w0_ref[...] + _dot(jnp.tanh(x_wa).astype(BF16), wup_ref[...])
        a = jax.nn.sigmoid(a0_ref[...] + _dot(x_wa.astype(BF16), aup_ref[...]))
        g = _dot(jax.nn.sigmoid(x_g).astype(BF16), gup_ref[...])
        softplus = jnp.maximum(-z, 0.0) + jnp.log1p(jnp.exp(-jnp.abs(z)))
        ell = -jnp.exp(-softplus - 0.5)

        if has_vres:
            vm = vm_ref[bi]
            vml = vm + vmu_ref[...] * (_shift_prev(vm, cvm_ref, bi) - vm)
            vgate = jax.nn.sigmoid(v0_ref[...] + _dot(vml.astype(BF16), vup_ref[...]))
            v = v + (vf_ref[bi] - v) * vgate
        else:
            vout_ref[bi] = v

        kk = k * kk_ref[...]
        kk = kk / jnp.maximum(jnp.sqrt(_head_sums(kk * kk, bd, 1)), 1e-12)
        k_eff = k * (1.0 + (a - 1.0) * ka_ref[...])
        b = kk * a
        cum = _dot_split_rhs(tri, ell, 3)
        cum_last = cum[C - 1:, :]
        g_inv = jnp.exp(-cum)
        g_tail = jnp.exp(cum_last - cum)
        kap_t = kk * jnp.exp(cum - ell)
        r_t = r * jnp.exp(cum)
        b_t = b * g_inv
        k_t = k_eff * g_inv
        b_g = b * g_tail
        k_g = k_eff * g_tail
        g_all = jnp.exp(cum_last)
        pre.append((r, k_eff, v, g))
        for p in range(npair):
            sl = slice(p * LANES, (p + 1) * LANES)
            inst.append(dict(bi=bi, p=p, sl=sl, kap=stack(kap_t[:, sl]), r=stack(r_t[:, sl]),
                             b=stack(b_t[:, sl]), k=stack(k_t[:, sl]), v=stack(v[:, sl]),
                             bg=stack(b_g[:, sl]), kg=stack(k_g[:, sl]), gall=g_all[:, sl]))

    incl2 = jnp.concatenate([incl, incl], axis=1)
    for d in inst:
        aa = _dot_nt(jnp.concatenate([d["kap"], d["r"]], axis=0),
                     jnp.concatenate([d["b"], d["k"]], axis=0))
        n = jnp.where(strict, -aa[:P2, :P2], 0.0)
        d["n"] = n.astype(BF16)
        d["t"] = eye + n
        d["a_ak"] = jnp.where(strict, aa[:P2, P2:], 0.0).astype(BF16)
        d["a_r"] = jnp.where(incl2, aa[P2:, :], 0.0).astype(BF16)
    for d in inst:
        d["akv"] = _dot(d["a_ak"], d["v"]).astype(BF16)
    for d in inst:
        d["pw"] = _dot(d["n"], d["n"]).astype(BF16)
    for _ in range(4):
        for d in inst:
            pt = _dot(d["pw"], jnp.concatenate([d["pw"], d["t"].astype(BF16)], axis=1))
            d["pw"] = pt[:, :P2].astype(BF16)
            d["t"] = d["t"] + pt[:, P2:]
    for d in inst:
        d["t"] = d["t"] + _dot(d["pw"], d["t"].astype(BF16))
    for d in inst:
        ty = _dot(d["t"].astype(BF16), jnp.concatenate([d["kap"], d["akv"]], axis=1))
        d["w_til"], d["u_til"] = ty[:, :LANES].astype(BF16), ty[:, LANES:]
    for d in inst:
        d["state"] = s_ref[d["bi"] * npair + d["p"]]
        xr = _dot_nt(jnp.concatenate([d["w_til"], d["r"]], axis=0), d["state"].astype(BF16))
        sa = -(xr[:P2] + d["u_til"])
        d["sv"] = jnp.concatenate([sa.astype(BF16), d["v"]], axis=0)
        d["rs"] = xr[P2:]
    for d in inst:
        s_ref[d["bi"] * npair + d["p"]] = (
            d["state"] * d["gall"] + _dot_tn(d["sv"], jnp.concatenate([d["bg"], d["kg"]], axis=0)))
    for d in inst:
        o_st = d["rs"] + _dot(d["a_r"], d["sv"])
        hout_ref[d["bi"], :, d["sl"]] = o_st[:C] + o_st[C:]

    inv_n = 1.0 / HEAD_DIM
    for bi in range(nb):
        r, k_eff, v, g = pre[bi]
        out = hout_ref[bi]
        mean = _head_sums(out, bd, 2) * inv_n
        cen = out - mean
        var = _head_sums(cen * cen, bd, 1) * inv_n
        normed = cen * lax.rsqrt(var + RW_LNX_EPS) * lw_ref[...] + lb_ref[...]
        bonus = _head_sums(r * k_eff * rk_ref[...], bd, 2) * v
        o_ref[bi] = ((normed + bonus) * g).astype(o_ref.dtype)


def _rwkv(u_rw, p, layer, B, S, vres=None):
    T = u_rw.shape[0]
    C = RW_CHUNK
    nb = RW_BATCH
    nc = S // C
    has_vres = vres is not None
    blk = lambda w: pl.BlockSpec((nb, C, w), lambda b, c: (b, c, 0))
    names = ("mu", "w0", "w_up", "a0", "a_up", "g_up", "k_k", "k_a", "r_k", "lnx_w", "lnx_b")
    in_specs = [blk(RW_COLS)]
    args = [u_rw.reshape(B, S, RW_COLS)]
    if has_vres:
        in_specs += [blk(LORA_PAD), blk(BRANCH)]
        args += [vres["vm"].reshape(B, S, LORA_PAD), vres["v_first"].reshape(B, S, BRANCH)]
    in_specs += [_layer_spec(p[n].shape[1:], layer) for n in names]
    in_specs.append(_const_spec((MXU_TILE, MXU_TILE)))
    args += [p[n] for n in names] + [p["bd"]]
    scratch = [pltpu.VMEM((nb * (BRANCH // LANES), 2 * C, LANES), F32),
               pltpu.VMEM((nb, 8, RW_COLS), F32)]
    if has_vres:
        vnames = ("mu", "v0", "up")
        in_specs += [_layer_spec(vres[n].shape[1:], layer - 1) for n in vnames]
        args += [vres[n] for n in vnames]
        out_shape = jax.ShapeDtypeStruct((B, S, BRANCH), BF16)
        out_specs = blk(BRANCH)
        scratch.append(pltpu.VMEM((nb, 8, LORA_PAD), F32))
    else:
        out_shape = (jax.ShapeDtypeStruct((B, S, BRANCH), BF16),
                     jax.ShapeDtypeStruct((B, S, BRANCH), F32))
        out_specs = (blk(BRANCH), blk(BRANCH))
    scratch.append(pltpu.VMEM((nb, C, BRANCH), F32))
    outs = pl.pallas_call(
        functools.partial(_rwkv_kernel, has_vres, nb),
        out_shape=out_shape,
        grid=(B // nb, nc),
        in_specs=in_specs,
        out_specs=out_specs,
        scratch_shapes=scratch,
        compiler_params=_cparams("parallel", "arbitrary"),
        name="rwkv",
    )(*args)
    if has_vres:
        return outs.reshape(T, BRANCH)
    return outs[0].reshape(T, BRANCH), outs[1].reshape(T, BRANCH)


def _moba_steps(q_ref, k_ref, v_ref, o_ref, kmean_ref, vt_ref):
    S = q_ref.shape[0]
    nb = S // MB_BLOCK
    tq = MB_BLOCK
    n_heads = LANES // HEAD_DIM
    for n in range(nb):
        blk = k_ref[n * MB_BLOCK:(n + 1) * MB_BLOCK, :].astype(F32)
        kmean_ref[n:n + 1, :] = jnp.mean(blk, axis=0, keepdims=True)
    kmean = kmean_ref[...]
    vt_ref[0:LANES, :] = v_ref[...].astype(F32).T.astype(BF16)
    vt_ref[LANES:, :] = jnp.ones((vt_ref.shape[0] - LANES, S), BF16)
    lane = lax.broadcasted_iota(jnp.int32, (1, LANES), 1)
    head_lanes = [(lane // HEAD_DIM) == hh for hh in range(n_heads)]

    blk_id = lax.broadcasted_iota(jnp.int32, (nb, S), 0)
    own = lax.broadcasted_iota(jnp.int32, (nb, S), 1) // MB_BLOCK
    valid = blk_id < own
    q_all = q_ref[...]
    kmean_rep = jnp.concatenate(
        [kmean] + [jnp.broadcast_to(kmean[m:m + 1, :], (nb, LANES)) for m in range(nb - 1)], axis=0)
    sels = []
    for hh in range(n_heads):
        pieces = _split_bf16(jnp.where(head_lanes[hh], kmean_rep, 0.0), 3)
        gates = sum(_dot_nt(piece, q_all) for piece in pieces)
        gate = gates[:nb]
        cnt = jnp.zeros((nb, S), jnp.int32)
        for m in range(nb - 1):
            gm = gates[nb * (m + 1):nb * (m + 2)]
            beats = (own > m) & ((gm > gate) | ((gm == gate) & (m < blk_id)))
            cnt = cnt + beats.astype(jnp.int32)
        sels.append((valid & (cnt < MB_TOPK)).astype(F32))

    ki = lax.broadcasted_iota(jnp.int32, (tq, tq), 0)
    qi = lax.broadcasted_iota(jnp.int32, (tq, tq), 1)
    causal = ki <= qi

    def scores(i):
        q_t = q_ref[i * tq:(i + 1) * tq, :]
        k_all = k_ref[0:(i + 1) * MB_BLOCK, :]
        return [_dot_nt(k_all, jnp.where(head_lanes[hh], q_t, jnp.zeros_like(q_t)))
                for hh in range(n_heads)]

    state = {"s": scores(0)}

    def tile(i):
        s_cur = state["s"]
        if i + 1 < nb:
            state["s"] = scores(i + 1)
        probs = []
        for hh in range(n_heads):
            s = s_cur[hh]
            s_own = jnp.where(causal, s[i * MB_BLOCK:, :], MASK_VALUE)
            m_col = jnp.max(s_own, axis=0, keepdims=True)
            keeps = []
            for j in range(i):
                keep = sels[hh][j:j + 1, i * tq:(i + 1) * tq] > 0.5
                blk_max = jnp.max(s[j * MB_BLOCK:(j + 1) * MB_BLOCK, :], axis=0, keepdims=True)
                m_col = jnp.maximum(m_col, jnp.where(keep, blk_max, MASK_VALUE))
                keeps.append(keep)
            parts = [jnp.exp2(s[j * MB_BLOCK:(j + 1) * MB_BLOCK, :] - jnp.where(keeps[j], m_col, -MASK_VALUE))
                     for j in range(i)]
            parts.append(jnp.exp2(s_own - m_col))
            p = parts[0] if len(parts) == 1 else jnp.concatenate(parts, axis=0)
            probs.append(p.astype(BF16))
        vt = vt_ref[:, 0:(i + 1) * MB_BLOCK]
        ot = [_dot(vt, p) for p in probs]
        ot = [o[:LANES] / o[LANES:LANES + 1] for o in ot]
        o_t = jnp.concatenate([ot[0][:HEAD_DIM], ot[1][HEAD_DIM:]], axis=0).T
        o_ref[i * tq:(i + 1) * tq, :] = o_t.astype(o_ref.dtype)

    return [functools.partial(tile, i) for i in range(nb)]


def _swa_steps(q_ref, k_ref, v_ref, sink_ref, o_ref, vt_ref):
    S = q_ref.shape[0]
    W = SW_WINDOW
    n_heads = LANES // HEAD_DIM
    vt_ref[0:LANES, :] = v_ref[...].astype(F32).T.astype(BF16)
    vt_ref[LANES:, :] = jnp.ones((vt_ref.shape[0] - LANES, S), BF16)
    lane = lax.broadcasted_iota(jnp.int32, (1, LANES), 1)
    dist = (lax.broadcasted_iota(jnp.int32, (2 * W, W), 1) + W
            - lax.broadcasted_iota(jnp.int32, (2 * W, W), 0))
    band = (dist >= 0) & (dist < W)
    causal = (lax.broadcasted_iota(jnp.int32, (W, W), 0)
              <= lax.broadcasted_iota(jnp.int32, (W, W), 1))
    head_lanes = [(lane // HEAD_DIM) == hh for hh in range(n_heads)]
    sinks = [sink_ref[0, hh:hh + 1, 0:1] * LOG2E for hh in range(n_heads)]
    group = 4

    def scores(g0):
        work = []
        for n in range(g0, g0 + group):
            k0 = max(n - 1, 0) * W
            k1 = (n + 1) * W
            mask = causal if n == 0 else band
            q_t = q_ref[n * W:(n + 1) * W, :]
            k_w = k_ref[k0:k1, :]
            for hh in range(n_heads):
                q_h = jnp.where(head_lanes[hh], q_t, jnp.zeros_like(q_t))
                work.append(dict(n=n, hh=hh, k0=k0, k1=k1, mask=mask, s=_dot_nt(k_w, q_h)))
        return work

    state = {"work": scores(0)}

    def run_group(g0):
        work = state["work"]
        if g0 + group < S // W:
            state["work"] = scores(g0 + group)
        for d in work:
            s = jnp.where(d["mask"], d["s"], MASK_VALUE)
            sink = sinks[d["hh"]]
            m_col = jnp.maximum(jnp.max(s, axis=0, keepdims=True), sink)
            d["p"] = jnp.exp2(s - m_col).astype(BF16)
            d["sink_term"] = jnp.exp2(sink - m_col)
        for d in work:
            ot = _dot(vt_ref[:, d["k0"]:d["k1"]], d["p"])
            d["ot"] = ot[:LANES] / (ot[LANES:LANES + 1] + d["sink_term"])
        o_t = jnp.concatenate(
            [jnp.concatenate([a["ot"][:HEAD_DIM], b["ot"][HEAD_DIM:]], axis=0)
             for a, b in zip(work[0::2], work[1::2])], axis=1)
        o_ref[g0 * W:(g0 + group) * W, :] = o_t.T.astype(o_ref.dtype)

    return [functools.partial(run_group, g0) for g0 in range(0, S // W, group)]


def _attn_kernel(mq_ref, mk_ref, mv_ref, sq_ref, sk_ref, sv_ref, sink_ref, omb_ref, osw_ref,
                 kmean_ref, mvt_ref, svt_ref):
    moba = _moba_steps(mq_ref, mk_ref, mv_ref, omb_ref, kmean_ref, mvt_ref)
    swa = _swa_steps(sq_ref, sk_ref, sv_ref, sink_ref, osw_ref, svt_ref)
    per = len(moba) // len(swa)
    for g, swa_group in enumerate(swa):
        for moba_tile in moba[g * per:(g + 1) * per]:
            moba_tile()
        swa_group()


def _attn(mq, mk, mv, sq, sk, sv, sinks, B, S, layer):
    T = mq.shape[0]
    npair = BRANCH // LANES
    spec = pl.BlockSpec((S, LANES), lambda b, p: (b, p))
    kv_spec = pl.BlockSpec((S, LANES), lambda b, p: (b, 0))
    return pl.pallas_call(
        _attn_kernel,
        out_shape=(jax.ShapeDtypeStruct((T, BRANCH), BF16),) * 2,
        grid=(B, npair),
        in_specs=[spec, spec, spec, spec, kv_spec, kv_spec,
                  pl.BlockSpec((None, 1, 2, LANES), lambda b, p: (layer, p, 0, 0))],
        out_specs=(spec, spec),
        scratch_shapes=[pltpu.VMEM((S // MB_BLOCK, LANES), F32),
                        pltpu.VMEM((LANES + BF16_SUBLANES, S), BF16),
                        pltpu.VMEM((LANES + BF16_SUBLANES, S), BF16)],
        compiler_params=_cparams("parallel", "parallel"),
        name="attn",
    )(mq, mk, mv, sq, sk, sv, sinks)


def _mix_out_kernel(x_ref, orw_ref, omb_ref, osw_ref, gate_ref, wb_ref, wo_ref, g_ref, out_ref):
    tm = x_ref.shape[0]
    subs = [slice(r, r + MIX_OUT_SUB) for r in range(0, tm, MIX_OUT_SUB)]

    def gated(rows):
        y = None
        for n, o_ref in enumerate((orw_ref, omb_ref, osw_ref)):
            t = gate_ref[rows, n * D_MODEL:(n + 1) * D_MODEL] * _dot(o_ref[rows, :], wb_ref[n])
            y = t if y is None else y + t
        return y.astype(BF16)

    y_next = gated(subs[0])
    for i, rows in enumerate(subs):
        y = y_next
        if i + 1 < len(subs):
            y_next = gated(subs[i + 1])
        out_ref[rows, :] = x_ref[rows, :] + _rms(_dot(y, wo_ref[...]), g_ref[...])


def _mix_out(x2, o_rw, o_mb, o_sw, gates, w_branch, w_out, g, layer):
    T = x2.shape[0]
    tm = 1024
    row = lambda i: (i, 0)
    br = pl.BlockSpec((tm, BRANCH), row)
    return pl.pallas_call(
        _mix_out_kernel,
        out_shape=jax.ShapeDtypeStruct((T, D_MODEL), F32),
        grid=(T // tm,),
        in_specs=[pl.BlockSpec((tm, D_MODEL), row), br, br, br,
                  pl.BlockSpec((tm, 3 * D_MODEL), row),
                  _layer_spec(w_branch.shape[1:], layer), _layer_spec(w_out.shape[1:], layer),
                  _layer_spec((1, D_MODEL), layer)],
        out_specs=pl.BlockSpec((tm, D_MODEL), row),
        compiler_params=_cparams("parallel"),
        name="mix_out",
    )(x2, o_rw, o_mb, o_sw, gates, w_branch, w_out, g)


def _mem_kv_kernel(m_ref, g_ref, wk_ref, wv_ref, k_ref, vt_ref):
    m = _rms(m_ref[...], g_ref[...]).astype(BF16)
    k_ref[...] = _dot(m, wk_ref[...]).astype(k_ref.dtype)
    vt_ref[...] = _dot(m, wv_ref[...]).T.astype(vt_ref.dtype)


def _mem_kv(mem2, g, wk, wv, M, layer):
    R = mem2.shape[0]
    row = lambda i: (i, 0)
    return pl.pallas_call(
        _mem_kv_kernel,
        out_shape=(jax.ShapeDtypeStruct((R, XA_WIDTH), BF16),
                   jax.ShapeDtypeStruct((R // M, XA_WIDTH, M), BF16)),
        grid=(R // M,),
        in_specs=[pl.BlockSpec((M, D_MODEL), row), _layer_spec((1, D_MODEL), layer),
                  _layer_spec(wk.shape[1:], layer), _layer_spec(wv.shape[1:], layer)],
        out_specs=(pl.BlockSpec((M, XA_WIDTH), row),
                   pl.BlockSpec((None, XA_WIDTH, M), lambda i: (i, 0, 0))),
        compiler_params=_cparams("parallel"),
        name="mem_kv",
    )(mem2, g, wk, wv)


def _xattn_kernel(x_ref, k_ref, vt_ref, gpre_ref, wq_ref, wo_ref, gpost_ref, out_ref):
    tm = x_ref.shape[0]
    groups = [slice(r, r + XATTN_SUB) for r in range(0, tm, XATTN_SUB)]
    heads = [slice(hd * XA_HEAD_DIM, (hd + 1) * XA_HEAD_DIM) for hd in range(XA_HEADS)]
    qs = []
    for rows in groups:
        h = _rms(x_ref[rows, :], gpre_ref[...]).astype(BF16)
        qs.append((_dot(h, wq_ref[...]) * (LOG2E * XA_HEAD_DIM ** -0.5)).astype(BF16))
    scores = [[_dot_nt(k_ref[:, sl], q[:, sl]) for sl in heads] for q in qs]
    ones = jnp.ones((BF16_SUBLANES, vt_ref.shape[1]), BF16)
    vts = [jnp.concatenate([vt_ref[sl, :], ones], axis=0) for sl in heads]
    outs = []
    for grp in scores:
        probs = [jnp.exp2(s - jnp.max(s, axis=0, keepdims=True)).astype(BF16) for s in grp]
        pv = [_dot(vt, p) for vt, p in zip(vts, probs)]
        o_t = jnp.concatenate([o[:XA_HEAD_DIM] / o[XA_HEAD_DIM:XA_HEAD_DIM + 1] for o in pv], axis=0)
        outs.append(o_t.T.astype(BF16))
    ys = [_dot(o, wo_ref[...]) for o in outs]
    for rows, y in zip(groups, ys):
        out_ref[rows, :] = x_ref[rows, :] + _rms(y, gpost_ref[...])


def _xattn(x2, k_mem, vt_mem, g_pre, wq, wo, g_post, B, S, M, layer):
    T = x2.shape[0]
    tm = 1024
    nt = S // tm
    row = lambda b, i: (b * nt + i, 0)
    return pl.pallas_call(
        _xattn_kernel,
        out_shape=jax.ShapeDtypeStruct((T, D_MODEL), F32),
        grid=(B, nt),
        in_specs=[pl.BlockSpec((tm, D_MODEL), row),
                  pl.BlockSpec((M, XA_WIDTH), lambda b, i: (b, 0)),
                  pl.BlockSpec((None, XA_WIDTH, M), lambda b, i: (b, 0, 0)),
                  _layer_spec((1, D_MODEL), layer), _layer_spec(wq.shape[1:], layer),
                  _layer_spec(wo.shape[1:], layer), _layer_spec((1, D_MODEL), layer)],
        out_specs=pl.BlockSpec((tm, D_MODEL), row),
        compiler_params=_cparams("parallel", "parallel"),
        name="xattn",
    )(x2, k_mem, vt_mem, g_pre, wq, wo, g_post)


def _mlp_kernel(x_ref, gpre_ref, wup_ref, wdn_ref, gpost_ref, out_ref, h_ref, acc_ref):
    j = pl.program_id(1)

    @pl.when(j == 0)
    def _():
        h_ref[...] = _rms(x_ref[...], gpre_ref[...]).astype(BF16)
        acc_ref[...] = jnp.zeros_like(acc_ref)

    h = h_ref[...]
    n_sub = wup_ref.shape[1] // MLP_SUB

    def up(c):
        return _dot(h, wup_ref[:, c * MLP_SUB:(c + 1) * MLP_SUB])

    a_next = up(0)
    acc = None
    for c in range(n_sub):
        a = jnp.maximum(a_next, 0.0)
        if c + 1 < n_sub:
            a_next = up(c + 1)
        t = _dot((a * a).astype(BF16), wdn_ref[c * MLP_SUB:(c + 1) * MLP_SUB, :])
        acc = t if acc is None else acc + t
    acc_ref[...] += acc

    @pl.when(j == pl.num_programs(1) - 1)
    def _():
        out_ref[...] = x_ref[...] + _rms(acc_ref[...], gpost_ref[...])


def _mlp(x2, g_pre, w_up, w_down, g_post, layer):
    T = x2.shape[0]
    tm = MLP_ROWS
    tf = MLP_COLS
    row = lambda i, j: (i, 0)
    return pl.pallas_call(
        _mlp_kernel,
        out_shape=jax.ShapeDtypeStruct((T, D_MODEL), F32),
        grid=(T // tm, D_FF // tf),
        in_specs=[pl.BlockSpec((tm, D_MODEL), row), _layer_spec((1, D_MODEL), layer),
                  pl.BlockSpec((None, D_MODEL, tf), lambda i, j: (layer, 0, j)),
                  pl.BlockSpec((None, tf, D_MODEL), lambda i, j: (layer, j, 0)),
                  _layer_spec((1, D_MODEL), layer)],
        out_specs=pl.BlockSpec((tm, D_MODEL), row),
        scratch_shapes=[pltpu.VMEM((tm, D_MODEL), BF16), pltpu.VMEM((tm, D_MODEL), F32)],
        compiler_params=_cparams("parallel", "arbitrary"),
        name="mlp",
    )(x2, g_pre, w_up, w_down, g_post)


def kernel(x, mem, positions, norm_mix_pre, norm_mix_post, norm_xattn_pre, norm_xattn_post, norm_mem, norm_mlp_pre, norm_mlp_post, w_in, rw_mu, rw_w0, rw_w_up, rw_a0, rw_a_up, rw_g_up, rw_k_k, rw_k_a, rw_r_k, rw_lnx_w, rw_lnx_b, rw_vres_down, rw_vres_mu, rw_v0, rw_vres_up, sw_sinks, w_branch, w_out, w_xq, w_xk, w_xv, w_xo, w_up, w_down):
    B, S, D = x.shape
    M = mem.shape[1]
    depth = w_in.shape[0]
    T = B * S
    x2 = x.reshape(T, D)
    mem2 = mem.reshape(B * M, D)
    cos_t, sin_t = _rope_tables(positions)

    rows = lambda t: t.reshape(t.shape[0], 1, -1)
    bf = lambda t: t.astype(BF16)
    lora_in = rw_w_up.shape[1]
    per_kv = N_HEADS // SW_KV_HEADS
    head_of_col = jnp.arange(MXU_TILE) // HEAD_DIM
    rw_p = dict(
        mu=rows(rw_mu), w0=rows(rw_w0), a0=rows(rw_a0), k_k=rows(rw_k_k), k_a=rows(rw_k_a),
        r_k=rows(rw_r_k), lnx_w=rows(rw_lnx_w), lnx_b=rows(rw_lnx_b),
        w_up=bf(jnp.pad(rw_w_up, ((0, 0), (0, LORA_PAD - lora_in), (0, 0)))),
        a_up=bf(jnp.pad(rw_a_up, ((0, 0), (lora_in, LORA_PAD - lora_in - rw_a_up.shape[1]), (0, 0)))),
        g_up=bf(rw_g_up),
        bd=(head_of_col[:, None] == head_of_col[None, :]).astype(BF16))
    mv_pad = LORA_PAD - rw_vres_down.shape[2]
    w_vm = bf(jnp.pad(rw_vres_down, ((0, 0), (0, 0), (0, mv_pad))))
    vres_p = dict(mu=rows(jnp.pad(rw_vres_mu, ((0, 0), (0, mv_pad)))), v0=rows(rw_v0),
                  up=bf(jnp.pad(rw_vres_up, ((0, 0), (0, mv_pad), (0, 0)))))
    sinks = jnp.broadcast_to(
        sw_sinks.reshape(depth, SW_KV_HEADS, per_kv).transpose(0, 2, 1)[..., None].astype(F32),
        (depth, per_kv, SW_KV_HEADS, LANES))
    wb_sw = w_branch[:, 2].reshape(depth, SW_KV_HEADS, per_kv, HEAD_DIM, D)
    wb_sw = wb_sw.transpose(0, 2, 1, 3, 4).reshape(depth, 1, BRANCH, D)
    wb = bf(jnp.concatenate([w_branch[:, :2], wb_sw], axis=1))
    w_in_b, w_out_b, w_up_b, w_down_b = bf(w_in), bf(w_out), bf(w_up), bf(w_down)
    w_xq_b, w_xk_b, w_xv_b, w_xo_b = bf(w_xq), bf(w_xk), bf(w_xv), bf(w_xo)
    g_mix_pre, g_mix_post = rows(norm_mix_pre), rows(norm_mix_post)
    g_xa_pre, g_xa_post, g_mem = rows(norm_xattn_pre), rows(norm_xattn_post), rows(norm_mem)
    g_mlp_pre, g_mlp_post = rows(norm_mlp_pre), rows(norm_mlp_post)

    v_first = None
    for l in range(depth):
        outs = _mix_in(x2, g_mix_pre, w_in_b, w_vm if l > 0 else None, cos_t, sin_t, l)
        if l > 0:
            u_rw, vm, mq, mk, mv, sq, sk, sv, gates = outs
            o_rw = _rwkv(u_rw, rw_p, l, B, S, dict(vres_p, vm=vm, v_first=v_first))
        else:
            u_rw, mq, mk, mv, sq, sk, sv, gates = outs
            o_rw, v_first = _rwkv(u_rw, rw_p, l, B, S)
        o_mb, o_sw = _attn(mq, mk, mv, sq, sk, sv, sinks, B, S, l)
        x2 = _mix_out(x2, o_rw, o_mb, o_sw, gates, wb, w_out_b, g_mix_post, l)
        k_mem, vt_mem = _mem_kv(mem2, g_mem, w_xk_b, w_xv_b, M, l)
        x2 = _xattn(x2, k_mem, vt_mem, g_xa_pre, w_xq_b, w_xo_b, g_xa_post, B, S, M, l)
        x2 = _mlp(x2, g_mlp_pre, w_up_b, w_down_b, g_mlp_post, l)
    return x2.reshape(B, S, D)
```

```python
import functools

import jax
import jax.numpy as jnp
from jax import lax
from jax.experimental import pallas as pl
from jax.experimental.pallas import tpu as pltpu

F32 = jnp.float32
BF16 = jnp.bfloat16

D_MODEL = 1024
HEAD_DIM = 64
BRANCH = 512
N_HEADS = BRANCH // HEAD_DIM
RW_COLS = 3 * BRANCH + 64 + 64 + 128
LORA_PAD = 128
MB_BLOCK = 256
MB_TOPK = 3
SW_WINDOW = 128
SW_KV_HEADS = 2
XA_HEADS = 4
XA_HEAD_DIM = 128
XA_WIDTH = XA_HEADS * XA_HEAD_DIM
D_FF = 4 * D_MODEL
ROPE_THETA = 10000.0
NORM_EPS = 1e-6
RW_LNX_EPS = 1e-5 * HEAD_DIM
MASK_VALUE = -1e30
LOG2E = 1.4426950408889634

LANES = 128
MXU_TILE = 256
BF16_SUBLANES = 16
RW_CHUNK = 64
ATTN_TILES = 2
SWA_GROUP = 4
XATTN_SUB = 256
MIX_OUT_SUB = 512
MLP_ROWS = 1024
MLP_COLS = 2048
MLP_SUB = 512
RW_BATCH = 4
VMEM_LIMIT = 56 * 1024 * 1024


def _cparams(*sem):
    return pltpu.CompilerParams(dimension_semantics=sem, vmem_limit_bytes=VMEM_LIMIT)


def _dot(a, b, precision=None):
    return jnp.dot(a, b, preferred_element_type=F32, precision=precision)


def _dot_nt(a, b, precision=None):
    return lax.dot_general(a, b, (((1,), (1,)), ((), ())),
                           preferred_element_type=F32, precision=precision)


def _dot_tn(a, b, precision=None):
    return lax.dot_general(a, b, (((0,), (0,)), ((), ())),
                           preferred_element_type=F32, precision=precision)


def _rms(xf, g):
    ms = jnp.mean(xf * xf, axis=-1, keepdims=True)
    return xf * lax.rsqrt(ms + NORM_EPS) * g


def _const_spec(shape):
    nd = len(shape)
    return pl.BlockSpec(shape, lambda *_: (0,) * nd, pipeline_mode=pl.Buffered(1))


def _layer_spec(shape, layer):
    nd = len(shape)
    return pl.BlockSpec((None,) + tuple(shape), lambda *_: (layer,) + (0,) * nd,
                        pipeline_mode=pl.Buffered(1))


def _rope_kernel(pos_ref, invf_ref, cos_ref, sin_ref):
    ang = invf_ref[...] * pos_ref[...].astype(F32)
    c = jnp.cos(ang)
    s = jnp.sin(ang)
    reps = LANES // HEAD_DIM
    cos_ref[...] = jnp.concatenate([c, c] * reps, axis=0).T
    sin_ref[...] = jnp.concatenate([-s, s] * reps, axis=0).T


def _rope_tables(positions):
    T = positions.size
    tm = 2048
    inv_freq = 1.0 / (ROPE_THETA ** (jnp.arange(0, HEAD_DIM, 2, dtype=F32) / HEAD_DIM))
    return pl.pallas_call(
        _rope_kernel,
        out_shape=(jax.ShapeDtypeStruct((T, LANES), F32),) * 2,
        grid=(T // tm,),
        in_specs=[pl.BlockSpec((1, tm), lambda i: (0, i)), _const_spec((HEAD_DIM // 2, 1))],
        out_specs=(pl.BlockSpec((tm, LANES), lambda i: (i, 0)),) * 2,
        compiler_params=_cparams("parallel"),
        name="rope_tables",
    )(positions.reshape(1, T), inv_freq[:, None])


def _rope_tile(t, cos, sin, lane_lt_half):
    fwd = pltpu.roll(t, LANES - HEAD_DIM // 2, axis=1)
    bwd = pltpu.roll(t, HEAD_DIM // 2, axis=1)
    partner = jnp.where(lane_lt_half, fwd, bwd)
    return t * cos + partner * sin


def _mix_in_kernel(has_vm, x_ref, g_ref, w_ref, *rest):
    if has_vm:
        (wvm_ref, cos_ref, sin_ref,
         urw_ref, vm_ref, mq_ref, mk_ref, mv_ref, sq_ref, sk_ref, sv_ref, gate_ref) = rest
    else:
        cos_ref, sin_ref, urw_ref, mq_ref, mk_ref, mv_ref, sq_ref, sk_ref, sv_ref, gate_ref = rest
    h = _rms(x_ref[...], g_ref[...]).astype(BF16)
    cos = cos_ref[...]
    sin = sin_ref[...]
    lane = lax.broadcasted_iota(jnp.int32, cos.shape, 1)
    lt_half = (lane % HEAD_DIM) < (HEAD_DIM // 2)
    first_head = lane < HEAD_DIM
    q_scale = LOG2E * HEAD_DIM ** -0.5

    col = [0]

    def seg(width):
        a = col[0]
        col[0] += width
        return _dot(h, w_ref[:, a:a + width])

    def rope_chunks(width, scale):
        acc = seg(width)
        return [_rope_tile(acc[:, c * LANES:(c + 1) * LANES], cos, sin, lt_half) * scale
                for c in range(width // LANES)]

    def store_chunks(ref, chunks):
        for c, t in enumerate(chunks):
            ref[:, c * LANES:(c + 1) * LANES] = t.astype(ref.dtype)

    urw_ref[...] = seg(RW_COLS).astype(urw_ref.dtype)
    if has_vm:
        vm_ref[...] = _dot(h, wvm_ref[...])
    store_chunks(mq_ref, rope_chunks(BRANCH, q_scale))
    store_chunks(mk_ref, rope_chunks(BRANCH, 1.0))
    mv_ref[...] = seg(BRANCH).astype(mv_ref.dtype)
    nat = rope_chunks(BRANCH, q_scale)
    half_tiles = len(nat) // 2
    paired = []
    for p in range(len(nat)):
        a, b = nat[p // 2], nat[p // 2 + half_tiles]
        if p % 2 == 0:
            paired.append(jnp.where(first_head, a, pltpu.roll(b, HEAD_DIM, axis=1)))
        else:
            paired.append(jnp.where(first_head, pltpu.roll(a, HEAD_DIM, axis=1), b))
    store_chunks(sq_ref, paired)
    store_chunks(sk_ref, rope_chunks(SW_KV_HEADS * HEAD_DIM, 1.0))
    sv_ref[...] = seg(SW_KV_HEADS * HEAD_DIM).astype(sv_ref.dtype)
    for n in range(3):
        gate_ref[:, n * D_MODEL:(n + 1) * D_MODEL] = jax.nn.sigmoid(seg(D_MODEL)).astype(gate_ref.dtype)


def _mix_in(x2, g, w_in, w_vm, cos_t, sin_t, layer):
    T = x2.shape[0]
    tm = 512
    has_vm = w_vm is not None
    kv_w = SW_KV_HEADS * HEAD_DIM
    widths = [(RW_COLS, BF16)]
    if has_vm:
        widths.append((LORA_PAD, F32))
    widths += [(BRANCH, BF16), (BRANCH, BF16), (BRANCH, BF16), (BRANCH, BF16),
               (kv_w, BF16), (kv_w, BF16), (3 * D_MODEL, BF16)]
    assert sum(w for w, _ in widths) - (LORA_PAD if has_vm else 0) == w_in.shape[2]
    row = lambda i: (i, 0)
    in_specs = [pl.BlockSpec((tm, D_MODEL), row), _layer_spec((1, D_MODEL), layer),
                _layer_spec(w_in.shape[1:], layer)]
    args = [x2, g, w_in]
    if has_vm:
        in_specs.append(_layer_spec(w_vm.shape[1:], layer - 1))
        args.append(w_vm)
    in_specs += [pl.BlockSpec((tm, LANES), row), pl.BlockSpec((tm, LANES), row)]
    args += [cos_t, sin_t]
    return pl.pallas_call(
        functools.partial(_mix_in_kernel, has_vm),
        out_shape=tuple(jax.ShapeDtypeStruct((T, w), dt) for w, dt in widths),
        grid=(T // tm,),
        in_specs=in_specs,
        out_specs=tuple(pl.BlockSpec((tm, w), row) for w, _ in widths),
        compiler_params=_cparams("parallel"),
        name="mix_in",
    )(*args)


def _shift_prev(cur, carry_ref, bi):
    rolled = pltpu.roll(cur, 1, axis=0)
    row = lax.broadcasted_iota(jnp.int32, cur.shape, 0)
    prev = jnp.where(row == 0, carry_ref[bi, 0:1, :], rolled)
    carry_ref[bi, 0:1, :] = cur[cur.shape[0] - 1:, :]
    return prev


def _split_bf16(x, terms):
    pieces = []
    for _ in range(terms):
        hi = x.astype(BF16)
        pieces.append(hi)
        x = x - hi.astype(F32)
    return pieces


def _head_sums(x, bd, terms):
    w = bd.shape[0]
    halves = []
    for c in range(x.shape[1] // w):
        pieces = _split_bf16(x[:, c * w:(c + 1) * w], terms)
        halves.append(sum(_dot(piece, bd) for piece in pieces))
    return jnp.concatenate(halves, axis=1)


def _dot_split_rhs(a, b, terms):
    return sum(_dot(a, piece) for piece in _split_bf16(b, terms))


def _rwkv_kernel(has_vres, nb, *refs):
    if has_vres:
        (u_ref, vm_ref, vf_ref, mu_ref, w0_ref, wup_ref, a0_ref, aup_ref, gup_ref, kk_ref,
         ka_ref, rk_ref, lw_ref, lb_ref, bd_ref, vmu_ref, v0_ref, vup_ref,
         o_ref, s_ref, cu_ref, cvm_ref, hout_ref) = refs
        vout_ref = None
    else:
        (u_ref, mu_ref, w0_ref, wup_ref, a0_ref, aup_ref, gup_ref, kk_ref,
         ka_ref, rk_ref, lw_ref, lb_ref, bd_ref,
         o_ref, vout_ref, s_ref, cu_ref, hout_ref) = refs
    C = RW_CHUNK
    P2 = 2 * C
    npair = BRANCH // LANES

    @pl.when(pl.program_id(1) == 0)
    def _():
        s_ref[...] = jnp.zeros_like(s_ref)
        cu_ref[...] = jnp.zeros_like(cu_ref)
        if has_vres:
            cvm_ref[...] = jnp.zeros_like(cvm_ref)

    bd = bd_ref[...]
    ti = lax.broadcasted_iota(jnp.int32, (C, C), 0)
    si = lax.broadcasted_iota(jnp.int32, (C, C), 1)
    tri = (si <= ti).astype(BF16)
    lane = lax.broadcasted_iota(jnp.int32, (1, LANES), 1)
    first_head = lane < HEAD_DIM
    pr = lax.broadcasted_iota(jnp.int32, (P2, P2), 0)
    pc = lax.broadcasted_iota(jnp.int32, (P2, P2), 1)
    same_head = (pr // C) == (pc // C)
    strict = same_head & ((pc % C) < (pr % C))
    incl = same_head & ((pc % C) <= (pr % C))
    eye = (pr == pc).astype(F32)

    def stack(t):
        t = t.astype(BF16)
        zero = jnp.zeros_like(t)
        return jnp.concatenate([jnp.where(first_head, t, zero), jnp.where(first_head, zero, t)], axis=0)

    pre = []
    inst = []
    for bi in range(nb):
        u = u_ref[bi].astype(F32)
        ul = u + mu_ref[...] * (_shift_prev(u, cu_ref, bi) - u)
        r = ul[:, 0:BRANCH]
        k = ul[:, BRANCH:2 * BRANCH]
        v = ul[:, 2 * BRANCH:3 * BRANCH]
        x_wa = ul[:, 3 * BRANCH:3 * BRANCH + LANES]
        x_g = ul[:, 3 * BRANCH + LANES:]

        z = w0_ref[...] + _dot(jnp.tanh(x_wa).astype(BF16), wup_ref[...])
        a = jax.nn.sigmoid(a0_ref[...] + _dot(x_wa.astype(BF16), aup_ref[...]))
        g = _dot(jax.nn.sigmoid(x_g).astype(BF16), gup_ref[...])
        softplus = jnp.maximum(-z, 0.0) + jnp.log1p(jnp.exp(-jnp.abs(z)))
        ell = -jnp.exp(-softplus - 0.5)

        if has_vres:
            vm = vm_ref[bi]
            vml = vm + vmu_ref[...] * (_shift_prev(vm, cvm_ref, bi) - vm)
            vgate = jax.nn.sigmoid(v0_ref[...] + _dot(vml.astype(BF16), vup_ref[...]))
            v = v + (vf_ref[bi] - v) * vgate
        else:
            vout_ref[bi] = v

        kk = k * kk_ref[...]
        kk = kk / jnp.maximum(jnp.sqrt(_head_sums(kk * kk, bd, 1)), 1e-12)
        k_eff = k * (1.0 + (a - 1.0) * ka_ref[...])
        b = kk * a
        cum = _dot_split_rhs(tri, ell, 3)
        cum_last = cum[C - 1:, :]
        g_inv = jnp.exp(-cum)
        g_tail = jnp.exp(cum_last - cum)
        kap_t = kk * jnp.exp(cum - ell)
        r_t = r * jnp.exp(cum)
        b_t = b * g_inv
        k_t = k_eff * g_inv
        b_g = b * g_tail
        k_g = k_eff * g_tail
        g_all = jnp.exp(cum_last)
        pre.append((r, k_eff, v, g))
        for p in range(npair):
            sl = slice(p * LANES, (p + 1) * LANES)
            inst.append(dict(bi=bi, p=p, sl=sl, kap=stack(kap_t[:, sl]), r=stack(r_t[:, sl]),
                             b=stack(b_t[:, sl]), k=stack(k_t[:, sl]), v=stack(v[:, sl]),
                             bg=stack(b_g[:, sl]), kg=stack(k_g[:, sl]), gall=g_all[:, sl]))

    incl2 = jnp.concatenate([incl, incl], axis=1)
    for d in inst:
        aa = _dot_nt(jnp.concatenate([d["kap"], d["r"]], axis=0),
                     jnp.concatenate([d["b"], d["k"]], axis=0))
        n = jnp.where(strict, -aa[:P2, :P2], 0.0)
        d["n"] = n.astype(BF16)
        d["t"] = eye + n
        d["a_ak"] = jnp.where(strict, aa[:P2, P2:], 0.0).astype(BF16)
        d["a_r"] = jnp.where(incl2, aa[P2:, :], 0.0).astype(BF16)
    for d in inst:
        d["akv"] = _dot(d["a_ak"], d["v"]).astype(BF16)
    for d in inst:
        d["pw"] = _dot(d["n"], d["n"]).astype(BF16)
    for _ in range(4):
        for d in inst:
            pt = _dot(d["pw"], jnp.concatenate([d["pw"], d["t"].astype(BF16)], axis=1))
            d["pw"] = pt[:, :P2].astype(BF16)
            d["t"] = d["t"] + pt[:, P2:]
    for d in inst:
        d["t"] = d["t"] + _dot(d["pw"], d["t"].astype(BF16))
    for d in inst:
        ty = _dot(d["t"].astype(BF16), jnp.concatenate([d["kap"], d["akv"]], axis=1))
        d["w_til"], d["u_til"] = ty[:, :LANES].astype(BF16), ty[:, LANES:]
    for d in inst:
        d["state"] = s_ref[d["bi"] * npair + d["p"]]
        xr = _dot_nt(jnp.concatenate([d["w_til"], d["r"]], axis=0), d["state"].astype(BF16))
        sa = -(xr[:P2] + d["u_til"])
        d["sv"] = jnp.concatenate([sa.astype(BF16), d["v"]], axis=0)
        d["rs"] = xr[P2:]
    for d in inst:
        s_ref[d["bi"] * npair + d["p"]] = (
            d["state"] * d["gall"] + _dot_tn(d["sv"], jnp.concatenate([d["bg"], d["kg"]], axis=0)))
    for d in inst:
        o_st = d["rs"] + _dot(d["a_r"], d["sv"])
        hout_ref[d["bi"], :, d["sl"]] = o_st[:C] + o_st[C:]

    inv_n = 1.0 / HEAD_DIM
    for bi in range(nb):
        r, k_eff, v, g = pre[bi]
        out = hout_ref[bi]
        mean = _head_sums(out, bd, 2) * inv_n
        cen = out - mean
        var = _head_sums(cen * cen, bd, 1) * inv_n
        normed = cen * lax.rsqrt(var + RW_LNX_EPS) * lw_ref[...] + lb_ref[...]
        bonus = _head_sums(r * k_eff * rk_ref[...], bd, 2) * v
        o_ref[bi] = ((normed + bonus) * g).astype(o_ref.dtype)


def _rwkv(u_rw, p, layer, B, S, vres=None):
    T = u_rw.shape[0]
    C = RW_CHUNK
    nb = RW_BATCH
    nc = S // C
    has_vres = vres is not None
    blk = lambda w: pl.BlockSpec((nb, C, w), lambda b, c: (b, c, 0))
    names = ("mu", "w0", "w_up", "a0", "a_up", "g_up", "k_k", "k_a", "r_k", "lnx_w", "lnx_b")
    in_specs = [blk(RW_COLS)]
    args = [u_rw.reshape(B, S, RW_COLS)]
    if has_vres:
        in_specs += [blk(LORA_PAD), blk(BRANCH)]
        args += [vres["vm"].reshape(B, S, LORA_PAD), vres["v_first"].reshape(B, S, BRANCH)]
    in_specs += [_layer_spec(p[n].shape[1:], layer) for n in names]
    in_specs.append(_const_spec((MXU_TILE, MXU_TILE)))
    args += [p[n] for n in names] + [p["bd"]]
    scratch = [pltpu.VMEM((nb * (BRANCH // LANES), 2 * C, LANES), F32),
               pltpu.VMEM((nb, 8, RW_COLS), F32)]
    if has_vres:
        vnames = ("mu", "v0", "up")
        in_specs += [_layer_spec(vres[n].shape[1:], layer - 1) for n in vnames]
        args += [vres[n] for n in vnames]
        out_shape = jax.ShapeDtypeStruct((B, S, BRANCH), BF16)
        out_specs = blk(BRANCH)
        scratch.append(pltpu.VMEM((nb, 8, LORA_PAD), F32))
    else:
        out_shape = (jax.ShapeDtypeStruct((B, S, BRANCH), BF16),
                     jax.ShapeDtypeStruct((B, S, BRANCH), F32))
        out_specs = (blk(BRANCH), blk(BRANCH))
    scratch.append(pltpu.VMEM((nb, C, BRANCH), F32))
    outs = pl.pallas_call(
        functools.partial(_rwkv_kernel, has_vres, nb),
        out_shape=out_shape,
        grid=(B // nb, nc),
        in_specs=in_specs,
        out_specs=out_specs,
        scratch_shapes=scratch,
        compiler_params=_cparams("parallel", "arbitrary"),
        name="rwkv",
    )(*args)
    if has_vres:
        return outs.reshape(T, BRANCH)
    return outs[0].reshape(T, BRANCH), outs[1].reshape(T, BRANCH)


def _moba_steps(q_ref, k_ref, v_ref, o_ref, kmean_ref, vt_ref):
    S = q_ref.shape[0]
    nb = S // MB_BLOCK
    tq = MB_BLOCK
    pair = LANES // HEAD_DIM
    tiles = [slice(t * LANES, (t + 1) * LANES) for t in range(q_ref.shape[1] // LANES)]
    heads = [(t, hh) for t in range(len(tiles)) for hh in range(pair)]
    for n in range(nb):
        blk = k_ref[n * MB_BLOCK:(n + 1) * MB_BLOCK, :].astype(F32)
        kmean_ref[n:n + 1, :] = jnp.mean(blk, axis=0, keepdims=True)
    for t, cols in enumerate(tiles):
        vt_ref[t, 0:LANES, :] = v_ref[:, cols].astype(F32).T.astype(BF16)
        vt_ref[t, LANES:, :] = jnp.ones((vt_ref.shape[1] - LANES, S), BF16)
    lane = lax.broadcasted_iota(jnp.int32, (1, LANES), 1)
    head_lanes = [(lane // HEAD_DIM) == hh for hh in range(pair)]

    blk_id = lax.broadcasted_iota(jnp.int32, (nb, S), 0)
    own = lax.broadcasted_iota(jnp.int32, (nb, S), 1) // MB_BLOCK
    valid = blk_id < own
    sels = []
    for t, hh in heads:
        kmean = kmean_ref[:, tiles[t]]
        kmean_rep = jnp.concatenate(
            [kmean] + [jnp.broadcast_to(kmean[m:m + 1, :], (nb, LANES)) for m in range(nb - 1)], axis=0)
        pieces = _split_bf16(jnp.where(head_lanes[hh], kmean_rep, 0.0), 3)
        gates = sum(_dot_nt(piece, q_ref[:, tiles[t]]) for piece in pieces)
        gate = gates[:nb]
        cnt = jnp.zeros((nb, S), jnp.int32)
        for m in range(nb - 1):
            gm = gates[nb * (m + 1):nb * (m + 2)]
            beats = (own > m) & ((gm > gate) | ((gm == gate) & (m < blk_id)))
            cnt = cnt + beats.astype(jnp.int32)
        sels.append((valid & (cnt < MB_TOPK)).astype(F32))

    ki = lax.broadcasted_iota(jnp.int32, (tq, tq), 0)
    qi = lax.broadcasted_iota(jnp.int32, (tq, tq), 1)
    causal = ki <= qi

    def scores(i):
        out = []
        for t, hh in heads:
            q_t = q_ref[i * tq:(i + 1) * tq, tiles[t]]
            k_all = k_ref[0:(i + 1) * MB_BLOCK, tiles[t]]
            out.append(_dot_nt(k_all, jnp.where(head_lanes[hh], q_t, jnp.zeros_like(q_t))))
        return out

    state = {"s": scores(0)}

    def tile(i):
        s_cur = state["s"]
        if i + 1 < nb:
            state["s"] = scores(i + 1)
        probs = []
        for hh in range(len(heads)):
            s = s_cur[hh]
            s_own = jnp.where(causal, s[i * MB_BLOCK:, :], MASK_VALUE)
            m_col = jnp.max(s_own, axis=0, keepdims=True)
            keeps = []
            for j in range(i):
                keep = sels[hh][j:j + 1, i * tq:(i + 1) * tq] > 0.5
                blk_max = jnp.max(s[j * MB_BLOCK:(j + 1) * MB_BLOCK, :], axis=0, keepdims=True)
                m_col = jnp.maximum(m_col, jnp.where(keep, blk_max, MASK_VALUE))
                keeps.append(keep)
            parts = [jnp.exp2(s[j * MB_BLOCK:(j + 1) * MB_BLOCK, :] - jnp.where(keeps[j], m_col, -MASK_VALUE))
                     for j in range(i)]
            parts.append(jnp.exp2(s_own - m_col))
            p = parts[0] if len(parts) == 1 else jnp.concatenate(parts, axis=0)
            probs.append(p.astype(BF16))
        ot = [_dot(vt_ref[t, :, 0:(i + 1) * MB_BLOCK], p) for (t, _), p in zip(heads, probs)]
        ot = [o[:LANES] / o[LANES:LANES + 1] for o in ot]
        for t, cols in enumerate(tiles):
            o_t = jnp.concatenate([ot[pair * t][:HEAD_DIM], ot[pair * t + 1][HEAD_DIM:]], axis=0).T
            o_ref[i * tq:(i + 1) * tq, cols] = o_t.astype(o_ref.dtype)

    return [functools.partial(tile, i) for i in range(nb)]


def _swa_steps(q_ref, k_ref, v_ref, sink_ref, o_ref, vt_ref):
    S = q_ref.shape[0]
    W = SW_WINDOW
    pair = LANES // HEAD_DIM
    tiles = [slice(t * LANES, (t + 1) * LANES) for t in range(q_ref.shape[1] // LANES)]
    vt_ref[0:LANES, :] = v_ref[...].astype(F32).T.astype(BF16)
    vt_ref[LANES:, :] = jnp.ones((vt_ref.shape[0] - LANES, S), BF16)
    lane = lax.broadcasted_iota(jnp.int32, (1, LANES), 1)
    dist = (lax.broadcasted_iota(jnp.int32, (2 * W, W), 1) + W
            - lax.broadcasted_iota(jnp.int32, (2 * W, W), 0))
    band = (dist >= 0) & (dist < W)
    causal = (lax.broadcasted_iota(jnp.int32, (W, W), 0)
              <= lax.broadcasted_iota(jnp.int32, (W, W), 1))
    head_lanes = [(lane // HEAD_DIM) == hh for hh in range(pair)]
    sinks = [[sink_ref[t, hh:hh + 1, 0:1] * LOG2E for hh in range(pair)]
             for t in range(len(tiles))]
    group = SWA_GROUP

    def scores(g0):
        work = []
        for n in range(g0, g0 + group):
            k0 = max(n - 1, 0) * W
            k1 = (n + 1) * W
            mask = causal if n == 0 else band
            k_w = k_ref[k0:k1, :]
            for t, cols in enumerate(tiles):
                q_t = q_ref[n * W:(n + 1) * W, cols]
                for hh in range(pair):
                    q_h = jnp.where(head_lanes[hh], q_t, jnp.zeros_like(q_t))
                    work.append(dict(n=n, t=t, hh=hh, k0=k0, k1=k1, mask=mask, s=_dot_nt(k_w, q_h)))
        return work

    state = {"work": scores(0)}

    def run_group(g0):
        work = state["work"]
        if g0 + group < S // W:
            state["work"] = scores(g0 + group)
        for d in work:
            s = jnp.where(d["mask"], d["s"], MASK_VALUE)
            sink = sinks[d["t"]][d["hh"]]
            m_col = jnp.maximum(jnp.max(s, axis=0, keepdims=True), sink)
            d["p"] = jnp.exp2(s - m_col).astype(BF16)
            d["sink_term"] = jnp.exp2(sink - m_col)
        for d in work:
            ot = _dot(vt_ref[:, d["k0"]:d["k1"]], d["p"])
            d["ot"] = ot[:LANES] / (ot[LANES:LANES + 1] + d["sink_term"])
        for t, cols in enumerate(tiles):
            mine = [d for d in work if d["t"] == t]
            o_t = jnp.concatenate(
                [jnp.concatenate([a["ot"][:HEAD_DIM], b["ot"][HEAD_DIM:]], axis=0)
                 for a, b in zip(mine[0::2], mine[1::2])], axis=1)
            o_ref[g0 * W:(g0 + group) * W, cols] = o_t.T.astype(o_ref.dtype)

    return [functools.partial(run_group, g0) for g0 in range(0, S // W, group)]


def _attn_kernel(mq_ref, mk_ref, mv_ref, sq_ref, sk_ref, sv_ref, sink_ref, omb_ref, osw_ref,
                 kmean_ref, mvt_ref, svt_ref):
    moba = _moba_steps(mq_ref, mk_ref, mv_ref, omb_ref, kmean_ref, mvt_ref)
    swa = _swa_steps(sq_ref, sk_ref, sv_ref, sink_ref, osw_ref, svt_ref)
    per = len(moba) // len(swa)
    for g, swa_group in enumerate(swa):
        for moba_tile in moba[g * per:(g + 1) * per]:
            moba_tile()
        swa_group()


def _attn(mq, mk, mv, sq, sk, sv, sinks, B, S, layer):
    T = mq.shape[0]
    nt = ATTN_TILES
    width = nt * LANES
    spec = pl.BlockSpec((S, width), lambda b, p: (b, p))
    kv_spec = pl.BlockSpec((S, LANES), lambda b, p: (b, 0))
    return pl.pallas_call(
        _attn_kernel,
        out_shape=(jax.ShapeDtypeStruct((T, BRANCH), BF16),) * 2,
        grid=(B, BRANCH // width),
        in_specs=[spec, spec, spec, spec, kv_spec, kv_spec,
                  pl.BlockSpec((None, nt, 2, LANES), lambda b, p: (layer, p, 0, 0))],
        out_specs=(spec, spec),
        scratch_shapes=[pltpu.VMEM((S // MB_BLOCK, width), F32),
                        pltpu.VMEM((nt, LANES + BF16_SUBLANES, S), BF16),
                        pltpu.VMEM((LANES + BF16_SUBLANES, S), BF16)],
        compiler_params=_cparams("parallel", "parallel"),
        name="attn",
    )(mq, mk, mv, sq, sk, sv, sinks)


def _mix_out_kernel(x_ref, orw_ref, omb_ref, osw_ref, gate_ref, wb_ref, wo_ref, g_ref, out_ref):
    tm = x_ref.shape[0]
    subs = [slice(r, r + MIX_OUT_SUB) for r in range(0, tm, MIX_OUT_SUB)]

    def gated(rows):
        y = None
        for n, o_ref in enumerate((orw_ref, omb_ref, osw_ref)):
            t = gate_ref[rows, n * D_MODEL:(n + 1) * D_MODEL] * _dot(o_ref[rows, :], wb_ref[n])
            y = t if y is None else y + t
        return y.astype(BF16)

    y_next = gated(subs[0])
    for i, rows in enumerate(subs):
        y = y_next
        if i + 1 < len(subs):
            y_next = gated(subs[i + 1])
        out_ref[rows, :] = x_ref[rows, :] + _rms(_dot(y, wo_ref[...]), g_ref[...])


def _mix_out(x2, o_rw, o_mb, o_sw, gates, w_branch, w_out, g, layer):
    T = x2.shape[0]
    tm = 1024
    row = lambda i: (i, 0)
    br = pl.BlockSpec((tm, BRANCH), row)
    return pl.pallas_call(
        _mix_out_kernel,
        out_shape=jax.ShapeDtypeStruct((T, D_MODEL), F32),
        grid=(T // tm,),
        in_specs=[pl.BlockSpec((tm, D_MODEL), row), br, br, br,
                  pl.BlockSpec((tm, 3 * D_MODEL), row),
                  _layer_spec(w_branch.shape[1:], layer), _layer_spec(w_out.shape[1:], layer),
                  _layer_spec((1, D_MODEL), layer)],
        out_specs=pl.BlockSpec((tm, D_MODEL), row),
        compiler_params=_cparams("parallel"),
        name="mix_out",
    )(x2, o_rw, o_mb, o_sw, gates, w_branch, w_out, g)


def _mem_kv_kernel(m_ref, g_ref, wk_ref, wv_ref, k_ref, vt_ref):
    m = _rms(m_ref[...], g_ref[...]).astype(BF16)
    k_ref[...] = _dot(m, wk_ref[...]).astype(k_ref.dtype)
    vt_ref[...] = _dot(m, wv_ref[...]).T.astype(vt_ref.dtype)


def _mem_kv(mem2, g, wk, wv, M, layer):
    R = mem2.shape[0]
    row = lambda i: (i, 0)
    return pl.pallas_call(
        _mem_kv_kernel,
        out_shape=(jax.ShapeDtypeStruct((R, XA_WIDTH), BF16),
                   jax.ShapeDtypeStruct((R // M, XA_WIDTH, M), BF16)),
        grid=(R // M,),
        in_specs=[pl.BlockSpec((M, D_MODEL), row), _layer_spec((1, D_MODEL), layer),
                  _layer_spec(wk.shape[1:], layer), _layer_spec(wv.shape[1:], layer)],
        out_specs=(pl.BlockSpec((M, XA_WIDTH), row),
                   pl.BlockSpec((None, XA_WIDTH, M), lambda i: (i, 0, 0))),
        compiler_params=_cparams("parallel"),
        name="mem_kv",
    )(mem2, g, wk, wv)


def _xattn_kernel(x_ref, k_ref, vt_ref, gpre_ref, wq_ref, wo_ref, gpost_ref, out_ref):
    tm = x_ref.shape[0]
    groups = [slice(r, r + XATTN_SUB) for r in range(0, tm, XATTN_SUB)]
    heads = [slice(hd * XA_HEAD_DIM, (hd + 1) * XA_HEAD_DIM) for hd in range(XA_HEADS)]
    qs = []
    for rows in groups:
        h = _rms(x_ref[rows, :], gpre_ref[...]).astype(BF16)
        qs.append((_dot(h, wq_ref[...]) * (LOG2E * XA_HEAD_DIM ** -0.5)).astype(BF16))
    scores = [[_dot_nt(k_ref[:, sl], q[:, sl]) for sl in heads] for q in qs]
    ones = jnp.ones((BF16_SUBLANES, vt_ref.shape[1]), BF16)
    vts = [jnp.concatenate([vt_ref[sl, :], ones], axis=0) for sl in heads]
    outs = []
    for grp in scores:
        probs = [jnp.exp2(s - jnp.max(s, axis=0, keepdims=True)).astype(BF16) for s in grp]
        pv = [_dot(vt, p) for vt, p in zip(vts, probs)]
        o_t = jnp.concatenate([o[:XA_HEAD_DIM] / o[XA_HEAD_DIM:XA_HEAD_DIM + 1] for o in pv], axis=0)
        outs.append(o_t.T.astype(BF16))
    ys = [_dot(o, wo_ref[...]) for o in outs]
    for rows, y in zip(groups, ys):
        out_ref[rows, :] = x_ref[rows, :] + _rms(y, gpost_ref[...])


def _xattn(x2, k_mem, vt_mem, g_pre, wq, wo, g_post, B, S, M, layer):
    T = x2.shape[0]
    tm = 1024
    nt = S // tm
    row = lambda b, i: (b * nt + i, 0)
    return pl.pallas_call(
        _xattn_kernel,
        out_shape=jax.ShapeDtypeStruct((T, D_MODEL), F32),
        grid=(B, nt),
        in_specs=[pl.BlockSpec((tm, D_MODEL), row),
                  pl.BlockSpec((M, XA_WIDTH), lambda b, i: (b, 0)),
                  pl.BlockSpec((None, XA_WIDTH, M), lambda b, i: (b, 0, 0)),
                  _layer_spec((1, D_MODEL), layer), _layer_spec(wq.shape[1:], layer),
                  _layer_spec(wo.shape[1:], layer), _layer_spec((1, D_MODEL), layer)],
        out_specs=pl.BlockSpec((tm, D_MODEL), row),
        compiler_params=_cparams("parallel", "parallel"),
        name="xattn",
    )(x2, k_mem, vt_mem, g_pre, wq, wo, g_post)


def _mlp_kernel(x_ref, gpre_ref, wup_ref, wdn_ref, gpost_ref, out_ref, h_ref, acc_ref):
    j = pl.program_id(1)

    @pl.when(j == 0)
    def _():
        h_ref[...] = _rms(x_ref[...], gpre_ref[...]).astype(BF16)
        acc_ref[...] = jnp.zeros_like(acc_ref)

    h = h_ref[...]
    n_sub = wup_ref.shape[1] // MLP_SUB

    def up(c):
        return _dot(h, wup_ref[:, c * MLP_SUB:(c + 1) * MLP_SUB])

    a_next = up(0)
    acc = None
    for c in range(n_sub):
        a = jnp.maximum(a_next, 0.0)
        if c + 1 < n_sub:
            a_next = up(c + 1)
        t = _dot((a * a).astype(BF16), wdn_ref[c * MLP_SUB:(c + 1) * MLP_SUB, :])
        acc = t if acc is None else acc + t
    acc_ref[...] += acc

    @pl.when(j == pl.num_programs(1) - 1)
    def _():
        out_ref[...] = x_ref[...] + _rms(acc_ref[...], gpost_ref[...])


def _mlp(x2, g_pre, w_up, w_down, g_post, layer):
    T = x2.shape[0]
    tm = MLP_ROWS
    tf = MLP_COLS
    row = lambda i, j: (i, 0)
    return pl.pallas_call(
        _mlp_kernel,
        out_shape=jax.ShapeDtypeStruct((T, D_MODEL), F32),
        grid=(T // tm, D_FF // tf),
        in_specs=[pl.BlockSpec((tm, D_MODEL), row), _layer_spec((1, D_MODEL), layer),
                  pl.BlockSpec((None, D_MODEL, tf), lambda i, j: (layer, 0, j)),
                  pl.BlockSpec((None, tf, D_MODEL), lambda i, j: (layer, j, 0)),
                  _layer_spec((1, D_MODEL), layer)],
        out_specs=pl.BlockSpec((tm, D_MODEL), row),
        scratch_shapes=[pltpu.VMEM((tm, D_MODEL), BF16), pltpu.VMEM((tm, D_MODEL), F32)],
        compiler_params=_cparams("parallel", "arbitrary"),
        name="mlp",
    )(x2, g_pre, w_up, w_down, g_post)


def kernel(x, mem, positions, norm_mix_pre, norm_mix_post, norm_xattn_pre, norm_xattn_post, norm_mem, norm_mlp_pre, norm_mlp_post, w_in, rw_mu, rw_w0, rw_w_up, rw_a0, rw_a_up, rw_g_up, rw_k_k, rw_k_a, rw_r_k, rw_lnx_w, rw_lnx_b, rw_vres_down, rw_vres_mu, rw_v0, rw_vres_up, sw_sinks, w_branch, w_out, w_xq, w_xk, w_xv, w_xo, w_up, w_down):
    B, S, D = x.shape
    M = mem.shape[1]
    depth = w_in.shape[0]
    T = B * S
    x2 = x.reshape(T, D)
    mem2 = mem.reshape(B * M, D)
    cos_t, sin_t = _rope_tables(positions)

    rows = lambda t: t.reshape(t.shape[0], 1, -1)
    bf = lambda t: t.astype(BF16)
    lora_in = rw_w_up.shape[1]
    per_kv = N_HEADS // SW_KV_HEADS
    head_of_col = jnp.arange(MXU_TILE) // HEAD_DIM
    rw_p = dict(
        mu=rows(rw_mu), w0=rows(rw_w0), a0=rows(rw_a0), k_k=rows(rw_k_k), k_a=rows(rw_k_a),
        r_k=rows(rw_r_k), lnx_w=rows(rw_lnx_w), lnx_b=rows(rw_lnx_b),
        w_up=bf(jnp.pad(rw_w_up, ((0, 0), (0, LORA_PAD - lora_in), (0, 0)))),
        a_up=bf(jnp.pad(rw_a_up, ((0, 0), (lora_in, LORA_PAD - lora_in - rw_a_up.shape[1]), (0, 0)))),
        g_up=bf(rw_g_up),
        bd=(head_of_col[:, None] == head_of_col[None, :]).astype(BF16))
    mv_pad = LORA_PAD - rw_vres_down.shape[2]
    w_vm = bf(jnp.pad(rw_vres_down, ((0, 0), (0, 0), (0, mv_pad))))
    vres_p = dict(mu=rows(jnp.pad(rw_vres_mu, ((0, 0), (0, mv_pad)))), v0=rows(rw_v0),
                  up=bf(jnp.pad(rw_vres_up, ((0, 0), (0, mv_pad), (0, 0)))))
    sinks = jnp.broadcast_to(
        sw_sinks.reshape(depth, SW_KV_HEADS, per_kv).transpose(0, 2, 1)[..., None].astype(F32),
        (depth, per_kv, SW_KV_HEADS, LANES))
    wb_sw = w_branch[:, 2].reshape(depth, SW_KV_HEADS, per_kv, HEAD_DIM, D)
    wb_sw = wb_sw.transpose(0, 2, 1, 3, 4).reshape(depth, 1, BRANCH, D)
    wb = bf(jnp.concatenate([w_branch[:, :2], wb_sw], axis=1))
    w_in_b, w_out_b, w_up_b, w_down_b = bf(w_in), bf(w_out), bf(w_up), bf(w_down)
    w_xq_b, w_xk_b, w_xv_b, w_xo_b = bf(w_xq), bf(w_xk), bf(w_xv), bf(w_xo)
    g_mix_pre, g_mix_post = rows(norm_mix_pre), rows(norm_mix_post)
    g_xa_pre, g_xa_post, g_mem = rows(norm_xattn_pre), rows(norm_xattn_post), rows(norm_mem)
    g_mlp_pre, g_mlp_post = rows(norm_mlp_pre), rows(norm_mlp_post)

    v_first = None
    for l in range(depth):
        outs = _mix_in(x2, g_mix_pre, w_in_b, w_vm if l > 0 else None, cos_t, sin_t, l)
        if l > 0:
            u_rw, vm, mq, mk, mv, sq, sk, sv, gates = outs
            o_rw = _rwkv(u_rw, rw_p, l, B, S, dict(vres_p, vm=vm, v_first=v_first))
        else:
            u_rw, mq, mk, mv, sq, sk, sv, gates = outs
            o_rw, v_first = _rwkv(u_rw, rw_p, l, B, S)
        o_mb, o_sw = _attn(mq, mk, mv, sq, sk, sv, sinks, B, S, l)
        x2 = _mix_out(x2, o_rw, o_mb, o_sw, gates, wb, w_out_b, g_mix_post, l)
        k_mem, vt_mem = _mem_kv(mem2, g_mem, w_xk_b, w_xv_b, M, l)
        x2 = _xattn(x2, k_mem, vt_mem, g_xa_pre, w_xq_b, w_xo_b, g_xa_post, B, S, M, l)
        x2 = _mlp(x2, g_mlp_pre, w_up_b, w_down_b, g_mlp_post, l)
    return x2.reshape(B, S, D)
```

```python
import functools

import jax
import jax.numpy as jnp
from jax import lax
from jax.experimental import pallas as pl
from jax.experimental.pallas import tpu as pltpu

F32 = jnp.float32
BF16 = jnp.bfloat16

D_MODEL = 1024
HEAD_DIM = 64
BRANCH = 512
N_HEADS = BRANCH // HEAD_DIM
RW_COLS = 3 * BRANCH + 64 + 64 + 128
LORA_PAD = 128
MB_BLOCK = 256
MB_TOPK = 3
SW_WINDOW = 128
SW_KV_HEADS = 2
XA_HEADS = 4
XA_HEAD_DIM = 128
XA_WIDTH = XA_HEADS * XA_HEAD_DIM
D_FF = 4 * D_MODEL
ROPE_THETA = 10000.0
NORM_EPS = 1e-6
RW_LNX_EPS = 1e-5 * HEAD_DIM
MASK_VALUE = -1e30
LOG2E = 1.4426950408889634

LANES = 128
MXU_TILE = 256
BF16_SUBLANES = 16
RW_CHUNK = 64
ATTN_TILES = 2
SWA_GROUP = 4
XATTN_SUB = 256
MIX_OUT_SUB = 512
MLP_ROWS = 1024
MLP_COLS = 2048
MLP_SUB = 512
RW_SUBCHUNKS = 2
RW_BATCH = 4
VMEM_LIMIT = 56 * 1024 * 1024


def _cparams(*sem):
    return pltpu.CompilerParams(dimension_semantics=sem, vmem_limit_bytes=VMEM_LIMIT)


def _dot(a, b, precision=None):
    return jnp.dot(a, b, preferred_element_type=F32, precision=precision)


def _dot_nt(a, b, precision=None):
    return lax.dot_general(a, b, (((1,), (1,)), ((), ())),
                           preferred_element_type=F32, precision=precision)


def _dot_tn(a, b, precision=None):
    return lax.dot_general(a, b, (((0,), (0,)), ((), ())),
                           preferred_element_type=F32, precision=precision)


def _rms(xf, g):
    ms = jnp.mean(xf * xf, axis=-1, keepdims=True)
    return xf * lax.rsqrt(ms + NORM_EPS) * g


def _const_spec(shape):
    nd = len(shape)
    return pl.BlockSpec(shape, lambda *_: (0,) * nd, pipeline_mode=pl.Buffered(1))


def _layer_spec(shape, layer):
    nd = len(shape)
    return pl.BlockSpec((None,) + tuple(shape), lambda *_: (layer,) + (0,) * nd,
                        pipeline_mode=pl.Buffered(1))


def _rope_kernel(pos_ref, invf_ref, cos_ref, sin_ref):
    ang = invf_ref[...] * pos_ref[...].astype(F32)
    c = jnp.cos(ang)
    s = jnp.sin(ang)
    reps = LANES // HEAD_DIM
    cos_ref[...] = jnp.concatenate([c, c] * reps, axis=0).T
    sin_ref[...] = jnp.concatenate([-s, s] * reps, axis=0).T


def _rope_tables(positions):
    T = positions.size
    tm = 2048
    inv_freq = 1.0 / (ROPE_THETA ** (jnp.arange(0, HEAD_DIM, 2, dtype=F32) / HEAD_DIM))
    return pl.pallas_call(
        _rope_kernel,
        out_shape=(jax.ShapeDtypeStruct((T, LANES), F32),) * 2,
        grid=(T // tm,),
        in_specs=[pl.BlockSpec((1, tm), lambda i: (0, i)), _const_spec((HEAD_DIM // 2, 1))],
        out_specs=(pl.BlockSpec((tm, LANES), lambda i: (i, 0)),) * 2,
        compiler_params=_cparams("parallel"),
        name="rope_tables",
    )(positions.reshape(1, T), inv_freq[:, None])


def _rope_tile(t, cos, sin, lane_lt_half):
    fwd = pltpu.roll(t, LANES - HEAD_DIM // 2, axis=1)
    bwd = pltpu.roll(t, HEAD_DIM // 2, axis=1)
    partner = jnp.where(lane_lt_half, fwd, bwd)
    return t * cos + partner * sin


def _mix_in_kernel(has_vm, x_ref, g_ref, w_ref, *rest):
    if has_vm:
        (wvm_ref, cos_ref, sin_ref,
         urw_ref, vm_ref, mq_ref, mk_ref, mv_ref, sq_ref, sk_ref, sv_ref, gate_ref) = rest
    else:
        cos_ref, sin_ref, urw_ref, mq_ref, mk_ref, mv_ref, sq_ref, sk_ref, sv_ref, gate_ref = rest
    h = _rms(x_ref[...], g_ref[...]).astype(BF16)
    cos = cos_ref[...]
    sin = sin_ref[...]
    lane = lax.broadcasted_iota(jnp.int32, cos.shape, 1)
    lt_half = (lane % HEAD_DIM) < (HEAD_DIM // 2)
    first_head = lane < HEAD_DIM
    q_scale = LOG2E * HEAD_DIM ** -0.5

    col = [0]

    def seg(width):
        a = col[0]
        col[0] += width
        return _dot(h, w_ref[:, a:a + width])

    def rope_chunks(width, scale):
        acc = seg(width)
        return [_rope_tile(acc[:, c * LANES:(c + 1) * LANES], cos, sin, lt_half) * scale
                for c in range(width // LANES)]

    def store_chunks(ref, chunks):
        for c, t in enumerate(chunks):
            ref[:, c * LANES:(c + 1) * LANES] = t.astype(ref.dtype)

    urw_ref[...] = seg(RW_COLS).astype(urw_ref.dtype)
    if has_vm:
        vm_ref[...] = _dot(h, wvm_ref[...])
    store_chunks(mq_ref, rope_chunks(BRANCH, q_scale))
    store_chunks(mk_ref, rope_chunks(BRANCH, 1.0))
    mv_ref[...] = seg(BRANCH).astype(mv_ref.dtype)
    nat = rope_chunks(BRANCH, q_scale)
    half_tiles = len(nat) // 2
    paired = []
    for p in range(len(nat)):
        a, b = nat[p // 2], nat[p // 2 + half_tiles]
        if p % 2 == 0:
            paired.append(jnp.where(first_head, a, pltpu.roll(b, HEAD_DIM, axis=1)))
        else:
            paired.append(jnp.where(first_head, pltpu.roll(a, HEAD_DIM, axis=1), b))
    store_chunks(sq_ref, paired)
    store_chunks(sk_ref, rope_chunks(SW_KV_HEADS * HEAD_DIM, 1.0))
    sv_ref[...] = seg(SW_KV_HEADS * HEAD_DIM).astype(sv_ref.dtype)
    for n in range(3):
        gate_ref[:, n * D_MODEL:(n + 1) * D_MODEL] = jax.nn.sigmoid(seg(D_MODEL)).astype(gate_ref.dtype)


def _mix_in(x2, g, w_in, w_vm, cos_t, sin_t, layer):
    T = x2.shape[0]
    tm = 512
    has_vm = w_vm is not None
    kv_w = SW_KV_HEADS * HEAD_DIM
    widths = [(RW_COLS, BF16)]
    if has_vm:
        widths.append((LORA_PAD, F32))
    widths += [(BRANCH, BF16), (BRANCH, BF16), (BRANCH, BF16), (BRANCH, BF16),
               (kv_w, BF16), (kv_w, BF16), (3 * D_MODEL, BF16)]
    assert sum(w for w, _ in widths) - (LORA_PAD if has_vm else 0) == w_in.shape[2]
    row = lambda i: (i, 0)
    in_specs = [pl.BlockSpec((tm, D_MODEL), row), _layer_spec((1, D_MODEL), layer),
                _layer_spec(w_in.shape[1:], layer)]
    args = [x2, g, w_in]
    if has_vm:
        in_specs.append(_layer_spec(w_vm.shape[1:], layer - 1))
        args.append(w_vm)
    in_specs += [pl.BlockSpec((tm, LANES), row), pl.BlockSpec((tm, LANES), row)]
    args += [cos_t, sin_t]
    return pl.pallas_call(
        functools.partial(_mix_in_kernel, has_vm),
        out_shape=tuple(jax.ShapeDtypeStruct((T, w), dt) for w, dt in widths),
        grid=(T // tm,),
        in_specs=in_specs,
        out_specs=tuple(pl.BlockSpec((tm, w), row) for w, _ in widths),
        compiler_params=_cparams("parallel"),
        name="mix_in",
    )(*args)


def _shift_prev(cur, carry_ref, bi):
    rolled = pltpu.roll(cur, 1, axis=0)
    row = lax.broadcasted_iota(jnp.int32, cur.shape, 0)
    prev = jnp.where(row == 0, carry_ref[bi, 0:1, :], rolled)
    carry_ref[bi, 0:1, :] = cur[cur.shape[0] - 1:, :]
    return prev


def _split_bf16(x, terms):
    pieces = []
    for _ in range(terms):
        hi = x.astype(BF16)
        pieces.append(hi)
        x = x - hi.astype(F32)
    return pieces


def _head_sums(x, bd, terms):
    w = bd.shape[0]
    halves = []
    for c in range(x.shape[1] // w):
        pieces = _split_bf16(x[:, c * w:(c + 1) * w], terms)
        halves.append(sum(_dot(piece, bd) for piece in pieces))
    return jnp.concatenate(halves, axis=1)


def _dot_split_rhs(a, b, terms):
    return sum(_dot(a, piece) for piece in _split_bf16(b, terms))


def _rwkv_kernel(has_vres, nb, *refs):
    if has_vres:
        (u_ref, vm_ref, vf_ref, mu_ref, w0_ref, wup_ref, a0_ref, aup_ref, gup_ref, kk_ref,
         ka_ref, rk_ref, lw_ref, lb_ref, bd_ref, vmu_ref, v0_ref, vup_ref,
         o_ref, s_ref, cu_ref, cvm_ref, hout_ref) = refs
        vout_ref = None
    else:
        (u_ref, mu_ref, w0_ref, wup_ref, a0_ref, aup_ref, gup_ref, kk_ref,
         ka_ref, rk_ref, lw_ref, lb_ref, bd_ref,
         o_ref, vout_ref, s_ref, cu_ref, hout_ref) = refs
    C = RW_CHUNK
    P2 = 2 * C
    npair = BRANCH // LANES
    n_sub = u_ref.shape[1] // C

    @pl.when(pl.program_id(1) == 0)
    def _():
        s_ref[...] = jnp.zeros_like(s_ref)
        cu_ref[...] = jnp.zeros_like(cu_ref)
        if has_vres:
            cvm_ref[...] = jnp.zeros_like(cvm_ref)

    bd = bd_ref[...]
    ti = lax.broadcasted_iota(jnp.int32, (C, C), 0)
    si = lax.broadcasted_iota(jnp.int32, (C, C), 1)
    tri = (si <= ti).astype(BF16)
    lane = lax.broadcasted_iota(jnp.int32, (1, LANES), 1)
    first_head = lane < HEAD_DIM
    pr = lax.broadcasted_iota(jnp.int32, (P2, P2), 0)
    pc = lax.broadcasted_iota(jnp.int32, (P2, P2), 1)
    same_head = (pr // C) == (pc // C)
    strict = same_head & ((pc % C) < (pr % C))
    incl = same_head & ((pc % C) <= (pr % C))
    eye = (pr == pc).astype(F32)

    def stack(t):
        t = t.astype(BF16)
        zero = jnp.zeros_like(t)
        return jnp.concatenate([jnp.where(first_head, t, zero), jnp.where(first_head, zero, t)], axis=0)

    pre = []
    inst = []
    for bi in range(nb):
        u = u_ref[bi].astype(F32)
        ul = u + mu_ref[...] * (_shift_prev(u, cu_ref, bi) - u)
        r = ul[:, 0:BRANCH]
        k = ul[:, BRANCH:2 * BRANCH]
        v = ul[:, 2 * BRANCH:3 * BRANCH]
        x_wa = ul[:, 3 * BRANCH:3 * BRANCH + LANES]
        x_g = ul[:, 3 * BRANCH + LANES:]

        z = w0_ref[...] + _dot(jnp.tanh(x_wa).astype(BF16), wup_ref[...])
        a = jax.nn.sigmoid(a0_ref[...] + _dot(x_wa.astype(BF16), aup_ref[...]))
        g = _dot(jax.nn.sigmoid(x_g).astype(BF16), gup_ref[...])
        softplus = jnp.maximum(-z, 0.0) + jnp.log1p(jnp.exp(-jnp.abs(z)))
        ell = -jnp.exp(-softplus - 0.5)

        if has_vres:
            vm = vm_ref[bi]
            vml = vm + vmu_ref[...] * (_shift_prev(vm, cvm_ref, bi) - vm)
            vgate = jax.nn.sigmoid(v0_ref[...] + _dot(vml.astype(BF16), vup_ref[...]))
            v = v + (vf_ref[bi] - v) * vgate
        else:
            vout_ref[bi] = v

        kk = k * kk_ref[...]
        kk = kk / jnp.maximum(jnp.sqrt(_head_sums(kk * kk, bd, 1)), 1e-12)
        k_eff = k * (1.0 + (a - 1.0) * ka_ref[...])
        b = kk * a
        pre.append((r, k_eff, v, g))
        for ci in range(n_sub):
            rs = slice(ci * C, (ci + 1) * C)
            ell_c = ell[rs]
            cum = _dot_split_rhs(tri, ell_c, 3)
            cum_last = cum[C - 1:, :]
            g_inv = jnp.exp(-cum)
            g_tail = jnp.exp(cum_last - cum)
            kap_t = kk[rs] * jnp.exp(cum - ell_c)
            r_t = r[rs] * jnp.exp(cum)
            b_t = b[rs] * g_inv
            k_t = k_eff[rs] * g_inv
            b_g = b[rs] * g_tail
            k_g = k_eff[rs] * g_tail
            g_all = jnp.exp(cum_last)
            v_c = v[rs]
            for p in range(npair):
                sl = slice(p * LANES, (p + 1) * LANES)
                inst.append(dict(bi=bi, ci=ci, rs=rs, p=p, sl=sl,
                                 kap=stack(kap_t[:, sl]), r=stack(r_t[:, sl]), b=stack(b_t[:, sl]),
                                 k=stack(k_t[:, sl]), v=stack(v_c[:, sl]), bg=stack(b_g[:, sl]),
                                 kg=stack(k_g[:, sl]), gall=g_all[:, sl]))

    incl2 = jnp.concatenate([incl, incl], axis=1)
    for d in inst:
        aa = _dot_nt(jnp.concatenate([d["kap"], d["r"]], axis=0),
                     jnp.concatenate([d["b"], d["k"]], axis=0))
        n = jnp.where(strict, -aa[:P2, :P2], 0.0)
        d["n"] = n.astype(BF16)
        d["t"] = eye + n
        d["a_ak"] = jnp.where(strict, aa[:P2, P2:], 0.0).astype(BF16)
        d["a_r"] = jnp.where(incl2, aa[P2:, :], 0.0).astype(BF16)
    for d in inst:
        d["akv"] = _dot(d["a_ak"], d["v"]).astype(BF16)
    for d in inst:
        d["pw"] = _dot(d["n"], d["n"]).astype(BF16)
    for _ in range(4):
        for d in inst:
            pt = _dot(d["pw"], jnp.concatenate([d["pw"], d["t"].astype(BF16)], axis=1))
            d["pw"] = pt[:, :P2].astype(BF16)
            d["t"] = d["t"] + pt[:, P2:]
    for d in inst:
        d["t"] = d["t"] + _dot(d["pw"], d["t"].astype(BF16))
    for d in inst:
        ty = _dot(d["t"].astype(BF16), jnp.concatenate([d["kap"], d["akv"]], axis=1))
        d["w_til"], d["u_til"] = ty[:, :LANES].astype(BF16), ty[:, LANES:]
    for ci in range(n_sub):
        chunk = [d for d in inst if d["ci"] == ci]
        for d in chunk:
            d["state"] = s_ref[d["bi"] * npair + d["p"]]
            xr = _dot_nt(jnp.concatenate([d["w_til"], d["r"]], axis=0), d["state"].astype(BF16))
            sa = -(xr[:P2] + d["u_til"])
            d["sv"] = jnp.concatenate([sa.astype(BF16), d["v"]], axis=0)
            d["rs_out"] = xr[P2:]
        for d in chunk:
            s_ref[d["bi"] * npair + d["p"]] = (
                d["state"] * d["gall"]
                + _dot_tn(d["sv"], jnp.concatenate([d["bg"], d["kg"]], axis=0)))
        for d in chunk:
            o_st = d["rs_out"] + _dot(d["a_r"], d["sv"])
            hout_ref[d["bi"], d["rs"], d["sl"]] = o_st[:C] + o_st[C:]

    inv_n = 1.0 / HEAD_DIM
    for bi in range(nb):
        r, k_eff, v, g = pre[bi]
        out = hout_ref[bi]
        mean = _head_sums(out, bd, 2) * inv_n
        cen = out - mean
        var = _head_sums(cen * cen, bd, 1) * inv_n
        normed = cen * lax.rsqrt(var + RW_LNX_EPS) * lw_ref[...] + lb_ref[...]
        bonus = _head_sums(r * k_eff * rk_ref[...], bd, 2) * v
        o_ref[bi] = ((normed + bonus) * g).astype(o_ref.dtype)


def _rwkv(u_rw, p, layer, B, S, vres=None):
    T = u_rw.shape[0]
    C = RW_CHUNK
    nb = RW_BATCH
    rows = RW_SUBCHUNKS * C
    nc = S // rows
    has_vres = vres is not None
    blk = lambda w: pl.BlockSpec((nb, rows, w), lambda b, c: (b, c, 0))
    names = ("mu", "w0", "w_up", "a0", "a_up", "g_up", "k_k", "k_a", "r_k", "lnx_w", "lnx_b")
    in_specs = [blk(RW_COLS)]
    args = [u_rw.reshape(B, S, RW_COLS)]
    if has_vres:
        in_specs += [blk(LORA_PAD), blk(BRANCH)]
        args += [vres["vm"].reshape(B, S, LORA_PAD), vres["v_first"].reshape(B, S, BRANCH)]
    in_specs += [_layer_spec(p[n].shape[1:], layer) for n in names]
    in_specs.append(_const_spec((MXU_TILE, MXU_TILE)))
    args += [p[n] for n in names] + [p["bd"]]
    scratch = [pltpu.VMEM((nb * (BRANCH // LANES), 2 * C, LANES), F32),
               pltpu.VMEM((nb, 8, RW_COLS), F32)]
    if has_vres:
        vnames = ("mu", "v0", "up")
        in_specs += [_layer_spec(vres[n].shape[1:], layer - 1) for n in vnames]
        args += [vres[n] for n in vnames]
        out_shape = jax.ShapeDtypeStruct((B, S, BRANCH), BF16)
        out_specs = blk(BRANCH)
        scratch.append(pltpu.VMEM((nb, 8, LORA_PAD), F32))
    else:
        out_shape = (jax.ShapeDtypeStruct((B, S, BRANCH), BF16),
                     jax.ShapeDtypeStruct((B, S, BRANCH), F32))
        out_specs = (blk(BRANCH), blk(BRANCH))
    scratch.append(pltpu.VMEM((nb, rows, BRANCH), F32))
    outs = pl.pallas_call(
        functools.partial(_rwkv_kernel, has_vres, nb),
        out_shape=out_shape,
        grid=(B // nb, nc),
        in_specs=in_specs,
        out_specs=out_specs,
        scratch_shapes=scratch,
        compiler_params=_cparams("parallel", "arbitrary"),
        name="rwkv",
    )(*args)
    if has_vres:
        return outs.reshape(T, BRANCH)
    return outs[0].reshape(T, BRANCH), outs[1].reshape(T, BRANCH)


def _moba_steps(q_ref, k_ref, v_ref, o_ref, kmean_ref, vt_ref):
    S = q_ref.shape[0]
    nb = S // MB_BLOCK
    tq = MB_BLOCK
    pair = LANES // HEAD_DIM
    tiles = [slice(t * LANES, (t + 1) * LANES) for t in range(q_ref.shape[1] // LANES)]
    heads = [(t, hh) for t in range(len(tiles)) for hh in range(pair)]
    for n in range(nb):
        blk = k_ref[n * MB_BLOCK:(n + 1) * MB_BLOCK, :].astype(F32)
        kmean_ref[n:n + 1, :] = jnp.mean(blk, axis=0, keepdims=True)
    for t, cols in enumerate(tiles):
        vt_ref[t, 0:LANES, :] = v_ref[:, cols].astype(F32).T.astype(BF16)
        vt_ref[t, LANES:, :] = jnp.ones((vt_ref.shape[1] - LANES, S), BF16)
    lane = lax.broadcasted_iota(jnp.int32, (1, LANES), 1)
    head_lanes = [(lane // HEAD_DIM) == hh for hh in range(pair)]

    blk_id = lax.broadcasted_iota(jnp.int32, (nb, S), 0)
    own = lax.broadcasted_iota(jnp.int32, (nb, S), 1) // MB_BLOCK
    valid = blk_id < own
    sels = []
    for t, hh in heads:
        kmean = kmean_ref[:, tiles[t]]
        kmean_rep = jnp.concatenate(
            [kmean] + [jnp.broadcast_to(kmean[m:m + 1, :], (nb, LANES)) for m in range(nb - 1)], axis=0)
        pieces = _split_bf16(jnp.where(head_lanes[hh], kmean_rep, 0.0), 3)
        gates = sum(_dot_nt(piece, q_ref[:, tiles[t]]) for piece in pieces)
        gate = gates[:nb]
        cnt = jnp.zeros((nb, S), jnp.int32)
        for m in range(nb - 1):
            gm = gates[nb * (m + 1):nb * (m + 2)]
            beats = (own > m) & ((gm > gate) | ((gm == gate) & (m < blk_id)))
            cnt = cnt + beats.astype(jnp.int32)
        sels.append((valid & (cnt < MB_TOPK)).astype(F32))

    ki = lax.broadcasted_iota(jnp.int32, (tq, tq), 0)
    qi = lax.broadcasted_iota(jnp.int32, (tq, tq), 1)
    causal = ki <= qi

    def scores(i):
        out = []
        for t, hh in heads:
            q_t = q_ref[i * tq:(i + 1) * tq, tiles[t]]
            k_all = k_ref[0:(i + 1) * MB_BLOCK, tiles[t]]
            out.append(_dot_nt(k_all, jnp.where(head_lanes[hh], q_t, jnp.zeros_like(q_t))))
        return out

    state = {"s": scores(0)}

    def tile(i):
        s_cur = state["s"]
        if i + 1 < nb:
            state["s"] = scores(i + 1)
        probs = []
        for hh in range(len(heads)):
            s = s_cur[hh]
            s_own = jnp.where(causal, s[i * MB_BLOCK:, :], MASK_VALUE)
            m_col = jnp.max(s_own, axis=0, keepdims=True)
            keeps = []
            for j in range(i):
                keep = sels[hh][j:j + 1, i * tq:(i + 1) * tq] > 0.5
                blk_max = jnp.max(s[j * MB_BLOCK:(j + 1) * MB_BLOCK, :], axis=0, keepdims=True)
                m_col = jnp.maximum(m_col, jnp.where(keep, blk_max, MASK_VALUE))
                keeps.append(keep)
            parts = [jnp.exp2(s[j * MB_BLOCK:(j + 1) * MB_BLOCK, :] - jnp.where(keeps[j], m_col, -MASK_VALUE))
                     for j in range(i)]
            parts.append(jnp.exp2(s_own - m_col))
            p = parts[0] if len(parts) == 1 else jnp.concatenate(parts, axis=0)
            probs.append(p.astype(BF16))
        ot = [_dot(vt_ref[t, :, 0:(i + 1) * MB_BLOCK], p) for (t, _), p in zip(heads, probs)]
        ot = [o[:LANES] / o[LANES:LANES + 1] for o in ot]
        for t, cols in enumerate(tiles):
            o_t = jnp.concatenate([ot[pair * t][:HEAD_DIM], ot[pair * t + 1][HEAD_DIM:]], axis=0).T
            o_ref[i * tq:(i + 1) * tq, cols] = o_t.astype(o_ref.dtype)

    return [functools.partial(tile, i) for i in range(nb)]


def _swa_steps(q_ref, k_ref, v_ref, sink_ref, o_ref, vt_ref):
    S = q_ref.shape[0]
    W = SW_WINDOW
    pair = LANES // HEAD_DIM
    tiles = [slice(t * LANES, (t + 1) * LANES) for t in range(q_ref.shape[1] // LANES)]
    vt_ref[0:LANES, :] = v_ref[...].astype(F32).T.astype(BF16)
    vt_ref[LANES:, :] = jnp.ones((vt_ref.shape[0] - LANES, S), BF16)
    lane = lax.broadcasted_iota(jnp.int32, (1, LANES), 1)
    dist = (lax.broadcasted_iota(jnp.int32, (2 * W, W), 1) + W
            - lax.broadcasted_iota(jnp.int32, (2 * W, W), 0))
    band = (dist >= 0) & (dist < W)
    causal = (lax.broadcasted_iota(jnp.int32, (W, W), 0)
              <= lax.broadcasted_iota(jnp.int32, (W, W), 1))
    head_lanes = [(lane // HEAD_DIM) == hh for hh in range(pair)]
    sinks = [[sink_ref[t, hh:hh + 1, 0:1] * LOG2E for hh in range(pair)]
             for t in range(len(tiles))]
    group = SWA_GROUP

    def scores(g0):
        work = []
        for n in range(g0, g0 + group):
            k0 = max(n - 1, 0) * W
            k1 = (n + 1) * W
            mask = causal if n == 0 else band
            k_w = k_ref[k0:k1, :]
            for t, cols in enumerate(tiles):
                q_t = q_ref[n * W:(n + 1) * W, cols]
                for hh in range(pair):
                    q_h = jnp.where(head_lanes[hh], q_t, jnp.zeros_like(q_t))
                    work.append(dict(n=n, t=t, hh=hh, k0=k0, k1=k1, mask=mask, s=_dot_nt(k_w, q_h)))
        return work

    state = {"work": scores(0)}

    def run_group(g0):
        work = state["work"]
        if g0 + group < S // W:
            state["work"] = scores(g0 + group)
        for d in work:
            s = jnp.where(d["mask"], d["s"], MASK_VALUE)
            sink = sinks[d["t"]][d["hh"]]
            m_col = jnp.maximum(jnp.max(s, axis=0, keepdims=True), sink)
            d["p"] = jnp.exp2(s - m_col).astype(BF16)
            d["sink_term"] = jnp.exp2(sink - m_col)
        for d in work:
            ot = _dot(vt_ref[:, d["k0"]:d["k1"]], d["p"])
            d["ot"] = ot[:LANES] / (ot[LANES:LANES + 1] + d["sink_term"])
        for t, cols in enumerate(tiles):
            mine = [d for d in work if d["t"] == t]
            o_t = jnp.concatenate(
                [jnp.concatenate([a["ot"][:HEAD_DIM], b["ot"][HEAD_DIM:]], axis=0)
                 for a, b in zip(mine[0::2], mine[1::2])], axis=1)
            o_ref[g0 * W:(g0 + group) * W, cols] = o_t.T.astype(o_ref.dtype)

    return [functools.partial(run_group, g0) for g0 in range(0, S // W, group)]


def _attn_kernel(mq_ref, mk_ref, mv_ref, sq_ref, sk_ref, sv_ref, sink_ref, omb_ref, osw_ref,
                 kmean_ref, mvt_ref, svt_ref):
    moba = _moba_steps(mq_ref, mk_ref, mv_ref, omb_ref, kmean_ref, mvt_ref)
    swa = _swa_steps(sq_ref, sk_ref, sv_ref, sink_ref, osw_ref, svt_ref)
    per = len(moba) // len(swa)
    for g, swa_group in enumerate(swa):
        for moba_tile in moba[g * per:(g + 1) * per]:
            moba_tile()
        swa_group()


def _attn(mq, mk, mv, sq, sk, sv, sinks, B, S, layer):
    T = mq.shape[0]
    nt = ATTN_TILES
    width = nt * LANES
    spec = pl.BlockSpec((S, width), lambda b, p: (b, p))
    kv_spec = pl.BlockSpec((S, LANES), lambda b, p: (b, 0))
    return pl.pallas_call(
        _attn_kernel,
        out_shape=(jax.ShapeDtypeStruct((T, BRANCH), BF16),) * 2,
        grid=(B, BRANCH // width),
        in_specs=[spec, spec, spec, spec, kv_spec, kv_spec,
                  pl.BlockSpec((None, nt, 2, LANES), lambda b, p: (layer, p, 0, 0))],
        out_specs=(spec, spec),
        scratch_shapes=[pltpu.VMEM((S // MB_BLOCK, width), F32),
                        pltpu.VMEM((nt, LANES + BF16_SUBLANES, S), BF16),
                        pltpu.VMEM((LANES + BF16_SUBLANES, S), BF16)],
        compiler_params=_cparams("parallel", "parallel"),
        name="attn",
    )(mq, mk, mv, sq, sk, sv, sinks)


def _mix_out_kernel(x_ref, orw_ref, omb_ref, osw_ref, gate_ref, wb_ref, wo_ref, g_ref, out_ref):
    tm = x_ref.shape[0]
    subs = [slice(r, r + MIX_OUT_SUB) for r in range(0, tm, MIX_OUT_SUB)]

    def gated(rows):
        y = None
        for n, o_ref in enumerate((orw_ref, omb_ref, osw_ref)):
            t = gate_ref[rows, n * D_MODEL:(n + 1) * D_MODEL] * _dot(o_ref[rows, :], wb_ref[n])
            y = t if y is None else y + t
        return y.astype(BF16)

    y_next = gated(subs[0])
    for i, rows in enumerate(subs):
        y = y_next
        if i + 1 < len(subs):
            y_next = gated(subs[i + 1])
        out_ref[rows, :] = x_ref[rows, :] + _rms(_dot(y, wo_ref[...]), g_ref[...])


def _mix_out(x2, o_rw, o_mb, o_sw, gates, w_branch, w_out, g, layer):
    T = x2.shape[0]
    tm = 1024
    row = lambda i: (i, 0)
    br = pl.BlockSpec((tm, BRANCH), row)
    return pl.pallas_call(
        _mix_out_kernel,
        out_shape=jax.ShapeDtypeStruct((T, D_MODEL), F32),
        grid=(T // tm,),
        in_specs=[pl.BlockSpec((tm, D_MODEL), row), br, br, br,
                  pl.BlockSpec((tm, 3 * D_MODEL), row),
                  _layer_spec(w_branch.shape[1:], layer), _layer_spec(w_out.shape[1:], layer),
                  _layer_spec((1, D_MODEL), layer)],
        out_specs=pl.BlockSpec((tm, D_MODEL), row),
        compiler_params=_cparams("parallel"),
        name="mix_out",
    )(x2, o_rw, o_mb, o_sw, gates, w_branch, w_out, g)


def _mem_kv_kernel(m_ref, g_ref, wk_ref, wv_ref, k_ref, vt_ref):
    m = _rms(m_ref[...], g_ref[...]).astype(BF16)
    k_ref[...] = _dot(m, wk_ref[...]).astype(k_ref.dtype)
    vt_ref[...] = _dot(m, wv_ref[...]).T.astype(vt_ref.dtype)


def _mem_kv(mem2, g, wk, wv, M, layer):
    R = mem2.shape[0]
    row = lambda i: (i, 0)
    return pl.pallas_call(
        _mem_kv_kernel,
        out_shape=(jax.ShapeDtypeStruct((R, XA_WIDTH), BF16),
                   jax.ShapeDtypeStruct((R // M, XA_WIDTH, M), BF16)),
        grid=(R // M,),
        in_specs=[pl.BlockSpec((M, D_MODEL), row), _layer_spec((1, D_MODEL), layer),
                  _layer_spec(wk.shape[1:], layer), _layer_spec(wv.shape[1:], layer)],
        out_specs=(pl.BlockSpec((M, XA_WIDTH), row),
                   pl.BlockSpec((None, XA_WIDTH, M), lambda i: (i, 0, 0))),
        compiler_params=_cparams("parallel"),
        name="mem_kv",
    )(mem2, g, wk, wv)


def _xattn_kernel(x_ref, k_ref, vt_ref, gpre_ref, wq_ref, wo_ref, gpost_ref, out_ref):
    tm = x_ref.shape[0]
    groups = [slice(r, r + XATTN_SUB) for r in range(0, tm, XATTN_SUB)]
    heads = [slice(hd * XA_HEAD_DIM, (hd + 1) * XA_HEAD_DIM) for hd in range(XA_HEADS)]
    qs = []
    for rows in groups:
        h = _rms(x_ref[rows, :], gpre_ref[...]).astype(BF16)
        qs.append((_dot(h, wq_ref[...]) * (LOG2E * XA_HEAD_DIM ** -0.5)).astype(BF16))
    scores = [[_dot_nt(k_ref[:, sl], q[:, sl]) for sl in heads] for q in qs]
    ones = jnp.ones((BF16_SUBLANES, vt_ref.shape[1]), BF16)
    vts = [jnp.concatenate([vt_ref[sl, :], ones], axis=0) for sl in heads]
    outs = []
    for grp in scores:
        probs = [jnp.exp2(s - jnp.max(s, axis=0, keepdims=True)).astype(BF16) for s in grp]
        pv = [_dot(vt, p) for vt, p in zip(vts, probs)]
        o_t = jnp.concatenate([o[:XA_HEAD_DIM] / o[XA_HEAD_DIM:XA_HEAD_DIM + 1] for o in pv], axis=0)
        outs.append(o_t.T.astype(BF16))
    ys = [_dot(o, wo_ref[...]) for o in outs]
    for rows, y in zip(groups, ys):
        out_ref[rows, :] = x_ref[rows, :] + _rms(y, gpost_ref[...])


def _xattn(x2, k_mem, vt_mem, g_pre, wq, wo, g_post, B, S, M, layer):
    T = x2.shape[0]
    tm = 1024
    nt = S // tm
    row = lambda b, i: (b * nt + i, 0)
    return pl.pallas_call(
        _xattn_kernel,
        out_shape=jax.ShapeDtypeStruct((T, D_MODEL), F32),
        grid=(B, nt),
        in_specs=[pl.BlockSpec((tm, D_MODEL), row),
                  pl.BlockSpec((M, XA_WIDTH), lambda b, i: (b, 0)),
                  pl.BlockSpec((None, XA_WIDTH, M), lambda b, i: (b, 0, 0)),
                  _layer_spec((1, D_MODEL), layer), _layer_spec(wq.shape[1:], layer),
                  _layer_spec(wo.shape[1:], layer), _layer_spec((1, D_MODEL), layer)],
        out_specs=pl.BlockSpec((tm, D_MODEL), row),
        compiler_params=_cparams("parallel", "parallel"),
        name="xattn",
    )(x2, k_mem, vt_mem, g_pre, wq, wo, g_post)


def _mlp_kernel(x_ref, gpre_ref, wup_ref, wdn_ref, gpost_ref, out_ref, h_ref, acc_ref):
    j = pl.program_id(1)

    @pl.when(j == 0)
    def _():
        h_ref[...] = _rms(x_ref[...], gpre_ref[...]).astype(BF16)
        acc_ref[...] = jnp.zeros_like(acc_ref)

    h = h_ref[...]
    n_sub = wup_ref.shape[1] // MLP_SUB

    def up(c):
        return _dot(h, wup_ref[:, c * MLP_SUB:(c + 1) * MLP_SUB])

    a_next = up(0)
    acc = None
    for c in range(n_sub):
        a = jnp.maximum(a_next, 0.0)
        if c + 1 < n_sub:
            a_next = up(c + 1)
        t = _dot((a * a).astype(BF16), wdn_ref[c * MLP_SUB:(c + 1) * MLP_SUB, :])
        acc = t if acc is None else acc + t
    acc_ref[...] += acc

    @pl.when(j == pl.num_programs(1) - 1)
    def _():
        out_ref[...] = x_ref[...] + _rms(acc_ref[...], gpost_ref[...])


def _mlp(x2, g_pre, w_up, w_down, g_post, layer):
    T = x2.shape[0]
    tm = MLP_ROWS
    tf = MLP_COLS
    row = lambda i, j: (i, 0)
    return pl.pallas_call(
        _mlp_kernel,
        out_shape=jax.ShapeDtypeStruct((T, D_MODEL), F32),
        grid=(T // tm, D_FF // tf),
        in_specs=[pl.BlockSpec((tm, D_MODEL), row), _layer_spec((1, D_MODEL), layer),
                  pl.BlockSpec((None, D_MODEL, tf), lambda i, j: (layer, 0, j)),
                  pl.BlockSpec((None, tf, D_MODEL), lambda i, j: (layer, j, 0)),
                  _layer_spec((1, D_MODEL), layer)],
        out_specs=pl.BlockSpec((tm, D_MODEL), row),
        scratch_shapes=[pltpu.VMEM((tm, D_MODEL), BF16), pltpu.VMEM((tm, D_MODEL), F32)],
        compiler_params=_cparams("parallel", "arbitrary"),
        name="mlp",
    )(x2, g_pre, w_up, w_down, g_post)


def kernel(x, mem, positions, norm_mix_pre, norm_mix_post, norm_xattn_pre, norm_xattn_post, norm_mem, norm_mlp_pre, norm_mlp_post, w_in, rw_mu, rw_w0, rw_w_up, rw_a0, rw_a_up, rw_g_up, rw_k_k, rw_k_a, rw_r_k, rw_lnx_w, rw_lnx_b, rw_vres_down, rw_vres_mu, rw_v0, rw_vres_up, sw_sinks, w_branch, w_out, w_xq, w_xk, w_xv, w_xo, w_up, w_down):
    B, S, D = x.shape
    M = mem.shape[1]
    depth = w_in.shape[0]
    T = B * S
    x2 = x.reshape(T, D)
    mem2 = mem.reshape(B * M, D)
    cos_t, sin_t = _rope_tables(positions)

    rows = lambda t: t.reshape(t.shape[0], 1, -1)
    bf = lambda t: t.astype(BF16)
    lora_in = rw_w_up.shape[1]
    per_kv = N_HEADS // SW_KV_HEADS
    head_of_col = jnp.arange(MXU_TILE) // HEAD_DIM
    rw_p = dict(
        mu=rows(rw_mu), w0=rows(rw_w0), a0=rows(rw_a0), k_k=rows(rw_k_k), k_a=rows(rw_k_a),
        r_k=rows(rw_r_k), lnx_w=rows(rw_lnx_w), lnx_b=rows(rw_lnx_b),
        w_up=bf(jnp.pad(rw_w_up, ((0, 0), (0, LORA_PAD - lora_in), (0, 0)))),
        a_up=bf(jnp.pad(rw_a_up, ((0, 0), (lora_in, LORA_PAD - lora_in - rw_a_up.shape[1]), (0, 0)))),
        g_up=bf(rw_g_up),
        bd=(head_of_col[:, None] == head_of_col[None, :]).astype(BF16))
    mv_pad = LORA_PAD - rw_vres_down.shape[2]
    w_vm = bf(jnp.pad(rw_vres_down, ((0, 0), (0, 0), (0, mv_pad))))
    vres_p = dict(mu=rows(jnp.pad(rw_vres_mu, ((0, 0), (0, mv_pad)))), v0=rows(rw_v0),
                  up=bf(jnp.pad(rw_vres_up, ((0, 0), (0, mv_pad), (0, 0)))))
    sinks = jnp.broadcast_to(
        sw_sinks.reshape(depth, SW_KV_HEADS, per_kv).transpose(0, 2, 1)[..., None].astype(F32),
        (depth, per_kv, SW_KV_HEADS, LANES))
    wb_sw = w_branch[:, 2].reshape(depth, SW_KV_HEADS, per_kv, HEAD_DIM, D)
    wb_sw = wb_sw.transpose(0, 2, 1, 3, 4).reshape(depth, 1, BRANCH, D)
    wb = bf(jnp.concatenate([w_branch[:, :2], wb_sw], axis=1))
    w_in_b, w_out_b, w_up_b, w_down_b = bf(w_in), bf(w_out), bf(w_up), bf(w_down)
    w_xq_b, w_xk_b, w_xv_b, w_xo_b = bf(w_xq), bf(w_xk), bf(w_xv), bf(w_xo)
    g_mix_pre, g_mix_post = rows(norm_mix_pre), rows(norm_mix_post)
    g_xa_pre, g_xa_post, g_mem = rows(norm_xattn_pre), rows(norm_xattn_post), rows(norm_mem)
    g_mlp_pre, g_mlp_post = rows(norm_mlp_pre), rows(norm_mlp_post)

    v_first = None
    for l in range(depth):
        outs = _mix_in(x2, g_mix_pre, w_in_b, w_vm if l > 0 else None, cos_t, sin_t, l)
        if l > 0:
            u_rw, vm, mq, mk, mv, sq, sk, sv, gates = outs
            o_rw = _rwkv(u_rw, rw_p, l, B, S, dict(vres_p, vm=vm, v_first=v_first))
        else:
            u_rw, mq, mk, mv, sq, sk, sv, gates = outs
            o_rw, v_first = _rwkv(u_rw, rw_p, l, B, S)
        o_mb, o_sw = _attn(mq, mk, mv, sq, sk, sv, sinks, B, S, l)
        x2 = _mix_out(x2, o_rw, o_mb, o_sw, gates, wb, w_out_b, g_mix_post, l)
        k_mem, vt_mem = _mem_kv(mem2, g_mem, w_xk_b, w_xv_b, M, l)
        x2 = _xattn(x2, k_mem, vt_mem, g_xa_pre, w_xq_b, w_xo_b, g_xa_post, B, S, M, l)
        x2 = _mlp(x2, g_mlp_pre, w_up_b, w_down_b, g_mlp_post, l)
    return x2.reshape(B, S, D)
```

```python
import functools

import jax
import jax.numpy as jnp
from jax import lax
from jax.experimental import pallas as pl
from jax.experimental.pallas import tpu as pltpu

F32 = jnp.float32
BF16 = jnp.bfloat16

D_MODEL = 1024
HEAD_DIM = 64
BRANCH = 512
N_HEADS = BRANCH // HEAD_DIM
RW_COLS = 3 * BRANCH + 64 + 64 + 128
LORA_PAD = 128
MB_BLOCK = 256
MB_TOPK = 3
SW_WINDOW = 128
SW_KV_HEADS = 2
XA_HEADS = 4
XA_HEAD_DIM = 128
XA_WIDTH = XA_HEADS * XA_HEAD_DIM
D_FF = 4 * D_MODEL
ROPE_THETA = 10000.0
NORM_EPS = 1e-6
RW_LNX_EPS = 1e-5 * HEAD_DIM
MASK_VALUE = -1e30
LOG2E = 1.4426950408889634

LANES = 128
MXU_TILE = 256
BF16_SUBLANES = 16
VMEM_LIMIT = 56 * 1024 * 1024

RW_CHUNK = 64
RW_SUBCHUNKS = 2
RW_BATCH = 4
ATTN_TILES = 2
SWA_GROUP = 4
ROPE_ROWS = 2048
MIX_IN_ROWS = 512
XATTN_ROWS = 1024
XATTN_SUB = 256
MIX_OUT_ROWS = 1024
MIX_OUT_SUB = 512
MLP_ROWS = 1024
MLP_COLS = 2048
MLP_SUB = 512


def _cparams(*sem):
    return pltpu.CompilerParams(dimension_semantics=sem, vmem_limit_bytes=VMEM_LIMIT)


def _dot(a, b):
    return jnp.dot(a, b, preferred_element_type=F32)


def _dot_nt(a, b):
    return lax.dot_general(a, b, (((1,), (1,)), ((), ())), preferred_element_type=F32)


def _dot_tn(a, b):
    return lax.dot_general(a, b, (((0,), (0,)), ((), ())), preferred_element_type=F32)


def _rms(xf, g):
    ms = jnp.mean(xf * xf, axis=-1, keepdims=True)
    return xf * lax.rsqrt(ms + NORM_EPS) * g


def _const_spec(shape):
    nd = len(shape)
    return pl.BlockSpec(shape, lambda *_: (0,) * nd, pipeline_mode=pl.Buffered(1))


def _layer_spec(shape, layer):
    nd = len(shape)
    return pl.BlockSpec((None,) + tuple(shape), lambda *_: (layer,) + (0,) * nd,
                        pipeline_mode=pl.Buffered(1))


def _rope_kernel(pos_ref, invf_ref, cos_ref, sin_ref):
    ang = invf_ref[...] * pos_ref[...].astype(F32)
    c = jnp.cos(ang)
    s = jnp.sin(ang)
    reps = LANES // HEAD_DIM
    cos_ref[...] = jnp.concatenate([c, c] * reps, axis=0).T
    sin_ref[...] = jnp.concatenate([-s, s] * reps, axis=0).T


def _rope_tables(positions):
    T = positions.size
    tm = ROPE_ROWS
    inv_freq = 1.0 / (ROPE_THETA ** (jnp.arange(0, HEAD_DIM, 2, dtype=F32) / HEAD_DIM))
    return pl.pallas_call(
        _rope_kernel,
        out_shape=(jax.ShapeDtypeStruct((T, LANES), F32),) * 2,
        grid=(T // tm,),
        in_specs=[pl.BlockSpec((1, tm), lambda i: (0, i)), _const_spec((HEAD_DIM // 2, 1))],
        out_specs=(pl.BlockSpec((tm, LANES), lambda i: (i, 0)),) * 2,
        compiler_params=_cparams("parallel"),
        name="rope_tables",
    )(positions.reshape(1, T), inv_freq[:, None])


def _rope_tile(t, cos, sin, lane_lt_half):
    fwd = pltpu.roll(t, LANES - HEAD_DIM // 2, axis=1)
    bwd = pltpu.roll(t, HEAD_DIM // 2, axis=1)
    partner = jnp.where(lane_lt_half, fwd, bwd)
    return t * cos + partner * sin


def _mix_in_kernel(has_vm, x_ref, g_ref, w_ref, *rest):
    if has_vm:
        (wvm_ref, cos_ref, sin_ref,
         urw_ref, vm_ref, mq_ref, mk_ref, mv_ref, sq_ref, sk_ref, sv_ref, gate_ref) = rest
    else:
        cos_ref, sin_ref, urw_ref, mq_ref, mk_ref, mv_ref, sq_ref, sk_ref, sv_ref, gate_ref = rest
    h = _rms(x_ref[...], g_ref[...]).astype(BF16)
    cos = cos_ref[...]
    sin = sin_ref[...]
    lane = lax.broadcasted_iota(jnp.int32, cos.shape, 1)
    lt_half = (lane % HEAD_DIM) < (HEAD_DIM // 2)
    first_head = lane < HEAD_DIM
    q_scale = LOG2E * HEAD_DIM ** -0.5

    col = [0]

    def seg(width):
        a = col[0]
        col[0] += width
        return _dot(h, w_ref[:, a:a + width])

    def rope_chunks(width, scale):
        acc = seg(width)
        return [_rope_tile(acc[:, c * LANES:(c + 1) * LANES], cos, sin, lt_half) * scale
                for c in range(width // LANES)]

    def store_chunks(ref, chunks):
        for c, t in enumerate(chunks):
            ref[:, c * LANES:(c + 1) * LANES] = t.astype(ref.dtype)

    urw_ref[...] = seg(RW_COLS).astype(urw_ref.dtype)
    if has_vm:
        vm_ref[...] = _dot(h, wvm_ref[...])
    store_chunks(mq_ref, rope_chunks(BRANCH, q_scale))
    store_chunks(mk_ref, rope_chunks(BRANCH, 1.0))
    mv_ref[...] = seg(BRANCH).astype(mv_ref.dtype)
    nat = rope_chunks(BRANCH, q_scale)
    half_tiles = len(nat) // 2
    paired = []
    for p in range(len(nat)):
        a, b = nat[p // 2], nat[p // 2 + half_tiles]
        if p % 2 == 0:
            paired.append(jnp.where(first_head, a, pltpu.roll(b, HEAD_DIM, axis=1)))
        else:
            paired.append(jnp.where(first_head, pltpu.roll(a, HEAD_DIM, axis=1), b))
    store_chunks(sq_ref, paired)
    store_chunks(sk_ref, rope_chunks(SW_KV_HEADS * HEAD_DIM, 1.0))
    sv_ref[...] = seg(SW_KV_HEADS * HEAD_DIM).astype(sv_ref.dtype)
    for n in range(3):
        gate_ref[:, n * D_MODEL:(n + 1) * D_MODEL] = jax.nn.sigmoid(seg(D_MODEL)).astype(gate_ref.dtype)


def _mix_in(x2, g, w_in, w_vm, cos_t, sin_t, layer):
    T = x2.shape[0]
    tm = MIX_IN_ROWS
    has_vm = w_vm is not None
    kv_w = SW_KV_HEADS * HEAD_DIM
    widths = [(RW_COLS, BF16)]
    if has_vm:
        widths.append((LORA_PAD, F32))
    widths += [(BRANCH, BF16), (BRANCH, BF16), (BRANCH, BF16), (BRANCH, BF16),
               (kv_w, BF16), (kv_w, BF16), (3 * D_MODEL, BF16)]
    assert sum(w for w, _ in widths) - (LORA_PAD if has_vm else 0) == w_in.shape[2]
    row = lambda i: (i, 0)
    in_specs = [pl.BlockSpec((tm, D_MODEL), row), _layer_spec((1, D_MODEL), layer),
                _layer_spec(w_in.shape[1:], layer)]
    args = [x2, g, w_in]
    if has_vm:
        in_specs.append(_layer_spec(w_vm.shape[1:], layer - 1))
        args.append(w_vm)
    in_specs += [pl.BlockSpec((tm, LANES), row), pl.BlockSpec((tm, LANES), row)]
    args += [cos_t, sin_t]
    return pl.pallas_call(
        functools.partial(_mix_in_kernel, has_vm),
        out_shape=tuple(jax.ShapeDtypeStruct((T, w), dt) for w, dt in widths),
        grid=(T // tm,),
        in_specs=in_specs,
        out_specs=tuple(pl.BlockSpec((tm, w), row) for w, _ in widths),
        compiler_params=_cparams("parallel"),
        name="mix_in",
    )(*args)


def _shift_prev(cur, carry_ref, bi):
    rolled = pltpu.roll(cur, 1, axis=0)
    row = lax.broadcasted_iota(jnp.int32, cur.shape, 0)
    prev = jnp.where(row == 0, carry_ref[bi, 0:1, :], rolled)
    carry_ref[bi, 0:1, :] = cur[cur.shape[0] - 1:, :]
    return prev


def _split_bf16(x, terms):
    pieces = []
    for _ in range(terms):
        hi = x.astype(BF16)
        pieces.append(hi)
        x = x - hi.astype(F32)
    return pieces


def _head_sums(x, bd, terms):
    w = bd.shape[0]
    halves = []
    for c in range(x.shape[1] // w):
        pieces = _split_bf16(x[:, c * w:(c + 1) * w], terms)
        halves.append(sum(_dot(piece, bd) for piece in pieces))
    return jnp.concatenate(halves, axis=1)


def _dot_split_rhs(a, b, terms):
    return sum(_dot(a, piece) for piece in _split_bf16(b, terms))


def _rwkv_kernel(has_vres, nb, *refs):
    if has_vres:
        (u_ref, vm_ref, vf_ref, mu_ref, w0_ref, wup_ref, a0_ref, aup_ref, gup_ref, kk_ref,
         ka_ref, rk_ref, lw_ref, lb_ref, bd_ref, vmu_ref, v0_ref, vup_ref,
         o_ref, s_ref, cu_ref, cvm_ref, hout_ref) = refs
        vout_ref = None
    else:
        (u_ref, mu_ref, w0_ref, wup_ref, a0_ref, aup_ref, gup_ref, kk_ref,
         ka_ref, rk_ref, lw_ref, lb_ref, bd_ref,
         o_ref, vout_ref, s_ref, cu_ref, hout_ref) = refs
    C = RW_CHUNK
    P2 = 2 * C
    npair = BRANCH // LANES
    n_sub = u_ref.shape[1] // C

    @pl.when(pl.program_id(1) == 0)
    def _():
        s_ref[...] = jnp.zeros_like(s_ref)
        cu_ref[...] = jnp.zeros_like(cu_ref)
        if has_vres:
            cvm_ref[...] = jnp.zeros_like(cvm_ref)

    bd = bd_ref[...]
    ti = lax.broadcasted_iota(jnp.int32, (C, C), 0)
    si = lax.broadcasted_iota(jnp.int32, (C, C), 1)
    tri = (si <= ti).astype(BF16)
    lane = lax.broadcasted_iota(jnp.int32, (1, LANES), 1)
    first_head = lane < HEAD_DIM
    pr = lax.broadcasted_iota(jnp.int32, (P2, P2), 0)
    pc = lax.broadcasted_iota(jnp.int32, (P2, P2), 1)
    same_head = (pr // C) == (pc // C)
    strict = same_head & ((pc % C) < (pr % C))
    incl = same_head & ((pc % C) <= (pr % C))
    eye = (pr == pc).astype(F32)

    def stack(t):
        t = t.astype(BF16)
        zero = jnp.zeros_like(t)
        return jnp.concatenate([jnp.where(first_head, t, zero), jnp.where(first_head, zero, t)], axis=0)

    pre = []
    inst = []
    for bi in range(nb):
        u = u_ref[bi].astype(F32)
        ul = u + mu_ref[...] * (_shift_prev(u, cu_ref, bi) - u)
        r = ul[:, 0:BRANCH]
        k = ul[:, BRANCH:2 * BRANCH]
        v = ul[:, 2 * BRANCH:3 * BRANCH]
        x_wa = ul[:, 3 * BRANCH:3 * BRANCH + LANES]
        x_g = ul[:, 3 * BRANCH + LANES:]

        z = w0_ref[...] + _dot(jnp.tanh(x_wa).astype(BF16), wup_ref[...])
        a = jax.nn.sigmoid(a0_ref[...] + _dot(x_wa.astype(BF16), aup_ref[...]))
        g = _dot(jax.nn.sigmoid(x_g).astype(BF16), gup_ref[...])
        softplus = jnp.maximum(-z, 0.0) + jnp.log1p(jnp.exp(-jnp.abs(z)))
        ell = -jnp.exp(-softplus - 0.5)

        if has_vres:
            vm = vm_ref[bi]
            vml = vm + vmu_ref[...] * (_shift_prev(vm, cvm_ref, bi) - vm)
            vgate = jax.nn.sigmoid(v0_ref[...] + _dot(vml.astype(BF16), vup_ref[...]))
            v = v + (vf_ref[bi] - v) * vgate
        else:
            vout_ref[bi] = v

        kk = k * kk_ref[...]
        kk = kk / jnp.maximum(jnp.sqrt(_head_sums(kk * kk, bd, 1)), 1e-12)
        k_eff = k * (1.0 + (a - 1.0) * ka_ref[...])
        b = kk * a
        pre.append((r, k_eff, v, g))
        for ci in range(n_sub):
            rs = slice(ci * C, (ci + 1) * C)
            ell_c = ell[rs]
            cum = _dot_split_rhs(tri, ell_c, 3)
            cum_last = cum[C - 1:, :]
            g_inv = jnp.exp(-cum)
            g_tail = jnp.exp(cum_last - cum)
            kap_t = kk[rs] * jnp.exp(cum - ell_c)
            r_t = r[rs] * jnp.exp(cum)
            b_t = b[rs] * g_inv
            k_t = k_eff[rs] * g_inv
            b_g = b[rs] * g_tail
            k_g = k_eff[rs] * g_tail
            g_all = jnp.exp(cum_last)
            v_c = v[rs]
            for p in range(npair):
                sl = slice(p * LANES, (p + 1) * LANES)
                inst.append(dict(bi=bi, ci=ci, rs=rs, p=p, sl=sl,
                                 kap=stack(kap_t[:, sl]), r=stack(r_t[:, sl]), b=stack(b_t[:, sl]),
                                 k=stack(k_t[:, sl]), v=stack(v_c[:, sl]), bg=stack(b_g[:, sl]),
                                 kg=stack(k_g[:, sl]), gall=g_all[:, sl]))

    incl2 = jnp.concatenate([incl, incl], axis=1)
    for d in inst:
        aa = _dot_nt(jnp.concatenate([d["kap"], d["r"]], axis=0),
                     jnp.concatenate([d["b"], d["k"]], axis=0))
        n = jnp.where(strict, -aa[:P2, :P2], 0.0)
        d["n"] = n.astype(BF16)
        d["t"] = eye + n
        d["a_ak"] = jnp.where(strict, aa[:P2, P2:], 0.0).astype(BF16)
        d["a_r"] = jnp.where(incl2, aa[P2:, :], 0.0).astype(BF16)
    for d in inst:
        d["akv"] = _dot(d["a_ak"], d["v"]).astype(BF16)
    for d in inst:
        d["pw"] = _dot(d["n"], d["n"]).astype(BF16)
    for _ in range(4):
        for d in inst:
            pt = _dot(d["pw"], jnp.concatenate([d["pw"], d["t"].astype(BF16)], axis=1))
            d["pw"] = pt[:, :P2].astype(BF16)
            d["t"] = d["t"] + pt[:, P2:]
    for d in inst:
        d["t"] = d["t"] + _dot(d["pw"], d["t"].astype(BF16))
    for d in inst:
        ty = _dot(d["t"].astype(BF16), jnp.concatenate([d["kap"], d["akv"]], axis=1))
        d["w_til"], d["u_til"] = ty[:, :LANES].astype(BF16), ty[:, LANES:]
    for ci in range(n_sub):
        chunk = [d for d in inst if d["ci"] == ci]
        for d in chunk:
            d["state"] = s_ref[d["bi"] * npair + d["p"]]
            xr = _dot_nt(jnp.concatenate([d["w_til"], d["r"]], axis=0), d["state"].astype(BF16))
            sa = -(xr[:P2] + d["u_til"])
            d["sv"] = jnp.concatenate([sa.astype(BF16), d["v"]], axis=0)
            d["rs_out"] = xr[P2:]
        for d in chunk:
            s_ref[d["bi"] * npair + d["p"]] = (
                d["state"] * d["gall"]
                + _dot_tn(d["sv"], jnp.concatenate([d["bg"], d["kg"]], axis=0)))
        for d in chunk:
            o_st = d["rs_out"] + _dot(d["a_r"], d["sv"])
            hout_ref[d["bi"], d["rs"], d["sl"]] = o_st[:C] + o_st[C:]

    inv_n = 1.0 / HEAD_DIM
    for bi in range(nb):
        r, k_eff, v, g = pre[bi]
        out = hout_ref[bi]
        mean = _head_sums(out, bd, 2) * inv_n
        cen = out - mean
        var = _head_sums(cen * cen, bd, 1) * inv_n
        normed = cen * lax.rsqrt(var + RW_LNX_EPS) * lw_ref[...] + lb_ref[...]
        bonus = _head_sums(r * k_eff * rk_ref[...], bd, 2) * v
        o_ref[bi] = ((normed + bonus) * g).astype(o_ref.dtype)


def _rwkv(u_rw, p, layer, B, S, vres=None):
    T = u_rw.shape[0]
    C = RW_CHUNK
    nb = RW_BATCH
    rows = RW_SUBCHUNKS * C
    nc = S // rows
    has_vres = vres is not None
    blk = lambda w: pl.BlockSpec((nb, rows, w), lambda b, c: (b, c, 0))
    names = ("mu", "w0", "w_up", "a0", "a_up", "g_up", "k_k", "k_a", "r_k", "lnx_w", "lnx_b")
    in_specs = [blk(RW_COLS)]
    args = [u_rw.reshape(B, S, RW_COLS)]
    if has_vres:
        in_specs += [blk(LORA_PAD), blk(BRANCH)]
        args += [vres["vm"].reshape(B, S, LORA_PAD), vres["v_first"].reshape(B, S, BRANCH)]
    in_specs += [_layer_spec(p[n].shape[1:], layer) for n in names]
    in_specs.append(_const_spec((MXU_TILE, MXU_TILE)))
    args += [p[n] for n in names] + [p["bd"]]
    scratch = [pltpu.VMEM((nb * (BRANCH // LANES), 2 * C, LANES), F32),
               pltpu.VMEM((nb, 8, RW_COLS), F32)]
    if has_vres:
        vnames = ("mu", "v0", "up")
        in_specs += [_layer_spec(vres[n].shape[1:], layer - 1) for n in vnames]
        args += [vres[n] for n in vnames]
        out_shape = jax.ShapeDtypeStruct((B, S, BRANCH), BF16)
        out_specs = blk(BRANCH)
        scratch.append(pltpu.VMEM((nb, 8, LORA_PAD), F32))
    else:
        out_shape = (jax.ShapeDtypeStruct((B, S, BRANCH), BF16),
                     jax.ShapeDtypeStruct((B, S, BRANCH), F32))
        out_specs = (blk(BRANCH), blk(BRANCH))
    scratch.append(pltpu.VMEM((nb, rows, BRANCH), F32))
    outs = pl.pallas_call(
        functools.partial(_rwkv_kernel, has_vres, nb),
        out_shape=out_shape,
        grid=(B // nb, nc),
        in_specs=in_specs,
        out_specs=out_specs,
        scratch_shapes=scratch,
        compiler_params=_cparams("parallel", "arbitrary"),
        name="rwkv",
    )(*args)
    if has_vres:
        return outs.reshape(T, BRANCH)
    return outs[0].reshape(T, BRANCH), outs[1].reshape(T, BRANCH)


def _moba_steps(q_ref, k_ref, v_ref, o_ref, kmean_ref, vt_ref):
    S = q_ref.shape[0]
    nb = S // MB_BLOCK
    tq = MB_BLOCK
    pair = LANES // HEAD_DIM
    tiles = [slice(t * LANES, (t + 1) * LANES) for t in range(q_ref.shape[1] // LANES)]
    heads = [(t, hh) for t in range(len(tiles)) for hh in range(pair)]
    for n in range(nb):
        blk = k_ref[n * MB_BLOCK:(n + 1) * MB_BLOCK, :].astype(F32)
        kmean_ref[n:n + 1, :] = jnp.mean(blk, axis=0, keepdims=True)
    for t, cols in enumerate(tiles):
        vt_ref[t, 0:LANES, :] = v_ref[:, cols].astype(F32).T.astype(BF16)
        vt_ref[t, LANES:, :] = jnp.ones((vt_ref.shape[1] - LANES, S), BF16)
    lane = lax.broadcasted_iota(jnp.int32, (1, LANES), 1)
    head_lanes = [(lane // HEAD_DIM) == hh for hh in range(pair)]

    blk_id = lax.broadcasted_iota(jnp.int32, (nb, S), 0)
    own = lax.broadcasted_iota(jnp.int32, (nb, S), 1) // MB_BLOCK
    valid = blk_id < own
    sels = []
    for t, hh in heads:
        kmean = kmean_ref[:, tiles[t]]
        kmean_rep = jnp.concatenate(
            [kmean] + [jnp.broadcast_to(kmean[m:m + 1, :], (nb, LANES)) for m in range(nb - 1)], axis=0)
        pieces = _split_bf16(jnp.where(head_lanes[hh], kmean_rep, 0.0), 3)
        gates = sum(_dot_nt(piece, q_ref[:, tiles[t]]) for piece in pieces)
        gate = gates[:nb]
        cnt = jnp.zeros((nb, S), jnp.int32)
        for m in range(nb - 1):
            gm = gates[nb * (m + 1):nb * (m + 2)]
            beats = (own > m) & ((gm > gate) | ((gm == gate) & (m < blk_id)))
            cnt = cnt + beats.astype(jnp.int32)
        sels.append((valid & (cnt < MB_TOPK)).astype(F32))

    ki = lax.broadcasted_iota(jnp.int32, (tq, tq), 0)
    qi = lax.broadcasted_iota(jnp.int32, (tq, tq), 1)
    causal = ki <= qi

    def scores(i):
        out = []
        for t, hh in heads:
            q_t = q_ref[i * tq:(i + 1) * tq, tiles[t]]
            k_all = k_ref[0:(i + 1) * MB_BLOCK, tiles[t]]
            out.append(_dot_nt(k_all, jnp.where(head_lanes[hh], q_t, jnp.zeros_like(q_t))))
        return out

    state = {"s": scores(0)}

    def tile(i):
        s_cur = state["s"]
        if i + 1 < nb:
            state["s"] = scores(i + 1)
        probs = []
        for hh in range(len(heads)):
            s = s_cur[hh]
            s_own = jnp.where(causal, s[i * MB_BLOCK:, :], MASK_VALUE)
            m_col = jnp.max(s_own, axis=0, keepdims=True)
            keeps = []
            for j in range(i):
                keep = sels[hh][j:j + 1, i * tq:(i + 1) * tq] > 0.5
                blk_max = jnp.max(s[j * MB_BLOCK:(j + 1) * MB_BLOCK, :], axis=0, keepdims=True)
                m_col = jnp.maximum(m_col, jnp.where(keep, blk_max, MASK_VALUE))
                keeps.append(keep)
            parts = [jnp.exp2(s[j * MB_BLOCK:(j + 1) * MB_BLOCK, :] - jnp.where(keeps[j], m_col, -MASK_VALUE))
                     for j in range(i)]
            parts.append(jnp.exp2(s_own - m_col))
            p = parts[0] if len(parts) == 1 else jnp.concatenate(parts, axis=0)
            probs.append(p.astype(BF16))
        ot = [_dot(vt_ref[t, :, 0:(i + 1) * MB_BLOCK], p) for (t, _), p in zip(heads, probs)]
        ot = [o[:LANES] / o[LANES:LANES + 1] for o in ot]
        for t, cols in enumerate(tiles):
            o_t = jnp.concatenate([ot[pair * t][:HEAD_DIM], ot[pair * t + 1][HEAD_DIM:]], axis=0).T
            o_ref[i * tq:(i + 1) * tq, cols] = o_t.astype(o_ref.dtype)

    return [functools.partial(tile, i) for i in range(nb)]


def _swa_steps(q_ref, k_ref, v_ref, sink_ref, o_ref, vt_ref):
    S = q_ref.shape[0]
    W = SW_WINDOW
    pair = LANES // HEAD_DIM
    tiles = [slice(t * LANES, (t + 1) * LANES) for t in range(q_ref.shape[1] // LANES)]
    vt_ref[0:LANES, :] = v_ref[...].astype(F32).T.astype(BF16)
    vt_ref[LANES:, :] = jnp.ones((vt_ref.shape[0] - LANES, S), BF16)
    lane = lax.broadcasted_iota(jnp.int32, (1, LANES), 1)
    dist = (lax.broadcasted_iota(jnp.int32, (2 * W, W), 1) + W
            - lax.broadcasted_iota(jnp.int32, (2 * W, W), 0))
    band = (dist >= 0) & (dist < W)
    causal = (lax.broadcasted_iota(jnp.int32, (W, W), 0)
              <= lax.broadcasted_iota(jnp.int32, (W, W), 1))
    head_lanes = [(lane // HEAD_DIM) == hh for hh in range(pair)]
    sinks = [[sink_ref[t, hh:hh + 1, 0:1] * LOG2E for hh in range(pair)]
             for t in range(len(tiles))]
    group = SWA_GROUP

    def scores(g0):
        work = []
        for n in range(g0, g0 + group):
            k0 = max(n - 1, 0) * W
            k1 = (n + 1) * W
            mask = causal if n == 0 else band
            k_w = k_ref[k0:k1, :]
            for t, cols in enumerate(tiles):
                q_t = q_ref[n * W:(n + 1) * W, cols]
                for hh in range(pair):
                    q_h = jnp.where(head_lanes[hh], q_t, jnp.zeros_like(q_t))
                    work.append(dict(n=n, t=t, hh=hh, k0=k0, k1=k1, mask=mask, s=_dot_nt(k_w, q_h)))
        return work

    state = {"work": scores(0)}

    def run_group(g0):
        work = state["work"]
        if g0 + group < S // W:
            state["work"] = scores(g0 + group)
        for d in work:
            s = jnp.where(d["mask"], d["s"], MASK_VALUE)
            sink = sinks[d["t"]][d["hh"]]
            m_col = jnp.maximum(jnp.max(s, axis=0, keepdims=True), sink)
            d["p"] = jnp.exp2(s - m_col).astype(BF16)
            d["sink_term"] = jnp.exp2(sink - m_col)
        for d in work:
            ot = _dot(vt_ref[:, d["k0"]:d["k1"]], d["p"])
            d["ot"] = ot[:LANES] / (ot[LANES:LANES + 1] + d["sink_term"])
        for t, cols in enumerate(tiles):
            mine = [d for d in work if d["t"] == t]
            o_t = jnp.concatenate(
                [jnp.concatenate([a["ot"][:HEAD_DIM], b["ot"][HEAD_DIM:]], axis=0)
                 for a, b in zip(mine[0::2], mine[1::2])], axis=1)
            o_ref[g0 * W:(g0 + group) * W, cols] = o_t.T.astype(o_ref.dtype)

    return [functools.partial(run_group, g0) for g0 in range(0, S // W, group)]


def _attn_kernel(mq_ref, mk_ref, mv_ref, sq_ref, sk_ref, sv_ref, sink_ref, omb_ref, osw_ref,
                 kmean_ref, mvt_ref, svt_ref):
    moba = _moba_steps(mq_ref, mk_ref, mv_ref, omb_ref, kmean_ref, mvt_ref)
    swa = _swa_steps(sq_ref, sk_ref, sv_ref, sink_ref, osw_ref, svt_ref)
    per = len(moba) // len(swa)
    for g, swa_group in enumerate(swa):
        for moba_tile in moba[g * per:(g + 1) * per]:
            moba_tile()
        swa_group()


def _attn(mq, mk, mv, sq, sk, sv, sinks, B, S, layer):
    T = mq.shape[0]
    nt = ATTN_TILES
    width = nt * LANES
    spec = pl.BlockSpec((S, width), lambda b, p: (b, p))
    kv_spec = pl.BlockSpec((S, LANES), lambda b, p: (b, 0))
    return pl.pallas_call(
        _attn_kernel,
        out_shape=(jax.ShapeDtypeStruct((T, BRANCH), BF16),) * 2,
        grid=(B, BRANCH // width),
        in_specs=[spec, spec, spec, spec, kv_spec, kv_spec,
                  pl.BlockSpec((None, nt, 2, LANES), lambda b, p: (layer, p, 0, 0))],
        out_specs=(spec, spec),
        scratch_shapes=[pltpu.VMEM((S // MB_BLOCK, width), F32),
                        pltpu.VMEM((nt, LANES + BF16_SUBLANES, S), BF16),
                        pltpu.VMEM((LANES + BF16_SUBLANES, S), BF16)],
        compiler_params=_cparams("parallel", "parallel"),
        name="attn",
    )(mq, mk, mv, sq, sk, sv, sinks)


def _mix_out_kernel(x_ref, orw_ref, omb_ref, osw_ref, gate_ref, wb_ref, wo_ref, g_ref, out_ref):
    tm = x_ref.shape[0]
    subs = [slice(r, r + MIX_OUT_SUB) for r in range(0, tm, MIX_OUT_SUB)]

    def gated(rows):
        y = None
        for n, o_ref in enumerate((orw_ref, omb_ref, osw_ref)):
            t = gate_ref[rows, n * D_MODEL:(n + 1) * D_MODEL] * _dot(o_ref[rows, :], wb_ref[n])
            y = t if y is None else y + t
        return y.astype(BF16)

    y_next = gated(subs[0])
    for i, rows in enumerate(subs):
        y = y_next
        if i + 1 < len(subs):
            y_next = gated(subs[i + 1])
        out_ref[rows, :] = x_ref[rows, :] + _rms(_dot(y, wo_ref[...]), g_ref[...])


def _mix_out(x2, o_rw, o_mb, o_sw, gates, w_branch, w_out, g, layer):
    T = x2.shape[0]
    tm = MIX_OUT_ROWS
    row = lambda i: (i, 0)
    br = pl.BlockSpec((tm, BRANCH), row)
    return pl.pallas_call(
        _mix_out_kernel,
        out_shape=jax.ShapeDtypeStruct((T, D_MODEL), F32),
        grid=(T // tm,),
        in_specs=[pl.BlockSpec((tm, D_MODEL), row), br, br, br,
                  pl.BlockSpec((tm, 3 * D_MODEL), row),
                  _layer_spec(w_branch.shape[1:], layer), _layer_spec(w_out.shape[1:], layer),
                  _layer_spec((1, D_MODEL), layer)],
        out_specs=pl.BlockSpec((tm, D_MODEL), row),
        compiler_params=_cparams("parallel"),
        name="mix_out",
    )(x2, o_rw, o_mb, o_sw, gates, w_branch, w_out, g)


def _mem_kv_kernel(m_ref, g_ref, wk_ref, wv_ref, k_ref, vt_ref):
    m = _rms(m_ref[...], g_ref[...]).astype(BF16)
    k_ref[...] = _dot(m, wk_ref[...]).astype(k_ref.dtype)
    vt_ref[...] = _dot(m, wv_ref[...]).T.astype(vt_ref.dtype)


def _mem_kv(mem2, g, wk, wv, M, layer):
    R = mem2.shape[0]
    row = lambda i: (i, 0)
    return pl.pallas_call(
        _mem_kv_kernel,
        out_shape=(jax.ShapeDtypeStruct((R, XA_WIDTH), BF16),
                   jax.ShapeDtypeStruct((R // M, XA_WIDTH, M), BF16)),
        grid=(R // M,),
        in_specs=[pl.BlockSpec((M, D_MODEL), row), _layer_spec((1, D_MODEL), layer),
                  _layer_spec(wk.shape[1:], layer), _layer_spec(wv.shape[1:], layer)],
        out_specs=(pl.BlockSpec((M, XA_WIDTH), row),
                   pl.BlockSpec((None, XA_WIDTH, M), lambda i: (i, 0, 0))),
        compiler_params=_cparams("parallel"),
        name="mem_kv",
    )(mem2, g, wk, wv)


def _xattn_kernel(x_ref, k_ref, vt_ref, gpre_ref, wq_ref, wo_ref, gpost_ref, out_ref):
    tm = x_ref.shape[0]
    groups = [slice(r, r + XATTN_SUB) for r in range(0, tm, XATTN_SUB)]
    heads = [slice(hd * XA_HEAD_DIM, (hd + 1) * XA_HEAD_DIM) for hd in range(XA_HEADS)]
    qs = []
    for rows in groups:
        h = _rms(x_ref[rows, :], gpre_ref[...]).astype(BF16)
        qs.append((_dot(h, wq_ref[...]) * (LOG2E * XA_HEAD_DIM ** -0.5)).astype(BF16))
    scores = [[_dot_nt(k_ref[:, sl], q[:, sl]) for sl in heads] for q in qs]
    ones = jnp.ones((BF16_SUBLANES, vt_ref.shape[1]), BF16)
    vts = [jnp.concatenate([vt_ref[sl, :], ones], axis=0) for sl in heads]
    outs = []
    for grp in scores:
        probs = [jnp.exp2(s - jnp.max(s, axis=0, keepdims=True)).astype(BF16) for s in grp]
        pv = [_dot(vt, p) for vt, p in zip(vts, probs)]
        o_t = jnp.concatenate([o[:XA_HEAD_DIM] / o[XA_HEAD_DIM:XA_HEAD_DIM + 1] for o in pv], axis=0)
        outs.append(o_t.T.astype(BF16))
    ys = [_dot(o, wo_ref[...]) for o in outs]
    for rows, y in zip(groups, ys):
        out_ref[rows, :] = x_ref[rows, :] + _rms(y, gpost_ref[...])


def _xattn(x2, k_mem, vt_mem, g_pre, wq, wo, g_post, B, S, M, layer):
    T = x2.shape[0]
    tm = XATTN_ROWS
    nt = S // tm
    row = lambda b, i: (b * nt + i, 0)
    return pl.pallas_call(
        _xattn_kernel,
        out_shape=jax.ShapeDtypeStruct((T, D_MODEL), F32),
        grid=(B, nt),
        in_specs=[pl.BlockSpec((tm, D_MODEL), row),
                  pl.BlockSpec((M, XA_WIDTH), lambda b, i: (b, 0)),
                  pl.BlockSpec((None, XA_WIDTH, M), lambda b, i: (b, 0, 0)),
                  _layer_spec((1, D_MODEL), layer), _layer_spec(wq.shape[1:], layer),
                  _layer_spec(wo.shape[1:], layer), _layer_spec((1, D_MODEL), layer)],
        out_specs=pl.BlockSpec((tm, D_MODEL), row),
        compiler_params=_cparams("parallel", "parallel"),
        name="xattn",
    )(x2, k_mem, vt_mem, g_pre, wq, wo, g_post)


def _mlp_kernel(x_ref, gpre_ref, wup_ref, wdn_ref, gpost_ref, out_ref, h_ref, acc_ref):
    j = pl.program_id(1)

    @pl.when(j == 0)
    def _():
        h_ref[...] = _rms(x_ref[...], gpre_ref[...]).astype(BF16)
        acc_ref[...] = jnp.zeros_like(acc_ref)

    h = h_ref[...]
    n_sub = wup_ref.shape[1] // MLP_SUB

    def up(c):
        return _dot(h, wup_ref[:, c * MLP_SUB:(c + 1) * MLP_SUB])

    a_next = up(0)
    acc = None
    for c in range(n_sub):
        a = jnp.maximum(a_next, 0.0)
        if c + 1 < n_sub:
            a_next = up(c + 1)
        t = _dot((a * a).astype(BF16), wdn_ref[c * MLP_SUB:(c + 1) * MLP_SUB, :])
        acc = t if acc is None else acc + t
    acc_ref[...] += acc

    @pl.when(j == pl.num_programs(1) - 1)
    def _():
        out_ref[...] = x_ref[...] + _rms(acc_ref[...], gpost_ref[...])


def _mlp(x2, g_pre, w_up, w_down, g_post, layer):
    T = x2.shape[0]
    tm = MLP_ROWS
    tf = MLP_COLS
    row = lambda i, j: (i, 0)
    return pl.pallas_call(
        _mlp_kernel,
        out_shape=jax.ShapeDtypeStruct((T, D_MODEL), F32),
        grid=(T // tm, D_FF // tf),
        in_specs=[pl.BlockSpec((tm, D_MODEL), row), _layer_spec((1, D_MODEL), layer),
                  pl.BlockSpec((None, D_MODEL, tf), lambda i, j: (layer, 0, j)),
                  pl.BlockSpec((None, tf, D_MODEL), lambda i, j: (layer, j, 0)),
                  _layer_spec((1, D_MODEL), layer)],
        out_specs=pl.BlockSpec((tm, D_MODEL), row),
        scratch_shapes=[pltpu.VMEM((tm, D_MODEL), BF16), pltpu.VMEM((tm, D_MODEL), F32)],
        compiler_params=_cparams("parallel", "arbitrary"),
        name="mlp",
    )(x2, g_pre, w_up, w_down, g_post)


def kernel(x, mem, positions, norm_mix_pre, norm_mix_post, norm_xattn_pre, norm_xattn_post, norm_mem, norm_mlp_pre, norm_mlp_post, w_in, rw_mu, rw_w0, rw_w_up, rw_a0, rw_a_up, rw_g_up, rw_k_k, rw_k_a, rw_r_k, rw_lnx_w, rw_lnx_b, rw_vres_down, rw_vres_mu, rw_v0, rw_vres_up, sw_sinks, w_branch, w_out, w_xq, w_xk, w_xv, w_xo, w_up, w_down):
    B, S, D = x.shape
    M = mem.shape[1]
    depth = w_in.shape[0]
    T = B * S
    x2 = x.reshape(T, D)
    mem2 = mem.reshape(B * M, D)
    cos_t, sin_t = _rope_tables(positions)

    rows = lambda t: t.reshape(t.shape[0], 1, -1)
    bf = lambda t: t.astype(BF16)
    lora_in = rw_w_up.shape[1]
    per_kv = N_HEADS // SW_KV_HEADS
    head_of_col = jnp.arange(MXU_TILE) // HEAD_DIM
    rw_p = dict(
        mu=rows(rw_mu), w0=rows(rw_w0), a0=rows(rw_a0), k_k=rows(rw_k_k), k_a=rows(rw_k_a),
        r_k=rows(rw_r_k), lnx_w=rows(rw_lnx_w), lnx_b=rows(rw_lnx_b),
        w_up=bf(jnp.pad(rw_w_up, ((0, 0), (0, LORA_PAD - lora_in), (0, 0)))),
        a_up=bf(jnp.pad(rw_a_up, ((0, 0), (lora_in, LORA_PAD - lora_in - rw_a_up.shape[1]), (0, 0)))),
        g_up=bf(rw_g_up),
        bd=(head_of_col[:, None] == head_of_col[None, :]).astype(BF16))
    mv_pad = LORA_PAD - rw_vres_down.shape[2]
    w_vm = bf(jnp.pad(rw_vres_down, ((0, 0), (0, 0), (0, mv_pad))))
    vres_p = dict(mu=rows(jnp.pad(rw_vres_mu, ((0, 0), (0, mv_pad)))), v0=rows(rw_v0),
                  up=bf(jnp.pad(rw_vres_up, ((0, 0), (0, mv_pad), (0, 0)))))
    sinks = jnp.broadcast_to(
        sw_sinks.reshape(depth, SW_KV_HEADS, per_kv).transpose(0, 2, 1)[..., None].astype(F32),
        (depth, per_kv, SW_KV_HEADS, LANES))
    wb_sw = w_branch[:, 2].reshape(depth, SW_KV_HEADS, per_kv, HEAD_DIM, D)
    wb_sw = wb_sw.transpose(0, 2, 1, 3, 4).reshape(depth, 1, BRANCH, D)
    wb = bf(jnp.concatenate([w_branch[:, :2], wb_sw], axis=1))
    w_in_b, w_out_b, w_up_b, w_down_b = bf(w_in), bf(w_out), bf(w_up), bf(w_down)
    w_xq_b, w_xk_b, w_xv_b, w_xo_b = bf(w_xq), bf(w_xk), bf(w_xv), bf(w_xo)
    g_mix_pre, g_mix_post = rows(norm_mix_pre), rows(norm_mix_post)
    g_xa_pre, g_xa_post, g_mem = rows(norm_xattn_pre), rows(norm_xattn_post), rows(norm_mem)
    g_mlp_pre, g_mlp_post = rows(norm_mlp_pre), rows(norm_mlp_post)

    v_first = None
    for l in range(depth):
        outs = _mix_in(x2, g_mix_pre, w_in_b, w_vm if l > 0 else None, cos_t, sin_t, l)
        if l > 0:
            u_rw, vm, mq, mk, mv, sq, sk, sv, gates = outs
            o_rw = _rwkv(u_rw, rw_p, l, B, S, dict(vres_p, vm=vm, v_first=v_first))
        else:
            u_rw, mq, mk, mv, sq, sk, sv, gates = outs
            o_rw, v_first = _rwkv(u_rw, rw_p, l, B, S)
        o_mb, o_sw = _attn(mq, mk, mv, sq, sk, sv, sinks, B, S, l)
        x2 = _mix_out(x2, o_rw, o_mb, o_sw, gates, wb, w_out_b, g_mix_post, l)
        k_mem, vt_mem = _mem_kv(mem2, g_mem, w_xk_b, w_xv_b, M, l)
        x2 = _xattn(x2, k_mem, vt_mem, g_xa_pre, w_xq_b, w_xo_b, g_xa_post, B, S, M, l)
        x2 = _mlp(x2, g_mlp_pre, w_up_b, w_down_b, g_mlp_post, l)
    return x2.reshape(B, S, D)
```

```python
import functools

import jax
import jax.numpy as jnp
from jax import lax
from jax.experimental import pallas as pl
from jax.experimental.pallas import tpu as pltpu

F32 = jnp.float32
BF16 = jnp.bfloat16

D_MODEL = 1024
HEAD_DIM = 64
BRANCH = 512
N_HEADS = BRANCH // HEAD_DIM
RW_COLS = 3 * BRANCH + 64 + 64 + 128
LORA_PAD = 128
MB_BLOCK = 256
MB_TOPK = 3
SW_WINDOW = 128
SW_KV_HEADS = 2
XA_HEADS = 4
XA_HEAD_DIM = 128
XA_WIDTH = XA_HEADS * XA_HEAD_DIM
ROPE_THETA = 10000.0
NORM_EPS = 1e-6
RW_LNX_EPS = 1e-5 * HEAD_DIM
MASK_VALUE = -1e30
LOG2E = 1.4426950408889634

LANES = 128
MXU_TILE = 256
BF16_SUBLANES = 16
VMEM_LIMIT = 56 * 1024 * 1024

RW_CHUNK = 64
RW_SUBCHUNKS = 2
RW_BATCH = 4
ATTN_TILES = 2
SWA_GROUP = 4
ROPE_ROWS = 2048
MIX_IN_ROWS = 512
XATTN_ROWS = 1024
XATTN_SUB = 256
MIX_OUT_ROWS = 1024
MIX_OUT_SUB = 512
MLP_ROWS = 1024
MLP_GROUP = 256
MLP_SUB = 512


def _cparams(*sem):
    return pltpu.CompilerParams(dimension_semantics=sem, vmem_limit_bytes=VMEM_LIMIT)


def _dot(a, b):
    return jnp.dot(a, b, preferred_element_type=F32)


def _dot_nt(a, b):
    return lax.dot_general(a, b, (((1,), (1,)), ((), ())), preferred_element_type=F32)


def _dot_tn(a, b):
    return lax.dot_general(a, b, (((0,), (0,)), ((), ())), preferred_element_type=F32)


def _rms(xf, g):
    ms = jnp.mean(xf * xf, axis=-1, keepdims=True)
    return xf * lax.rsqrt(ms + NORM_EPS) * g


def _const_spec(shape):
    nd = len(shape)
    return pl.BlockSpec(shape, lambda *_: (0,) * nd, pipeline_mode=pl.Buffered(1))


def _layer_spec(shape, layer):
    nd = len(shape)
    return pl.BlockSpec((None,) + tuple(shape), lambda *_: (layer,) + (0,) * nd,
                        pipeline_mode=pl.Buffered(1))


def _rope_kernel(pos_ref, invf_ref, cos_ref, sin_ref):
    ang = invf_ref[...] * pos_ref[...].astype(F32)
    c = jnp.cos(ang)
    s = jnp.sin(ang)
    reps = LANES // HEAD_DIM
    cos_ref[...] = jnp.concatenate([c, c] * reps, axis=0).T
    sin_ref[...] = jnp.concatenate([-s, s] * reps, axis=0).T


def _rope_tables(positions):
    T = positions.size
    tm = ROPE_ROWS
    inv_freq = 1.0 / (ROPE_THETA ** (jnp.arange(0, HEAD_DIM, 2, dtype=F32) / HEAD_DIM))
    return pl.pallas_call(
        _rope_kernel,
        out_shape=(jax.ShapeDtypeStruct((T, LANES), F32),) * 2,
        grid=(T // tm,),
        in_specs=[pl.BlockSpec((1, tm), lambda i: (0, i)), _const_spec((HEAD_DIM // 2, 1))],
        out_specs=(pl.BlockSpec((tm, LANES), lambda i: (i, 0)),) * 2,
        compiler_params=_cparams("parallel"),
        name="rope_tables",
    )(positions.reshape(1, T), inv_freq[:, None])


def _rope_tile(t, cos, sin, lane_lt_half):
    fwd = pltpu.roll(t, LANES - HEAD_DIM // 2, axis=1)
    bwd = pltpu.roll(t, HEAD_DIM // 2, axis=1)
    partner = jnp.where(lane_lt_half, fwd, bwd)
    return t * cos + partner * sin


def _mix_in_kernel(has_vm, x_ref, g_ref, w_ref, *rest):
    if has_vm:
        (wvm_ref, cos_ref, sin_ref,
         urw_ref, vm_ref, mq_ref, mk_ref, mv_ref, sq_ref, sk_ref, sv_ref, gate_ref) = rest
    else:
        cos_ref, sin_ref, urw_ref, mq_ref, mk_ref, mv_ref, sq_ref, sk_ref, sv_ref, gate_ref = rest
    h = _rms(x_ref[...], g_ref[...]).astype(BF16)
    cos = cos_ref[...]
    sin = sin_ref[...]
    lane = lax.broadcasted_iota(jnp.int32, cos.shape, 1)
    lt_half = (lane % HEAD_DIM) < (HEAD_DIM // 2)
    first_head = lane < HEAD_DIM
    q_scale = LOG2E * HEAD_DIM ** -0.5

    kv_w = SW_KV_HEADS * HEAD_DIM
    widths = dict(rw=RW_COLS, mq=BRANCH, mk=BRANCH, mv=BRANCH, sq=BRANCH, sk=kv_w, sv=kv_w,
                  gate=3 * D_MODEL)
    start, col = {}, 0
    for name, width in widths.items():
        start[name] = col
        col += width

    def seg(name, offset=0, width=None):
        a = start[name] + offset
        return _dot(h, w_ref[:, a:a + (widths[name] if width is None else width)])

    def rope_chunks(name, scale):
        acc = seg(name)
        return [_rope_tile(acc[:, c * LANES:(c + 1) * LANES], cos, sin, lt_half) * scale
                for c in range(widths[name] // LANES)]

    def store_chunks(ref, chunks):
        for c, t in enumerate(chunks):
            ref[:, c * LANES:(c + 1) * LANES] = t.astype(ref.dtype)

    for n in range(3):
        gate = jax.nn.sigmoid(seg("gate", n * D_MODEL, D_MODEL))
        gate_ref[:, n * D_MODEL:(n + 1) * D_MODEL] = gate.astype(gate_ref.dtype)
    store_chunks(mq_ref, rope_chunks("mq", q_scale))
    store_chunks(mk_ref, rope_chunks("mk", 1.0))
    nat = rope_chunks("sq", q_scale)
    half_tiles = len(nat) // 2
    paired = []
    for p in range(len(nat)):
        a, b = nat[p // 2], nat[p // 2 + half_tiles]
        if p % 2 == 0:
            paired.append(jnp.where(first_head, a, pltpu.roll(b, HEAD_DIM, axis=1)))
        else:
            paired.append(jnp.where(first_head, pltpu.roll(a, HEAD_DIM, axis=1), b))
    store_chunks(sq_ref, paired)
    store_chunks(sk_ref, rope_chunks("sk", 1.0))
    urw_ref[...] = seg("rw").astype(urw_ref.dtype)
    mv_ref[...] = seg("mv").astype(mv_ref.dtype)
    if has_vm:
        vm_ref[...] = _dot(h, wvm_ref[...])
    sv_ref[...] = seg("sv").astype(sv_ref.dtype)


def _mix_in(x2, g, w_in, w_vm, cos_t, sin_t, layer):
    T = x2.shape[0]
    tm = MIX_IN_ROWS
    has_vm = w_vm is not None
    kv_w = SW_KV_HEADS * HEAD_DIM
    widths = [(RW_COLS, BF16)]
    if has_vm:
        widths.append((LORA_PAD, F32))
    widths += [(BRANCH, BF16), (BRANCH, BF16), (BRANCH, BF16), (BRANCH, BF16),
               (kv_w, BF16), (kv_w, BF16), (3 * D_MODEL, BF16)]
    assert sum(w for w, _ in widths) - (LORA_PAD if has_vm else 0) == w_in.shape[2]
    row = lambda i: (i, 0)
    in_specs = [pl.BlockSpec((tm, D_MODEL), row), _layer_spec((1, D_MODEL), layer),
                _layer_spec(w_in.shape[1:], layer)]
    args = [x2, g, w_in]
    if has_vm:
        in_specs.append(_layer_spec(w_vm.shape[1:], layer - 1))
        args.append(w_vm)
    in_specs += [pl.BlockSpec((tm, LANES), row), pl.BlockSpec((tm, LANES), row)]
    args += [cos_t, sin_t]
    return pl.pallas_call(
        functools.partial(_mix_in_kernel, has_vm),
        out_shape=tuple(jax.ShapeDtypeStruct((T, w), dt) for w, dt in widths),
        grid=(T // tm,),
        in_specs=in_specs,
        out_specs=tuple(pl.BlockSpec((tm, w), row) for w, _ in widths),
        compiler_params=_cparams("parallel"),
        name="mix_in",
    )(*args)


def _shift_prev(cur, carry_ref, bi):
    rolled = pltpu.roll(cur, 1, axis=0)
    row = lax.broadcasted_iota(jnp.int32, cur.shape, 0)
    prev = jnp.where(row == 0, carry_ref[bi, 0:1, :], rolled)
    carry_ref[bi, 0:1, :] = cur[cur.shape[0] - 1:, :]
    return prev


def _split_bf16(x, terms):
    pieces = []
    for _ in range(terms):
        hi = x.astype(BF16)
        pieces.append(hi)
        x = x - hi.astype(F32)
    return pieces


def _head_sums(x, bd, terms):
    w = bd.shape[0]
    halves = []
    for c in range(x.shape[1] // w):
        pieces = _split_bf16(x[:, c * w:(c + 1) * w], terms)
        halves.append(sum(_dot(piece, bd) for piece in pieces))
    return jnp.concatenate(halves, axis=1)


def _dot_split_rhs(a, b, terms):
    return sum(_dot(a, piece) for piece in _split_bf16(b, terms))


def _rwkv_kernel(has_vres, nb, *refs):
    if has_vres:
        (u_ref, vm_ref, vf_ref, mu_ref, w0_ref, wup_ref, a0_ref, aup_ref, gup_ref, kk_ref,
         ka_ref, rk_ref, lw_ref, lb_ref, bd_ref, vmu_ref, v0_ref, vup_ref,
         o_ref, s_ref, cu_ref, cvm_ref, hout_ref) = refs
        vout_ref = None
    else:
        (u_ref, mu_ref, w0_ref, wup_ref, a0_ref, aup_ref, gup_ref, kk_ref,
         ka_ref, rk_ref, lw_ref, lb_ref, bd_ref,
         o_ref, vout_ref, s_ref, cu_ref, hout_ref) = refs
    C = RW_CHUNK
    P2 = 2 * C
    npair = BRANCH // LANES
    n_sub = u_ref.shape[1] // C

    @pl.when(pl.program_id(1) == 0)
    def _():
        s_ref[...] = jnp.zeros_like(s_ref)
        cu_ref[...] = jnp.zeros_like(cu_ref)
        if has_vres:
            cvm_ref[...] = jnp.zeros_like(cvm_ref)

    bd = bd_ref[...]
    ti = lax.broadcasted_iota(jnp.int32, (C, C), 0)
    si = lax.broadcasted_iota(jnp.int32, (C, C), 1)
    tri = (si <= ti).astype(BF16)
    lane = lax.broadcasted_iota(jnp.int32, (1, LANES), 1)
    first_head = lane < HEAD_DIM
    pr = lax.broadcasted_iota(jnp.int32, (P2, P2), 0)
    pc = lax.broadcasted_iota(jnp.int32, (P2, P2), 1)
    same_head = (pr // C) == (pc // C)
    strict = same_head & ((pc % C) < (pr % C))
    incl = same_head & ((pc % C) <= (pr % C))
    eye = (pr == pc).astype(F32)

    def stack(t):
        t = t.astype(BF16)
        zero = jnp.zeros_like(t)
        return jnp.concatenate([jnp.where(first_head, t, zero), jnp.where(first_head, zero, t)], axis=0)

    pre = []
    inst = []
    for bi in range(nb):
        u = u_ref[bi].astype(F32)
        ul = u + mu_ref[...] * (_shift_prev(u, cu_ref, bi) - u)
        r = ul[:, 0:BRANCH]
        k = ul[:, BRANCH:2 * BRANCH]
        v = ul[:, 2 * BRANCH:3 * BRANCH]
        x_wa = ul[:, 3 * BRANCH:3 * BRANCH + LANES]
        x_g = ul[:, 3 * BRANCH + LANES:]

        z = w0_ref[...] + _dot(jnp.tanh(x_wa).astype(BF16), wup_ref[...])
        a = jax.nn.sigmoid(a0_ref[...] + _dot(x_wa.astype(BF16), aup_ref[...]))
        g = _dot(jax.nn.sigmoid(x_g).astype(BF16), gup_ref[...])
        softplus = jnp.maximum(-z, 0.0) + jnp.log1p(jnp.exp(-jnp.abs(z)))
        ell = -jnp.exp(-softplus - 0.5)

        if has_vres:
            vm = vm_ref[bi]
            vml = vm + vmu_ref[...] * (_shift_prev(vm, cvm_ref, bi) - vm)
            vgate = jax.nn.sigmoid(v0_ref[...] + _dot(vml.astype(BF16), vup_ref[...]))
            v = v + (vf_ref[bi] - v) * vgate
        else:
            vout_ref[bi] = v

        kk = k * kk_ref[...]
        kk = kk / jnp.maximum(jnp.sqrt(_head_sums(kk * kk, bd, 1)), 1e-12)
        k_eff = k * (1.0 + (a - 1.0) * ka_ref[...])
        b = kk * a
        pre.append((r, k_eff, v, g))
        for ci in range(n_sub):
            rs = slice(ci * C, (ci + 1) * C)
            ell_c = ell[rs]
            cum = _dot_split_rhs(tri, ell_c, 3)
            cum_last = cum[C - 1:, :]
            g_inv = jnp.exp(-cum)
            g_tail = jnp.exp(cum_last - cum)
            kap_t = kk[rs] * jnp.exp(cum - ell_c)
            r_t = r[rs] * jnp.exp(cum)
            b_t = b[rs] * g_inv
            k_t = k_eff[rs] * g_inv
            b_g = b[rs] * g_tail
            k_g = k_eff[rs] * g_tail
            g_all = jnp.exp(cum_last)
            v_c = v[rs]
            for p in range(npair):
                sl = slice(p * LANES, (p + 1) * LANES)
                inst.append(dict(bi=bi, ci=ci, rs=rs, p=p, sl=sl,
                                 kap=stack(kap_t[:, sl]), r=stack(r_t[:, sl]), b=stack(b_t[:, sl]),
                                 k=stack(k_t[:, sl]), v=stack(v_c[:, sl]), bg=stack(b_g[:, sl]),
                                 kg=stack(k_g[:, sl]), gall=g_all[:, sl]))

    incl2 = jnp.concatenate([incl, incl], axis=1)
    for d in inst:
        aa = _dot_nt(jnp.concatenate([d["kap"], d["r"]], axis=0),
                     jnp.concatenate([d["b"], d["k"]], axis=0))
        n = jnp.where(strict, -aa[:P2, :P2], 0.0)
        d["n"] = n.astype(BF16)
        d["t"] = eye + n
        d["a_ak"] = jnp.where(strict, aa[:P2, P2:], 0.0).astype(BF16)
        d["a_r"] = jnp.where(incl2, aa[P2:, :], 0.0).astype(BF16)
    for d in inst:
        d["akv"] = _dot(d["a_ak"], d["v"]).astype(BF16)
    for d in inst:
        d["pw"] = _dot(d["n"], d["n"]).astype(BF16)
    for _ in range(4):
        for d in inst:
            pt = _dot(d["pw"], jnp.concatenate([d["pw"], d["t"].astype(BF16)], axis=1))
            d["pw"] = pt[:, :P2].astype(BF16)
            d["t"] = d["t"] + pt[:, P2:]
    for d in inst:
        d["t"] = d["t"] + _dot(d["pw"], d["t"].astype(BF16))
    for d in inst:
        ty = _dot(d["t"].astype(BF16), jnp.concatenate([d["kap"], d["akv"]], axis=1))
        d["w_til"], d["u_til"] = ty[:, :LANES].astype(BF16), ty[:, LANES:]
    for ci in range(n_sub):
        chunk = [d for d in inst if d["ci"] == ci]
        for d in chunk:
            d["state"] = s_ref[d["bi"] * npair + d["p"]]
            xr = _dot_nt(jnp.concatenate([d["w_til"], d["r"]], axis=0), d["state"].astype(BF16))
            sa = -(xr[:P2] + d["u_til"])
            d["sv"] = jnp.concatenate([sa.astype(BF16), d["v"]], axis=0)
            d["rs_out"] = xr[P2:]
        for d in chunk:
            s_ref[d["bi"] * npair + d["p"]] = (
                d["state"] * d["gall"]
                + _dot_tn(d["sv"], jnp.concatenate([d["bg"], d["kg"]], axis=0)))
        for d in chunk:
            o_st = d["rs_out"] + _dot(d["a_r"], d["sv"])
            hout_ref[d["bi"], d["rs"], d["sl"]] = o_st[:C] + o_st[C:]

    inv_n = 1.0 / HEAD_DIM
    for bi in range(nb):
        r, k_eff, v, g = pre[bi]
        out = hout_ref[bi]
        mean = _head_sums(out, bd, 2) * inv_n
        cen = out - mean
        var = _head_sums(cen * cen, bd, 1) * inv_n
        normed = cen * lax.rsqrt(var + RW_LNX_EPS) * lw_ref[...] + lb_ref[...]
        bonus = _head_sums(r * k_eff * rk_ref[...], bd, 2) * v
        o_ref[bi] = ((normed + bonus) * g).astype(o_ref.dtype)


def _rwkv(u_rw, p, layer, B, S, vres=None):
    T = u_rw.shape[0]
    C = RW_CHUNK
    nb = RW_BATCH
    rows = RW_SUBCHUNKS * C
    nc = S // rows
    has_vres = vres is not None
    blk = lambda w: pl.BlockSpec((nb, rows, w), lambda b, c: (b, c, 0))
    names = ("mu", "w0", "w_up", "a0", "a_up", "g_up", "k_k", "k_a", "r_k", "lnx_w", "lnx_b")
    in_specs = [blk(RW_COLS)]
    args = [u_rw.reshape(B, S, RW_COLS)]
    if has_vres:
        in_specs += [blk(LORA_PAD), blk(BRANCH)]
        args += [vres["vm"].reshape(B, S, LORA_PAD), vres["v_first"].reshape(B, S, BRANCH)]
    in_specs += [_layer_spec(p[n].shape[1:], layer) for n in names]
    in_specs.append(_const_spec((MXU_TILE, MXU_TILE)))
    args += [p[n] for n in names] + [p["bd"]]
    scratch = [pltpu.VMEM((nb * (BRANCH // LANES), 2 * C, LANES), F32),
               pltpu.VMEM((nb, 8, RW_COLS), F32)]
    if has_vres:
        vnames = ("mu", "v0", "up")
        in_specs += [_layer_spec(vres[n].shape[1:], layer - 1) for n in vnames]
        args += [vres[n] for n in vnames]
        out_shape = jax.ShapeDtypeStruct((B, S, BRANCH), BF16)
        out_specs = blk(BRANCH)
        scratch.append(pltpu.VMEM((nb, 8, LORA_PAD), F32))
    else:
        out_shape = (jax.ShapeDtypeStruct((B, S, BRANCH), BF16),
                     jax.ShapeDtypeStruct((B, S, BRANCH), F32))
        out_specs = (blk(BRANCH), blk(BRANCH))
    scratch.append(pltpu.VMEM((nb, rows, BRANCH), F32))
    outs = pl.pallas_call(
        functools.partial(_rwkv_kernel, has_vres, nb),
        out_shape=out_shape,
        grid=(B // nb, nc),
        in_specs=in_specs,
        out_specs=out_specs,
        scratch_shapes=scratch,
        compiler_params=_cparams("parallel", "arbitrary"),
        name="rwkv",
    )(*args)
    if has_vres:
        return outs.reshape(T, BRANCH)
    return outs[0].reshape(T, BRANCH), outs[1].reshape(T, BRANCH)


def _moba_steps(q_ref, k_ref, v_ref, o_ref, kmean_ref, vt_ref):
    S = q_ref.shape[0]
    nb = S // MB_BLOCK
    tq = MB_BLOCK
    pair = LANES // HEAD_DIM
    tiles = [slice(t * LANES, (t + 1) * LANES) for t in range(q_ref.shape[1] // LANES)]
    heads = [(t, hh) for t in range(len(tiles)) for hh in range(pair)]
    for n in range(nb):
        blk = k_ref[n * MB_BLOCK:(n + 1) * MB_BLOCK, :].astype(F32)
        kmean_ref[n:n + 1, :] = jnp.mean(blk, axis=0, keepdims=True)
    for t, cols in enumerate(tiles):
        vt_ref[t, 0:LANES, :] = v_ref[:, cols].astype(F32).T.astype(BF16)
        vt_ref[t, LANES:, :] = jnp.ones((vt_ref.shape[1] - LANES, S), BF16)
    lane = lax.broadcasted_iota(jnp.int32, (1, LANES), 1)
    head_lanes = [(lane // HEAD_DIM) == hh for hh in range(pair)]

    blk_id = lax.broadcasted_iota(jnp.int32, (nb, S), 0)
    own = lax.broadcasted_iota(jnp.int32, (nb, S), 1) // MB_BLOCK
    valid = blk_id < own
    sels = []
    for t, hh in heads:
        kmean = kmean_ref[:, tiles[t]]
        kmean_rep = jnp.concatenate(
            [kmean] + [jnp.broadcast_to(kmean[m:m + 1, :], (nb, LANES)) for m in range(nb - 1)], axis=0)
        pieces = _split_bf16(jnp.where(head_lanes[hh], kmean_rep, 0.0), 3)
        gates = sum(_dot_nt(piece, q_ref[:, tiles[t]]) for piece in pieces)
        gate = gates[:nb]
        cnt = jnp.zeros((nb, S), jnp.int32)
        for m in range(nb - 1):
            gm = gates[nb * (m + 1):nb * (m + 2)]
            beats = (own > m) & ((gm > gate) | ((gm == gate) & (m < blk_id)))
            cnt = cnt + beats.astype(jnp.int32)
        sels.append((valid & (cnt < MB_TOPK)).astype(F32))

    ki = lax.broadcasted_iota(jnp.int32, (tq, tq), 0)
    qi = lax.broadcasted_iota(jnp.int32, (tq, tq), 1)
    causal = ki <= qi

    def scores(i):
        out = []
        for t, hh in heads:
            q_t = q_ref[i * tq:(i + 1) * tq, tiles[t]]
            k_all = k_ref[0:(i + 1) * MB_BLOCK, tiles[t]]
            out.append(_dot_nt(k_all, jnp.where(head_lanes[hh], q_t, jnp.zeros_like(q_t))))
        return out

    state = {"s": scores(0)}

    def tile(i):
        s_cur = state["s"]
        if i + 1 < nb:
            state["s"] = scores(i + 1)
        probs = []
        for hh in range(len(heads)):
            s = s_cur[hh]
            s_own = jnp.where(causal, s[i * MB_BLOCK:, :], MASK_VALUE)
            m_col = jnp.max(s_own, axis=0, keepdims=True)
            keeps = []
            for j in range(i):
                keep = sels[hh][j:j + 1, i * tq:(i + 1) * tq] > 0.5
                blk_max = jnp.max(s[j * MB_BLOCK:(j + 1) * MB_BLOCK, :], axis=0, keepdims=True)
                m_col = jnp.maximum(m_col, jnp.where(keep, blk_max, MASK_VALUE))
                keeps.append(keep)
            parts = [jnp.exp2(s[j * MB_BLOCK:(j + 1) * MB_BLOCK, :] - jnp.where(keeps[j], m_col, -MASK_VALUE))
                     for j in range(i)]
            parts.append(jnp.exp2(s_own - m_col))
            p = parts[0] if len(parts) == 1 else jnp.concatenate(parts, axis=0)
            probs.append(p.astype(BF16))
        ot = [_dot(vt_ref[t, :, 0:(i + 1) * MB_BLOCK], p) for (t, _), p in zip(heads, probs)]
        ot = [o[:LANES] / o[LANES:LANES + 1] for o in ot]
        for t, cols in enumerate(tiles):
            o_t = jnp.concatenate([ot[pair * t][:HEAD_DIM], ot[pair * t + 1][HEAD_DIM:]], axis=0).T
            o_ref[i * tq:(i + 1) * tq, cols] = o_t.astype(o_ref.dtype)

    return [functools.partial(tile, i) for i in range(nb)]


def _swa_steps(q_ref, k_ref, v_ref, sink_ref, o_ref, vt_ref):
    S = q_ref.shape[0]
    W = SW_WINDOW
    pair = LANES // HEAD_DIM
    tiles = [slice(t * LANES, (t + 1) * LANES) for t in range(q_ref.shape[1] // LANES)]
    vt_ref[0:LANES, :] = v_ref[...].astype(F32).T.astype(BF16)
    vt_ref[LANES:, :] = jnp.ones((vt_ref.shape[0] - LANES, S), BF16)
    lane = lax.broadcasted_iota(jnp.int32, (1, LANES), 1)
    dist = (lax.broadcasted_iota(jnp.int32, (2 * W, W), 1) + W
            - lax.broadcasted_iota(jnp.int32, (2 * W, W), 0))
    band = (dist >= 0) & (dist < W)
    causal = (lax.broadcasted_iota(jnp.int32, (W, W), 0)
              <= lax.broadcasted_iota(jnp.int32, (W, W), 1))
    head_lanes = [(lane // HEAD_DIM) == hh for hh in range(pair)]
    sinks = [[sink_ref[t, hh:hh + 1, 0:1] * LOG2E for hh in range(pair)]
             for t in range(len(tiles))]
    group = SWA_GROUP

    def scores(g0):
        work = []
        for n in range(g0, g0 + group):
            k0 = max(n - 1, 0) * W
            k1 = (n + 1) * W
            mask = causal if n == 0 else band
            k_w = k_ref[k0:k1, :]
            for t, cols in enumerate(tiles):
                q_t = q_ref[n * W:(n + 1) * W, cols]
                for hh in range(pair):
                    q_h = jnp.where(head_lanes[hh], q_t, jnp.zeros_like(q_t))
                    work.append(dict(n=n, t=t, hh=hh, k0=k0, k1=k1, mask=mask, s=_dot_nt(k_w, q_h)))
        return work

    state = {"work": scores(0)}

    def run_group(g0):
        work = state["work"]
        if g0 + group < S // W:
            state["work"] = scores(g0 + group)
        for d in work:
            s = jnp.where(d["mask"], d["s"], MASK_VALUE)
            sink = sinks[d["t"]][d["hh"]]
            m_col = jnp.maximum(jnp.max(s, axis=0, keepdims=True), sink)
            d["p"] = jnp.exp2(s - m_col).astype(BF16)
            d["sink_term"] = jnp.exp2(sink - m_col)
        for d in work:
            ot = _dot(vt_ref[:, d["k0"]:d["k1"]], d["p"])
            d["ot"] = ot[:LANES] / (ot[LANES:LANES + 1] + d["sink_term"])
        for t, cols in enumerate(tiles):
            mine = [d for d in work if d["t"] == t]
            o_t = jnp.concatenate(
                [jnp.concatenate([a["ot"][:HEAD_DIM], b["ot"][HEAD_DIM:]], axis=0)
                 for a, b in zip(mine[0::2], mine[1::2])], axis=1)
            o_ref[g0 * W:(g0 + group) * W, cols] = o_t.T.astype(o_ref.dtype)

    return [functools.partial(run_group, g0) for g0 in range(0, S // W, group)]


def _attn_kernel(mq_ref, mk_ref, mv_ref, sq_ref, sk_ref, sv_ref, sink_ref, omb_ref, osw_ref,
                 kmean_ref, mvt_ref, svt_ref):
    moba = _moba_steps(mq_ref, mk_ref, mv_ref, omb_ref, kmean_ref, mvt_ref)
    swa = _swa_steps(sq_ref, sk_ref, sv_ref, sink_ref, osw_ref, svt_ref)
    per = len(moba) // len(swa)
    for g, swa_group in enumerate(swa):
        for moba_tile in moba[g * per:(g + 1) * per]:
            moba_tile()
        swa_group()


def _attn(mq, mk, mv, sq, sk, sv, sinks, B, S, layer):
    T = mq.shape[0]
    nt = ATTN_TILES
    width = nt * LANES
    spec = pl.BlockSpec((S, width), lambda b, p: (b, p))
    kv_spec = pl.BlockSpec((S, LANES), lambda b, p: (b, 0))
    return pl.pallas_call(
        _attn_kernel,
        out_shape=(jax.ShapeDtypeStruct((T, BRANCH), BF16),) * 2,
        grid=(B, BRANCH // width),
        in_specs=[spec, spec, spec, spec, kv_spec, kv_spec,
                  pl.BlockSpec((None, nt, 2, LANES), lambda b, p: (layer, p, 0, 0))],
        out_specs=(spec, spec),
        scratch_shapes=[pltpu.VMEM((S // MB_BLOCK, width), F32),
                        pltpu.VMEM((nt, LANES + BF16_SUBLANES, S), BF16),
                        pltpu.VMEM((LANES + BF16_SUBLANES, S), BF16)],
        compiler_params=_cparams("parallel", "parallel"),
        name="attn",
    )(mq, mk, mv, sq, sk, sv, sinks)


def _mix_out_kernel(x_ref, orw_ref, omb_ref, osw_ref, gate_ref, wb_ref, wo_ref, g_ref, out_ref):
    tm = x_ref.shape[0]
    subs = [slice(r, r + MIX_OUT_SUB) for r in range(0, tm, MIX_OUT_SUB)]

    def gated(rows):
        y = None
        for n, o_ref in enumerate((orw_ref, omb_ref, osw_ref)):
            t = gate_ref[rows, n * D_MODEL:(n + 1) * D_MODEL] * _dot(o_ref[rows, :], wb_ref[n])
            y = t if y is None else y + t
        return y.astype(BF16)

    y_next = gated(subs[0])
    for i, rows in enumerate(subs):
        y = y_next
        if i + 1 < len(subs):
            y_next = gated(subs[i + 1])
        out_ref[rows, :] = x_ref[rows, :] + _rms(_dot(y, wo_ref[...]), g_ref[...])


def _mix_out(x2, o_rw, o_mb, o_sw, gates, w_branch, w_out, g, layer):
    T = x2.shape[0]
    tm = MIX_OUT_ROWS
    row = lambda i: (i, 0)
    br = pl.BlockSpec((tm, BRANCH), row)
    return pl.pallas_call(
        _mix_out_kernel,
        out_shape=jax.ShapeDtypeStruct((T, D_MODEL), F32),
        grid=(T // tm,),
        in_specs=[pl.BlockSpec((tm, D_MODEL), row), br, br, br,
                  pl.BlockSpec((tm, 3 * D_MODEL), row),
                  _layer_spec(w_branch.shape[1:], layer), _layer_spec(w_out.shape[1:], layer),
                  _layer_spec((1, D_MODEL), layer)],
        out_specs=pl.BlockSpec((tm, D_MODEL), row),
        compiler_params=_cparams("parallel"),
        name="mix_out",
    )(x2, o_rw, o_mb, o_sw, gates, w_branch, w_out, g)


def _mem_kv_kernel(m_ref, g_ref, wk_ref, wv_ref, k_ref, vt_ref):
    m = _rms(m_ref[...], g_ref[...]).astype(BF16)
    k_ref[...] = _dot(m, wk_ref[...]).astype(k_ref.dtype)
    vt_ref[...] = _dot(m, wv_ref[...]).T.astype(vt_ref.dtype)


def _mem_kv(mem2, g, wk, wv, M, layer):
    R = mem2.shape[0]
    row = lambda i: (i, 0)
    return pl.pallas_call(
        _mem_kv_kernel,
        out_shape=(jax.ShapeDtypeStruct((R, XA_WIDTH), BF16),
                   jax.ShapeDtypeStruct((R // M, XA_WIDTH, M), BF16)),
        grid=(R // M,),
        in_specs=[pl.BlockSpec((M, D_MODEL), row), _layer_spec((1, D_MODEL), layer),
                  _layer_spec(wk.shape[1:], layer), _layer_spec(wv.shape[1:], layer)],
        out_specs=(pl.BlockSpec((M, XA_WIDTH), row),
                   pl.BlockSpec((None, XA_WIDTH, M), lambda i: (i, 0, 0))),
        compiler_params=_cparams("parallel"),
        name="mem_kv",
    )(mem2, g, wk, wv)


def _xattn_kernel(x_ref, k_ref, vt_ref, gpre_ref, wq_ref, wo_ref, gpost_ref, out_ref):
    tm = x_ref.shape[0]
    groups = [slice(r, r + XATTN_SUB) for r in range(0, tm, XATTN_SUB)]
    heads = [slice(hd * XA_HEAD_DIM, (hd + 1) * XA_HEAD_DIM) for hd in range(XA_HEADS)]
    qs = []
    for rows in groups:
        h = _rms(x_ref[rows, :], gpre_ref[...]).astype(BF16)
        qs.append((_dot(h, wq_ref[...]) * (LOG2E * XA_HEAD_DIM ** -0.5)).astype(BF16))
    scores = [[_dot_nt(k_ref[:, sl], q[:, sl]) for sl in heads] for q in qs]
    ones = jnp.ones((BF16_SUBLANES, vt_ref.shape[1]), BF16)
    vts = [jnp.concatenate([vt_ref[sl, :], ones], axis=0) for sl in heads]
    outs = []
    for grp in scores:
        probs = [jnp.exp2(s - jnp.max(s, axis=0, keepdims=True)).astype(BF16) for s in grp]
        pv = [_dot(vt, p) for vt, p in zip(vts, probs)]
        o_t = jnp.concatenate([o[:XA_HEAD_DIM] / o[XA_HEAD_DIM:XA_HEAD_DIM + 1] for o in pv], axis=0)
        outs.append(o_t.T.astype(BF16))
    ys = [_dot(o, wo_ref[...]) for o in outs]
    for rows, y in zip(groups, ys):
        out_ref[rows, :] = x_ref[rows, :] + _rms(y, gpost_ref[...])


def _xattn(x2, k_mem, vt_mem, g_pre, wq, wo, g_post, B, S, M, layer):
    T = x2.shape[0]
    tm = XATTN_ROWS
    nt = S // tm
    row = lambda b, i: (b * nt + i, 0)
    return pl.pallas_call(
        _xattn_kernel,
        out_shape=jax.ShapeDtypeStruct((T, D_MODEL), F32),
        grid=(B, nt),
        in_specs=[pl.BlockSpec((tm, D_MODEL), row),
                  pl.BlockSpec((M, XA_WIDTH), lambda b, i: (b, 0)),
                  pl.BlockSpec((None, XA_WIDTH, M), lambda b, i: (b, 0, 0)),
                  _layer_spec((1, D_MODEL), layer), _layer_spec(wq.shape[1:], layer),
                  _layer_spec(wo.shape[1:], layer), _layer_spec((1, D_MODEL), layer)],
        out_specs=pl.BlockSpec((tm, D_MODEL), row),
        compiler_params=_cparams("parallel", "parallel"),
        name="xattn",
    )(x2, k_mem, vt_mem, g_pre, wq, wo, g_post)


def _mlp_kernel(x_ref, gpre_ref, wup_ref, wdn_ref, gpost_ref, out_ref):
    n_sub = wup_ref.shape[1] // MLP_SUB
    for r in range(0, x_ref.shape[0], MLP_GROUP):
        rows = slice(r, r + MLP_GROUP)
        x = x_ref[rows, :]
        h = _rms(x, gpre_ref[...]).astype(BF16)
        a_next = _dot(h, wup_ref[:, 0:MLP_SUB])
        acc = None
        for c in range(n_sub):
            a = jnp.maximum(a_next, 0.0)
            if c + 1 < n_sub:
                a_next = _dot(h, wup_ref[:, (c + 1) * MLP_SUB:(c + 2) * MLP_SUB])
            t = _dot((a * a).astype(BF16), wdn_ref[c * MLP_SUB:(c + 1) * MLP_SUB, :])
            acc = t if acc is None else acc + t
        out_ref[rows, :] = x + _rms(acc, gpost_ref[...])


def _mlp(x2, g_pre, w_up, w_down, g_post, layer):
    T = x2.shape[0]
    tm = MLP_ROWS
    row = lambda i: (i, 0)
    return pl.pallas_call(
        _mlp_kernel,
        out_shape=jax.ShapeDtypeStruct((T, D_MODEL), F32),
        grid=(T // tm,),
        in_specs=[pl.BlockSpec((tm, D_MODEL), row), _layer_spec((1, D_MODEL), layer),
                  _layer_spec(w_up.shape[1:], layer), _layer_spec(w_down.shape[1:], layer),
                  _layer_spec((1, D_MODEL), layer)],
        out_specs=pl.BlockSpec((tm, D_MODEL), row),
        compiler_params=_cparams("parallel"),
        name="mlp",
    )(x2, g_pre, w_up, w_down, g_post)


def kernel(x, mem, positions, norm_mix_pre, norm_mix_post, norm_xattn_pre, norm_xattn_post, norm_mem, norm_mlp_pre, norm_mlp_post, w_in, rw_mu, rw_w0, rw_w_up, rw_a0, rw_a_up, rw_g_up, rw_k_k, rw_k_a, rw_r_k, rw_lnx_w, rw_lnx_b, rw_vres_down, rw_vres_mu, rw_v0, rw_vres_up, sw_sinks, w_branch, w_out, w_xq, w_xk, w_xv, w_xo, w_up, w_down):
    B, S, D = x.shape
    M = mem.shape[1]
    depth = w_in.shape[0]
    T = B * S
    x2 = x.reshape(T, D)
    mem2 = mem.reshape(B * M, D)
    cos_t, sin_t = _rope_tables(positions)

    rows = lambda t: t.reshape(t.shape[0], 1, -1)
    bf = lambda t: t.astype(BF16)
    lora_in = rw_w_up.shape[1]
    per_kv = N_HEADS // SW_KV_HEADS
    head_of_col = jnp.arange(MXU_TILE) // HEAD_DIM
    rw_p = dict(
        mu=rows(rw_mu), w0=rows(rw_w0), a0=rows(rw_a0), k_k=rows(rw_k_k), k_a=rows(rw_k_a),
        r_k=rows(rw_r_k), lnx_w=rows(rw_lnx_w), lnx_b=rows(rw_lnx_b),
        w_up=bf(jnp.pad(rw_w_up, ((0, 0), (0, LORA_PAD - lora_in), (0, 0)))),
        a_up=bf(jnp.pad(rw_a_up, ((0, 0), (lora_in, LORA_PAD - lora_in - rw_a_up.shape[1]), (0, 0)))),
        g_up=bf(rw_g_up),
        bd=(head_of_col[:, None] == head_of_col[None, :]).astype(BF16))
    mv_pad = LORA_PAD - rw_vres_down.shape[2]
    w_vm = bf(jnp.pad(rw_vres_down, ((0, 0), (0, 0), (0, mv_pad))))
    vres_p = dict(mu=rows(jnp.pad(rw_vres_mu, ((0, 0), (0, mv_pad)))), v0=rows(rw_v0),
                  up=bf(jnp.pad(rw_vres_up, ((0, 0), (0, mv_pad), (0, 0)))))
    sinks = jnp.broadcast_to(
        sw_sinks.reshape(depth, SW_KV_HEADS, per_kv).transpose(0, 2, 1)[..., None].astype(F32),
        (depth, per_kv, SW_KV_HEADS, LANES))
    wb_sw = w_branch[:, 2].reshape(depth, SW_KV_HEADS, per_kv, HEAD_DIM, D)
    wb_sw = wb_sw.transpose(0, 2, 1, 3, 4).reshape(depth, 1, BRANCH, D)
    wb = bf(jnp.concatenate([w_branch[:, :2], wb_sw], axis=1))
    w_in_b, w_out_b, w_up_b, w_down_b = bf(w_in), bf(w_out), bf(w_up), bf(w_down)
    w_xq_b, w_xk_b, w_xv_b, w_xo_b = bf(w_xq), bf(w_xk), bf(w_xv), bf(w_xo)
    g_mix_pre, g_mix_post = rows(norm_mix_pre), rows(norm_mix_post)
    g_xa_pre, g_xa_post, g_mem = rows(norm_xattn_pre), rows(norm_xattn_post), rows(norm_mem)
    g_mlp_pre, g_mlp_post = rows(norm_mlp_pre), rows(norm_mlp_post)

    v_first = None
    for l in range(depth):
        outs = _mix_in(x2, g_mix_pre, w_in_b, w_vm if l > 0 else None, cos_t, sin_t, l)
        if l > 0:
            u_rw, vm, mq, mk, mv, sq, sk, sv, gates = outs
            o_rw = _rwkv(u_rw, rw_p, l, B, S, dict(vres_p, vm=vm, v_first=v_first))
        else:
            u_rw, mq, mk, mv, sq, sk, sv, gates = outs
            o_rw, v_first = _rwkv(u_rw, rw_p, l, B, S)
        o_mb, o_sw = _attn(mq, mk, mv, sq, sk, sv, sinks, B, S, l)
        x2 = _mix_out(x2, o_rw, o_mb, o_sw, gates, wb, w_out_b, g_mix_post, l)
        k_mem, vt_mem = _mem_kv(mem2, g_mem, w_xk_b, w_xv_b, M, l)
        x2 = _xattn(x2, k_mem, vt_mem, g_xa_pre, w_xq_b, w_xo_b, g_xa_post, B, S, M, l)
        x2 = _mlp(x2, g_mlp_pre, w_up_b, w_down_b, g_mlp_post, l)
    return x2.reshape(B, S, D)
```

```python
import functools

import jax
import jax.numpy as jnp
from jax import lax
from jax.experimental import pallas as pl
from jax.experimental.pallas import tpu as pltpu

F32 = jnp.float32
BF16 = jnp.bfloat16

D_MODEL = 1024
HEAD_DIM = 64
BRANCH = 512
N_HEADS = BRANCH // HEAD_DIM
RW_COLS = 3 * BRANCH + 64 + 64 + 128
LORA_PAD = 128
MB_BLOCK = 256
MB_TOPK = 3
SW_WINDOW = 128
SW_KV_HEADS = 2
XA_HEADS = 4
XA_HEAD_DIM = 128
XA_WIDTH = XA_HEADS * XA_HEAD_DIM
ROPE_THETA = 10000.0
NORM_EPS = 1e-6
RW_LNX_EPS = 1e-5 * HEAD_DIM
MASK_VALUE = -1e30
LOG2E = 1.4426950408889634

LANES = 128
MXU_TILE = 256
BF16_SUBLANES = 16
VMEM_LIMIT = 56 * 1024 * 1024

RW_CHUNK = 64
RW_SUBCHUNKS = 2
RW_BATCH = 4
ATTN_TILES = 2
SWA_GROUP = 4
ROPE_ROWS = 2048
MIX_IN_ROWS = 512
XATTN_ROWS = 1024
XATTN_SUB = 256
MIX_OUT_ROWS = 1024
MIX_OUT_SUB = 512
MLP_ROWS = 1024
MLP_GROUP = 256
MLP_SUB = 512


def _cparams(*sem):
    return pltpu.CompilerParams(dimension_semantics=sem, vmem_limit_bytes=VMEM_LIMIT)


def _dot(a, b):
    return jnp.dot(a, b, preferred_element_type=F32)


def _dot_nt(a, b):
    return lax.dot_general(a, b, (((1,), (1,)), ((), ())), preferred_element_type=F32)


def _dot_tn(a, b):
    return lax.dot_general(a, b, (((0,), (0,)), ((), ())), preferred_element_type=F32)


def _rms(xf, g):
    ms = jnp.mean(xf * xf, axis=-1, keepdims=True)
    return xf * lax.rsqrt(ms + NORM_EPS) * g


def _const_spec(shape):
    nd = len(shape)
    return pl.BlockSpec(shape, lambda *_: (0,) * nd, pipeline_mode=pl.Buffered(1))


def _layer_spec(shape, layer):
    nd = len(shape)
    return pl.BlockSpec((None,) + tuple(shape), lambda *_: (layer,) + (0,) * nd,
                        pipeline_mode=pl.Buffered(1))


def _rope_kernel(pos_ref, invf_ref, cos_ref, sin_ref):
    ang = invf_ref[...] * pos_ref[...].astype(F32)
    c = jnp.cos(ang)
    s = jnp.sin(ang)
    reps = LANES // HEAD_DIM
    cos_ref[...] = jnp.concatenate([c, c] * reps, axis=0).T
    sin_ref[...] = jnp.concatenate([-s, s] * reps, axis=0).T


def _rope_tables(positions):
    T = positions.size
    tm = ROPE_ROWS
    inv_freq = 1.0 / (ROPE_THETA ** (jnp.arange(0, HEAD_DIM, 2, dtype=F32) / HEAD_DIM))
    return pl.pallas_call(
        _rope_kernel,
        out_shape=(jax.ShapeDtypeStruct((T, LANES), F32),) * 2,
        grid=(T // tm,),
        in_specs=[pl.BlockSpec((1, tm), lambda i: (0, i)), _const_spec((HEAD_DIM // 2, 1))],
        out_specs=(pl.BlockSpec((tm, LANES), lambda i: (i, 0)),) * 2,
        compiler_params=_cparams("parallel"),
        name="rope_tables",
    )(positions.reshape(1, T), inv_freq[:, None])


def _rope_tile(t, cos, sin, lane_lt_half):
    fwd = pltpu.roll(t, LANES - HEAD_DIM // 2, axis=1)
    bwd = pltpu.roll(t, HEAD_DIM // 2, axis=1)
    partner = jnp.where(lane_lt_half, fwd, bwd)
    return t * cos + partner * sin


def _mix_in_kernel(has_vm, x_ref, g_ref, w_ref, *rest):
    if has_vm:
        (wvm_ref, cos_ref, sin_ref,
         urw_ref, vm_ref, mq_ref, mk_ref, mv_ref, sq_ref, sk_ref, sv_ref, gate_ref) = rest
    else:
        cos_ref, sin_ref, urw_ref, mq_ref, mk_ref, mv_ref, sq_ref, sk_ref, sv_ref, gate_ref = rest
    h = _rms(x_ref[...], g_ref[...]).astype(BF16)
    cos = cos_ref[...]
    sin = sin_ref[...]
    lane = lax.broadcasted_iota(jnp.int32, cos.shape, 1)
    lt_half = (lane % HEAD_DIM) < (HEAD_DIM // 2)
    first_head = lane < HEAD_DIM
    q_scale = LOG2E * HEAD_DIM ** -0.5

    kv_w = SW_KV_HEADS * HEAD_DIM
    widths = dict(rw=RW_COLS, mq=BRANCH, mk=BRANCH, mv=BRANCH, sq=BRANCH, sk=kv_w, sv=kv_w,
                  gate=3 * D_MODEL)
    start, col = {}, 0
    for name, width in widths.items():
        start[name] = col
        col += width

    def seg(name, offset=0, width=None):
        a = start[name] + offset
        return _dot(h, w_ref[:, a:a + (widths[name] if width is None else width)])

    def rope_chunks(name, scale):
        acc = seg(name)
        return [_rope_tile(acc[:, c * LANES:(c + 1) * LANES], cos, sin, lt_half) * scale
                for c in range(widths[name] // LANES)]

    def store_chunks(ref, chunks):
        for c, t in enumerate(chunks):
            ref[:, c * LANES:(c + 1) * LANES] = t.astype(ref.dtype)

    for n in range(3):
        gate = jax.nn.sigmoid(seg("gate", n * D_MODEL, D_MODEL))
        gate_ref[:, n * D_MODEL:(n + 1) * D_MODEL] = gate.astype(gate_ref.dtype)
    store_chunks(mq_ref, rope_chunks("mq", q_scale))
    store_chunks(mk_ref, rope_chunks("mk", 1.0))
    nat = rope_chunks("sq", q_scale)
    half_tiles = len(nat) // 2
    paired = []
    for p in range(len(nat)):
        a, b = nat[p // 2], nat[p // 2 + half_tiles]
        if p % 2 == 0:
            paired.append(jnp.where(first_head, a, pltpu.roll(b, HEAD_DIM, axis=1)))
        else:
            paired.append(jnp.where(first_head, pltpu.roll(a, HEAD_DIM, axis=1), b))
    store_chunks(sq_ref, paired)
    store_chunks(sk_ref, rope_chunks("sk", 1.0))
    urw_ref[...] = seg("rw").astype(urw_ref.dtype)
    mv_ref[...] = seg("mv").astype(mv_ref.dtype)
    if has_vm:
        vm_ref[...] = _dot(h, wvm_ref[...])
    sv_ref[...] = seg("sv").astype(sv_ref.dtype)


def _mix_in(x2, g, w_in, w_vm, cos_t, sin_t, layer):
    T = x2.shape[0]
    tm = MIX_IN_ROWS
    has_vm = w_vm is not None
    kv_w = SW_KV_HEADS * HEAD_DIM
    widths = [(RW_COLS, BF16)]
    if has_vm:
        widths.append((LORA_PAD, F32))
    widths += [(BRANCH, BF16), (BRANCH, BF16), (BRANCH, BF16), (BRANCH, BF16),
               (kv_w, BF16), (kv_w, BF16), (3 * D_MODEL, BF16)]
    assert sum(w for w, _ in widths) - (LORA_PAD if has_vm else 0) == w_in.shape[2]
    row = lambda i: (i, 0)
    in_specs = [pl.BlockSpec((tm, D_MODEL), row), _layer_spec((1, D_MODEL), layer),
                _layer_spec(w_in.shape[1:], layer)]
    args = [x2, g, w_in]
    if has_vm:
        in_specs.append(_layer_spec(w_vm.shape[1:], layer - 1))
        args.append(w_vm)
    in_specs += [pl.BlockSpec((tm, LANES), row), pl.BlockSpec((tm, LANES), row)]
    args += [cos_t, sin_t]
    return pl.pallas_call(
        functools.partial(_mix_in_kernel, has_vm),
        out_shape=tuple(jax.ShapeDtypeStruct((T, w), dt) for w, dt in widths),
        grid=(T // tm,),
        in_specs=in_specs,
        out_specs=tuple(pl.BlockSpec((tm, w), row) for w, _ in widths),
        compiler_params=_cparams("parallel"),
        name="mix_in",
    )(*args)


def _shift_prev(cur, carry_ref, bi):
    rolled = pltpu.roll(cur, 1, axis=0)
    row = lax.broadcasted_iota(jnp.int32, cur.shape, 0)
    prev = jnp.where(row == 0, carry_ref[bi, 0:1, :], rolled)
    carry_ref[bi, 0:1, :] = cur[cur.shape[0] - 1:, :]
    return prev


def _split_bf16(x, terms):
    pieces = []
    for _ in range(terms):
        hi = x.astype(BF16)
        pieces.append(hi)
        x = x - hi.astype(F32)
    return pieces


def _head_sums(x, bd, terms):
    w = bd.shape[0]
    halves = []
    for c in range(x.shape[1] // w):
        pieces = _split_bf16(x[:, c * w:(c + 1) * w], terms)
        halves.append(sum(_dot(piece, bd) for piece in pieces))
    return jnp.concatenate(halves, axis=1)


def _dot_split_rhs(a, b, terms):
    return sum(_dot(a, piece) for piece in _split_bf16(b, terms))


def _rwkv_kernel(has_vres, nb, *refs):
    if has_vres:
        (u_ref, vm_ref, vf_ref, mu_ref, w0_ref, wup_ref, a0_ref, aup_ref, gup_ref, kk_ref,
         ka_ref, rk_ref, lw_ref, lb_ref, bd_ref, vmu_ref, v0_ref, vup_ref,
         o_ref, s_ref, cu_ref, cvm_ref, hout_ref) = refs
        vout_ref = None
    else:
        (u_ref, mu_ref, w0_ref, wup_ref, a0_ref, aup_ref, gup_ref, kk_ref,
         ka_ref, rk_ref, lw_ref, lb_ref, bd_ref,
         o_ref, vout_ref, s_ref, cu_ref, hout_ref) = refs
    C = RW_CHUNK
    P2 = 2 * C
    npair = BRANCH // LANES
    n_sub = u_ref.shape[1] // C

    @pl.when(pl.program_id(1) == 0)
    def _():
        s_ref[...] = jnp.zeros_like(s_ref)
        cu_ref[...] = jnp.zeros_like(cu_ref)
        if has_vres:
            cvm_ref[...] = jnp.zeros_like(cvm_ref)

    bd = bd_ref[...]
    ti = lax.broadcasted_iota(jnp.int32, (C, C), 0)
    si = lax.broadcasted_iota(jnp.int32, (C, C), 1)
    tri = (si <= ti).astype(BF16)
    lane = lax.broadcasted_iota(jnp.int32, (1, LANES), 1)
    first_head = lane < HEAD_DIM
    pr = lax.broadcasted_iota(jnp.int32, (P2, P2), 0)
    pc = lax.broadcasted_iota(jnp.int32, (P2, P2), 1)
    same_head = (pr // C) == (pc // C)
    strict = same_head & ((pc % C) < (pr % C))
    incl = same_head & ((pc % C) <= (pr % C))
    eye = (pr == pc).astype(F32)

    def stack(t):
        t = t.astype(BF16)
        zero = jnp.zeros_like(t)
        return jnp.concatenate([jnp.where(first_head, t, zero), jnp.where(first_head, zero, t)], axis=0)

    pre = []
    inst = []
    for bi in range(nb):
        u = u_ref[bi].astype(F32)
        ul = u + mu_ref[...] * (_shift_prev(u, cu_ref, bi) - u)
        r = ul[:, 0:BRANCH]
        k = ul[:, BRANCH:2 * BRANCH]
        v = ul[:, 2 * BRANCH:3 * BRANCH]
        x_wa = ul[:, 3 * BRANCH:3 * BRANCH + LANES]
        x_g = ul[:, 3 * BRANCH + LANES:]

        z = w0_ref[...] + _dot(jnp.tanh(x_wa).astype(BF16), wup_ref[...])
        a = jax.nn.sigmoid(a0_ref[...] + _dot(x_wa.astype(BF16), aup_ref[...]))
        g = _dot(jax.nn.sigmoid(x_g).astype(BF16), gup_ref[...])
        softplus = jnp.maximum(-z, 0.0) + jnp.log1p(jnp.exp(-jnp.abs(z)))
        ell = -jnp.exp(-softplus - 0.5)

        if has_vres:
            vm = vm_ref[bi]
            vml = vm + vmu_ref[...] * (_shift_prev(vm, cvm_ref, bi) - vm)
            vgate = jax.nn.sigmoid(v0_ref[...] + _dot(vml.astype(BF16), vup_ref[...]))
            v = v + (vf_ref[bi] - v) * vgate
        else:
            vout_ref[bi] = v

        kk = k * kk_ref[...]
        kk = kk / jnp.maximum(jnp.sqrt(_head_sums(kk * kk, bd, 1)), 1e-12)
        k_eff = k * (1.0 + (a - 1.0) * ka_ref[...])
        b = kk * a
        pre.append((r, k_eff, v, g))
        for ci in range(n_sub):
            rs = slice(ci * C, (ci + 1) * C)
            ell_c = ell[rs]
            cum = _dot_split_rhs(tri, ell_c, 3)
            cum_last = cum[C - 1:, :]
            g_inv = jnp.exp(-cum)
            g_tail = jnp.exp(cum_last - cum)
            kap_t = kk[rs] * jnp.exp(cum - ell_c)
            r_t = r[rs] * jnp.exp(cum)
            b_t = b[rs] * g_inv
            k_t = k_eff[rs] * g_inv
            b_g = b[rs] * g_tail
            k_g = k_eff[rs] * g_tail
            g_all = jnp.exp(cum_last)
            v_c = v[rs]
            for p in range(npair):
                sl = slice(p * LANES, (p + 1) * LANES)
                inst.append(dict(bi=bi, ci=ci, rs=rs, p=p, sl=sl,
                                 kap=stack(kap_t[:, sl]), r=stack(r_t[:, sl]), b=stack(b_t[:, sl]),
                                 k=stack(k_t[:, sl]), v=stack(v_c[:, sl]), bg=stack(b_g[:, sl]),
                                 kg=stack(k_g[:, sl]), gall=g_all[:, sl]))

    incl2 = jnp.concatenate([incl, incl], axis=1)
    for d in inst:
        aa = _dot_nt(jnp.concatenate([d["kap"], d["r"]], axis=0),
                     jnp.concatenate([d["b"], d["k"]], axis=0))
        n = jnp.where(strict, -aa[:P2, :P2], 0.0)
        d["n"] = n.astype(BF16)
        d["t"] = eye + n
        d["a_ak"] = jnp.where(strict, aa[:P2, P2:], 0.0).astype(BF16)
        d["a_r"] = jnp.where(incl2, aa[P2:, :], 0.0).astype(BF16)
    for d in inst:
        d["akv"] = _dot(d["a_ak"], d["v"]).astype(BF16)
    for d in inst:
        d["pw"] = _dot(d["n"], d["n"]).astype(BF16)
    for _ in range(4):
        for d in inst:
            pt = _dot(d["pw"], jnp.concatenate([d["pw"], d["t"].astype(BF16)], axis=1))
            d["pw"] = pt[:, :P2].astype(BF16)
            d["t"] = d["t"] + pt[:, P2:]
    for d in inst:
        d["t"] = d["t"] + _dot(d["pw"], d["t"].astype(BF16))
    for d in inst:
        ty = _dot(d["t"].astype(BF16), jnp.concatenate([d["kap"], d["akv"]], axis=1))
        d["w_til"], d["u_til"] = ty[:, :LANES].astype(BF16), ty[:, LANES:]
    for ci in range(n_sub):
        chunk = [d for d in inst if d["ci"] == ci]
        for d in chunk:
            d["state"] = s_ref[d["bi"] * npair + d["p"]]
            xr = _dot_nt(jnp.concatenate([d["w_til"], d["r"]], axis=0), d["state"].astype(BF16))
            sa = -(xr[:P2] + d["u_til"])
            d["sv"] = jnp.concatenate([sa.astype(BF16), d["v"]], axis=0)
            d["rs_out"] = xr[P2:]
        for d in chunk:
            s_ref[d["bi"] * npair + d["p"]] = (
                d["state"] * d["gall"]
                + _dot_tn(d["sv"], jnp.concatenate([d["bg"], d["kg"]], axis=0)))
        for d in chunk:
            o_st = d["rs_out"] + _dot(d["a_r"], d["sv"])
            hout_ref[d["bi"], d["rs"], d["sl"]] = o_st[:C] + o_st[C:]

    inv_n = 1.0 / HEAD_DIM
    for bi in range(nb):
        r, k_eff, v, g = pre[bi]
        out = hout_ref[bi]
        mean = _head_sums(out, bd, 2) * inv_n
        cen = out - mean
        var = _head_sums(cen * cen, bd, 1) * inv_n
        normed = cen * lax.rsqrt(var + RW_LNX_EPS) * lw_ref[...] + lb_ref[...]
        bonus = _head_sums(r * k_eff * rk_ref[...], bd, 2) * v
        o_ref[bi] = ((normed + bonus) * g).astype(o_ref.dtype)


def _rwkv(u_rw, p, layer, B, S, vres=None):
    T = u_rw.shape[0]
    C = RW_CHUNK
    nb = RW_BATCH
    rows = RW_SUBCHUNKS * C
    nc = S // rows
    has_vres = vres is not None
    blk = lambda w: pl.BlockSpec((nb, rows, w), lambda b, c: (b, c, 0))
    names = ("mu", "w0", "w_up", "a0", "a_up", "g_up", "k_k", "k_a", "r_k", "lnx_w", "lnx_b")
    in_specs = [blk(RW_COLS)]
    args = [u_rw.reshape(B, S, RW_COLS)]
    if has_vres:
        in_specs += [blk(LORA_PAD), blk(BRANCH)]
        args += [vres["vm"].reshape(B, S, LORA_PAD), vres["v_first"].reshape(B, S, BRANCH)]
    in_specs += [_layer_spec(p[n].shape[1:], layer) for n in names]
    in_specs.append(_const_spec((MXU_TILE, MXU_TILE)))
    args += [p[n] for n in names] + [p["bd"]]
    scratch = [pltpu.VMEM((nb * (BRANCH // LANES), 2 * C, LANES), F32),
               pltpu.VMEM((nb, 8, RW_COLS), F32)]
    if has_vres:
        vnames = ("mu", "v0", "up")
        in_specs += [_layer_spec(vres[n].shape[1:], layer - 1) for n in vnames]
        args += [vres[n] for n in vnames]
        out_shape = jax.ShapeDtypeStruct((B, S, BRANCH), BF16)
        out_specs = blk(BRANCH)
        scratch.append(pltpu.VMEM((nb, 8, LORA_PAD), F32))
    else:
        out_shape = (jax.ShapeDtypeStruct((B, S, BRANCH), BF16),
                     jax.ShapeDtypeStruct((B, S, BRANCH), F32))
        out_specs = (blk(BRANCH), blk(BRANCH))
    scratch.append(pltpu.VMEM((nb, rows, BRANCH), F32))
    outs = pl.pallas_call(
        functools.partial(_rwkv_kernel, has_vres, nb),
        out_shape=out_shape,
        grid=(B // nb, nc),
        in_specs=in_specs,
        out_specs=out_specs,
        scratch_shapes=scratch,
        compiler_params=_cparams("parallel", "arbitrary"),
        name="rwkv",
    )(*args)
    if has_vres:
        return outs.reshape(T, BRANCH)
    return outs[0].reshape(T, BRANCH), outs[1].reshape(T, BRANCH)


def _moba_steps(q_ref, k_ref, v_ref, o_ref, kmean_ref, vt_ref):
    S = q_ref.shape[0]
    nb = S // MB_BLOCK
    tq = MB_BLOCK
    pair = LANES // HEAD_DIM
    tiles = [slice(t * LANES, (t + 1) * LANES) for t in range(q_ref.shape[1] // LANES)]
    heads = [(t, hh) for t in range(len(tiles)) for hh in range(pair)]
    for n in range(nb):
        blk = k_ref[n * MB_BLOCK:(n + 1) * MB_BLOCK, :].astype(F32)
        kmean_ref[n:n + 1, :] = jnp.mean(blk, axis=0, keepdims=True)
    for t, cols in enumerate(tiles):
        vt_ref[t, 0:LANES, :] = v_ref[:, cols].astype(F32).T.astype(BF16)
        vt_ref[t, LANES:, :] = jnp.ones((vt_ref.shape[1] - LANES, S), BF16)
    lane = lax.broadcasted_iota(jnp.int32, (1, LANES), 1)
    head_lanes = [(lane // HEAD_DIM) == hh for hh in range(pair)]

    def scores(i):
        out = []
        for t, hh in heads:
            q_t = q_ref[i * tq:(i + 1) * tq, tiles[t]]
            k_all = k_ref[0:(i + 1) * MB_BLOCK, tiles[t]]
            out.append(_dot_nt(k_all, jnp.where(head_lanes[hh], q_t, jnp.zeros_like(q_t))))
        return out

    state = {"s": scores(0)}

    blk_id = lax.broadcasted_iota(jnp.int32, (nb, S), 0)
    own = lax.broadcasted_iota(jnp.int32, (nb, S), 1) // MB_BLOCK
    valid = blk_id < own
    sels = []

    def select_blocks():
        for t, hh in heads:
            kmean = kmean_ref[:, tiles[t]]
            kmean_rep = jnp.concatenate(
                [kmean] + [jnp.broadcast_to(kmean[m:m + 1, :], (nb, LANES)) for m in range(nb - 1)],
                axis=0)
            pieces = _split_bf16(jnp.where(head_lanes[hh], kmean_rep, 0.0), 3)
            gates = sum(_dot_nt(piece, q_ref[:, tiles[t]]) for piece in pieces)
            gate = gates[:nb]
            cnt = jnp.zeros((nb, S), jnp.int32)
            for m in range(nb - 1):
                gm = gates[nb * (m + 1):nb * (m + 2)]
                beats = (own > m) & ((gm > gate) | ((gm == gate) & (m < blk_id)))
                cnt = cnt + beats.astype(jnp.int32)
            sels.append((valid & (cnt < MB_TOPK)).astype(F32))

    ki = lax.broadcasted_iota(jnp.int32, (tq, tq), 0)
    qi = lax.broadcasted_iota(jnp.int32, (tq, tq), 1)
    causal = ki <= qi

    def tile(i):
        s_cur = state["s"]
        if i + 1 < nb:
            state["s"] = scores(i + 1)
        if i == 1:
            select_blocks()
        probs = []
        for hh in range(len(heads)):
            s = s_cur[hh]
            s_own = jnp.where(causal, s[i * MB_BLOCK:, :], MASK_VALUE)
            m_col = jnp.max(s_own, axis=0, keepdims=True)
            keeps = []
            for j in range(i):
                keep = sels[hh][j:j + 1, i * tq:(i + 1) * tq] > 0.5
                blk_max = jnp.max(s[j * MB_BLOCK:(j + 1) * MB_BLOCK, :], axis=0, keepdims=True)
                m_col = jnp.maximum(m_col, jnp.where(keep, blk_max, MASK_VALUE))
                keeps.append(keep)
            parts = [jnp.exp2(s[j * MB_BLOCK:(j + 1) * MB_BLOCK, :] - jnp.where(keeps[j], m_col, -MASK_VALUE))
                     for j in range(i)]
            parts.append(jnp.exp2(s_own - m_col))
            p = parts[0] if len(parts) == 1 else jnp.concatenate(parts, axis=0)
            probs.append(p.astype(BF16))
        ot = [_dot(vt_ref[t, :, 0:(i + 1) * MB_BLOCK], p) for (t, _), p in zip(heads, probs)]
        ot = [o[:LANES] / o[LANES:LANES + 1] for o in ot]
        for t, cols in enumerate(tiles):
            o_t = jnp.concatenate([ot[pair * t][:HEAD_DIM], ot[pair * t + 1][HEAD_DIM:]], axis=0).T
            o_ref[i * tq:(i + 1) * tq, cols] = o_t.astype(o_ref.dtype)

    return [functools.partial(tile, i) for i in range(nb)]


def _swa_steps(q_ref, k_ref, v_ref, sink_ref, o_ref, vt_ref):
    S = q_ref.shape[0]
    W = SW_WINDOW
    pair = LANES // HEAD_DIM
    tiles = [slice(t * LANES, (t + 1) * LANES) for t in range(q_ref.shape[1] // LANES)]
    vt_ref[0:LANES, :] = v_ref[...].astype(F32).T.astype(BF16)
    vt_ref[LANES:, :] = jnp.ones((vt_ref.shape[0] - LANES, S), BF16)
    lane = lax.broadcasted_iota(jnp.int32, (1, LANES), 1)
    dist = (lax.broadcasted_iota(jnp.int32, (2 * W, W), 1) + W
            - lax.broadcasted_iota(jnp.int32, (2 * W, W), 0))
    band = (dist >= 0) & (dist < W)
    causal = (lax.broadcasted_iota(jnp.int32, (W, W), 0)
              <= lax.broadcasted_iota(jnp.int32, (W, W), 1))
    head_lanes = [(lane // HEAD_DIM) == hh for hh in range(pair)]
    sinks = [[sink_ref[t, hh:hh + 1, 0:1] * LOG2E for hh in range(pair)]
             for t in range(len(tiles))]
    group = SWA_GROUP

    def scores(g0):
        work = []
        for n in range(g0, g0 + group):
            k0 = max(n - 1, 0) * W
            k1 = (n + 1) * W
            mask = causal if n == 0 else band
            k_w = k_ref[k0:k1, :]
            for t, cols in enumerate(tiles):
                q_t = q_ref[n * W:(n + 1) * W, cols]
                for hh in range(pair):
                    q_h = jnp.where(head_lanes[hh], q_t, jnp.zeros_like(q_t))
                    work.append(dict(n=n, t=t, hh=hh, k0=k0, k1=k1, mask=mask, s=_dot_nt(k_w, q_h)))
        return work

    state = {"work": scores(0)}

    def run_group(g0):
        work = state["work"]
        if g0 + group < S // W:
            state["work"] = scores(g0 + group)
        for d in work:
            s = jnp.where(d["mask"], d["s"], MASK_VALUE)
            sink = sinks[d["t"]][d["hh"]]
            m_col = jnp.maximum(jnp.max(s, axis=0, keepdims=True), sink)
            d["p"] = jnp.exp2(s - m_col).astype(BF16)
            d["sink_term"] = jnp.exp2(sink - m_col)
        for d in work:
            ot = _dot(vt_ref[:, d["k0"]:d["k1"]], d["p"])
            d["ot"] = ot[:LANES] / (ot[LANES:LANES + 1] + d["sink_term"])
        for t, cols in enumerate(tiles):
            mine = [d for d in work if d["t"] == t]
            o_t = jnp.concatenate(
                [jnp.concatenate([a["ot"][:HEAD_DIM], b["ot"][HEAD_DIM:]], axis=0)
                 for a, b in zip(mine[0::2], mine[1::2])], axis=1)
            o_ref[g0 * W:(g0 + group) * W, cols] = o_t.T.astype(o_ref.dtype)

    return [functools.partial(run_group, g0) for g0 in range(0, S // W, group)]


def _attn_kernel(mq_ref, mk_ref, mv_ref, sq_ref, sk_ref, sv_ref, sink_ref, omb_ref, osw_ref,
                 kmean_ref, mvt_ref, svt_ref):
    moba = _moba_steps(mq_ref, mk_ref, mv_ref, omb_ref, kmean_ref, mvt_ref)
    swa = _swa_steps(sq_ref, sk_ref, sv_ref, sink_ref, osw_ref, svt_ref)
    per = len(moba) // len(swa)
    for g, swa_group in enumerate(swa):
        for moba_tile in moba[g * per:(g + 1) * per]:
            moba_tile()
        swa_group()


def _attn(mq, mk, mv, sq, sk, sv, sinks, B, S, layer):
    T = mq.shape[0]
    nt = ATTN_TILES
    width = nt * LANES
    spec = pl.BlockSpec((S, width), lambda b, p: (b, p))
    kv_spec = pl.BlockSpec((S, LANES), lambda b, p: (b, 0))
    return pl.pallas_call(
        _attn_kernel,
        out_shape=(jax.ShapeDtypeStruct((T, BRANCH), BF16),) * 2,
        grid=(B, BRANCH // width),
        in_specs=[spec, spec, spec, spec, kv_spec, kv_spec,
                  pl.BlockSpec((None, nt, 2, LANES), lambda b, p: (layer, p, 0, 0))],
        out_specs=(spec, spec),
        scratch_shapes=[pltpu.VMEM((S // MB_BLOCK, width), F32),
                        pltpu.VMEM((nt, LANES + BF16_SUBLANES, S), BF16),
                        pltpu.VMEM((LANES + BF16_SUBLANES, S), BF16)],
        compiler_params=_cparams("parallel", "parallel"),
        name="attn",
    )(mq, mk, mv, sq, sk, sv, sinks)


def _mix_out_kernel(x_ref, orw_ref, omb_ref, osw_ref, gate_ref, wb_ref, wo_ref, g_ref, out_ref):
    tm = x_ref.shape[0]
    subs = [slice(r, r + MIX_OUT_SUB) for r in range(0, tm, MIX_OUT_SUB)]

    def gated(rows):
        y = None
        for n, o_ref in enumerate((orw_ref, omb_ref, osw_ref)):
            t = gate_ref[rows, n * D_MODEL:(n + 1) * D_MODEL] * _dot(o_ref[rows, :], wb_ref[n])
            y = t if y is None else y + t
        return y.astype(BF16)

    y_next = gated(subs[0])
    for i, rows in enumerate(subs):
        y = y_next
        if i + 1 < len(subs):
            y_next = gated(subs[i + 1])
        out_ref[rows, :] = x_ref[rows, :] + _rms(_dot(y, wo_ref[...]), g_ref[...])


def _mix_out(x2, o_rw, o_mb, o_sw, gates, w_branch, w_out, g, layer):
    T = x2.shape[0]
    tm = MIX_OUT_ROWS
    row = lambda i: (i, 0)
    br = pl.BlockSpec((tm, BRANCH), row)
    return pl.pallas_call(
        _mix_out_kernel,
        out_shape=jax.ShapeDtypeStruct((T, D_MODEL), F32),
        grid=(T // tm,),
        in_specs=[pl.BlockSpec((tm, D_MODEL), row), br, br, br,
                  pl.BlockSpec((tm, 3 * D_MODEL), row),
                  _layer_spec(w_branch.shape[1:], layer), _layer_spec(w_out.shape[1:], layer),
                  _layer_spec((1, D_MODEL), layer)],
        out_specs=pl.BlockSpec((tm, D_MODEL), row),
        compiler_params=_cparams("parallel"),
        name="mix_out",
    )(x2, o_rw, o_mb, o_sw, gates, w_branch, w_out, g)


def _mem_kv_kernel(m_ref, g_ref, wk_ref, wv_ref, k_ref, vt_ref):
    m = _rms(m_ref[...], g_ref[...]).astype(BF16)
    k_ref[...] = _dot(m, wk_ref[...]).astype(k_ref.dtype)
    vt_ref[...] = _dot(m, wv_ref[...]).T.astype(vt_ref.dtype)


def _mem_kv(mem2, g, wk, wv, M, layer):
    R = mem2.shape[0]
    row = lambda i: (i, 0)
    return pl.pallas_call(
        _mem_kv_kernel,
        out_shape=(jax.ShapeDtypeStruct((R, XA_WIDTH), BF16),
                   jax.ShapeDtypeStruct((R // M, XA_WIDTH, M), BF16)),
        grid=(R // M,),
        in_specs=[pl.BlockSpec((M, D_MODEL), row), _layer_spec((1, D_MODEL), layer),
                  _layer_spec(wk.shape[1:], layer), _layer_spec(wv.shape[1:], layer)],
        out_specs=(pl.BlockSpec((M, XA_WIDTH), row),
                   pl.BlockSpec((None, XA_WIDTH, M), lambda i: (i, 0, 0))),
        compiler_params=_cparams("parallel"),
        name="mem_kv",
    )(mem2, g, wk, wv)


def _xattn_kernel(x_ref, k_ref, vt_ref, gpre_ref, wq_ref, wo_ref, gpost_ref, out_ref):
    tm = x_ref.shape[0]
    groups = [slice(r, r + XATTN_SUB) for r in range(0, tm, XATTN_SUB)]
    heads = [slice(hd * XA_HEAD_DIM, (hd + 1) * XA_HEAD_DIM) for hd in range(XA_HEADS)]
    qs = []
    for rows in groups:
        h = _rms(x_ref[rows, :], gpre_ref[...]).astype(BF16)
        qs.append((_dot(h, wq_ref[...]) * (LOG2E * XA_HEAD_DIM ** -0.5)).astype(BF16))
    scores = [[_dot_nt(k_ref[:, sl], q[:, sl]) for sl in heads] for q in qs]
    ones = jnp.ones((BF16_SUBLANES, vt_ref.shape[1]), BF16)
    vts = [jnp.concatenate([vt_ref[sl, :], ones], axis=0) for sl in heads]
    outs = []
    for grp in scores:
        probs = [jnp.exp2(s - jnp.max(s, axis=0, keepdims=True)).astype(BF16) for s in grp]
        pv = [_dot(vt, p) for vt, p in zip(vts, probs)]
        o_t = jnp.concatenate([o[:XA_HEAD_DIM] / o[XA_HEAD_DIM:XA_HEAD_DIM + 1] for o in pv], axis=0)
        outs.append(o_t.T.astype(BF16))
    ys = [_dot(o, wo_ref[...]) for o in outs]
    for rows, y in zip(groups, ys):
        out_ref[rows, :] = x_ref[rows, :] + _rms(y, gpost_ref[...])


def _xattn(x2, k_mem, vt_mem, g_pre, wq, wo, g_post, B, S, M, layer):
    T = x2.shape[0]
    tm = XATTN_ROWS
    nt = S // tm
    row = lambda b, i: (b * nt + i, 0)
    return pl.pallas_call(
        _xattn_kernel,
        out_shape=jax.ShapeDtypeStruct((T, D_MODEL), F32),
        grid=(B, nt),
        in_specs=[pl.BlockSpec((tm, D_MODEL), row),
                  pl.BlockSpec((M, XA_WIDTH), lambda b, i: (b, 0)),
                  pl.BlockSpec((None, XA_WIDTH, M), lambda b, i: (b, 0, 0)),
                  _layer_spec((1, D_MODEL), layer), _layer_spec(wq.shape[1:], layer),
                  _layer_spec(wo.shape[1:], layer), _layer_spec((1, D_MODEL), layer)],
        out_specs=pl.BlockSpec((tm, D_MODEL), row),
        compiler_params=_cparams("parallel", "parallel"),
        name="xattn",
    )(x2, k_mem, vt_mem, g_pre, wq, wo, g_post)


def _mlp_kernel(x_ref, gpre_ref, wup_ref, wdn_ref, gpost_ref, out_ref):
    n_sub = wup_ref.shape[1] // MLP_SUB
    for r in range(0, x_ref.shape[0], MLP_GROUP):
        rows = slice(r, r + MLP_GROUP)
        x = x_ref[rows, :]
        h = _rms(x, gpre_ref[...]).astype(BF16)
        a_next = _dot(h, wup_ref[:, 0:MLP_SUB])
        acc = None
        for c in range(n_sub):
            a = jnp.maximum(a_next, 0.0)
            if c + 1 < n_sub:
                a_next = _dot(h, wup_ref[:, (c + 1) * MLP_SUB:(c + 2) * MLP_SUB])
            t = _dot((a * a).astype(BF16), wdn_ref[c * MLP_SUB:(c + 1) * MLP_SUB, :])
            acc = t if acc is None else acc + t
        out_ref[rows, :] = x + _rms(acc, gpost_ref[...])


def _mlp(x2, g_pre, w_up, w_down, g_post, layer):
    T = x2.shape[0]
    tm = MLP_ROWS
    row = lambda i: (i, 0)
    return pl.pallas_call(
        _mlp_kernel,
        out_shape=jax.ShapeDtypeStruct((T, D_MODEL), F32),
        grid=(T // tm,),
        in_specs=[pl.BlockSpec((tm, D_MODEL), row), _layer_spec((1, D_MODEL), layer),
                  _layer_spec(w_up.shape[1:], layer), _layer_spec(w_down.shape[1:], layer),
                  _layer_spec((1, D_MODEL), layer)],
        out_specs=pl.BlockSpec((tm, D_MODEL), row),
        compiler_params=_cparams("parallel"),
        name="mlp",
    )(x2, g_pre, w_up, w_down, g_post)


def kernel(x, mem, positions, norm_mix_pre, norm_mix_post, norm_xattn_pre, norm_xattn_post, norm_mem, norm_mlp_pre, norm_mlp_post, w_in, rw_mu, rw_w0, rw_w_up, rw_a0, rw_a_up, rw_g_up, rw_k_k, rw_k_a, rw_r_k, rw_lnx_w, rw_lnx_b, rw_vres_down, rw_vres_mu, rw_v0, rw_vres_up, sw_sinks, w_branch, w_out, w_xq, w_xk, w_xv, w_xo, w_up, w_down):
    B, S, D = x.shape
    M = mem.shape[1]
    depth = w_in.shape[0]
    T = B * S
    x2 = x.reshape(T, D)
    mem2 = mem.reshape(B * M, D)
    cos_t, sin_t = _rope_tables(positions)

    rows = lambda t: t.reshape(t.shape[0], 1, -1)
    bf = lambda t: t.astype(BF16)
    lora_in = rw_w_up.shape[1]
    per_kv = N_HEADS // SW_KV_HEADS
    head_of_col = jnp.arange(MXU_TILE) // HEAD_DIM
    rw_p = dict(
        mu=rows(rw_mu), w0=rows(rw_w0), a0=rows(rw_a0), k_k=rows(rw_k_k), k_a=rows(rw_k_a),
        r_k=rows(rw_r_k), lnx_w=rows(rw_lnx_w), lnx_b=rows(rw_lnx_b),
        w_up=bf(jnp.pad(rw_w_up, ((0, 0), (0, LORA_PAD - lora_in), (0, 0)))),
        a_up=bf(jnp.pad(rw_a_up, ((0, 0), (lora_in, LORA_PAD - lora_in - rw_a_up.shape[1]), (0, 0)))),
        g_up=bf(rw_g_up),
        bd=(head_of_col[:, None] == head_of_col[None, :]).astype(BF16))
    mv_pad = LORA_PAD - rw_vres_down.shape[2]
    w_vm = bf(jnp.pad(rw_vres_down, ((0, 0), (0, 0), (0, mv_pad))))
    vres_p = dict(mu=rows(jnp.pad(rw_vres_mu, ((0, 0), (0, mv_pad)))), v0=rows(rw_v0),
                  up=bf(jnp.pad(rw_vres_up, ((0, 0), (0, mv_pad), (0, 0)))))
    sinks = jnp.broadcast_to(
        sw_sinks.reshape(depth, SW_KV_HEADS, per_kv).transpose(0, 2, 1)[..., None].astype(F32),
        (depth, per_kv, SW_KV_HEADS, LANES))
    wb_sw = w_branch[:, 2].reshape(depth, SW_KV_HEADS, per_kv, HEAD_DIM, D)
    wb_sw = wb_sw.transpose(0, 2, 1, 3, 4).reshape(depth, 1, BRANCH, D)
    wb = bf(jnp.concatenate([w_branch[:, :2], wb_sw], axis=1))
    w_in_b, w_out_b, w_up_b, w_down_b = bf(w_in), bf(w_out), bf(w_up), bf(w_down)
    w_xq_b, w_xk_b, w_xv_b, w_xo_b = bf(w_xq), bf(w_xk), bf(w_xv), bf(w_xo)
    g_mix_pre, g_mix_post = rows(norm_mix_pre), rows(norm_mix_post)
    g_xa_pre, g_xa_post, g_mem = rows(norm_xattn_pre), rows(norm_xattn_post), rows(norm_mem)
    g_mlp_pre, g_mlp_post = rows(norm_mlp_pre), rows(norm_mlp_post)

    v_first = None
    for l in range(depth):
        outs = _mix_in(x2, g_mix_pre, w_in_b, w_vm if l > 0 else None, cos_t, sin_t, l)
        if l > 0:
            u_rw, vm, mq, mk, mv, sq, sk, sv, gates = outs
            o_rw = _rwkv(u_rw, rw_p, l, B, S, dict(vres_p, vm=vm, v_first=v_first))
        else:
            u_rw, mq, mk, mv, sq, sk, sv, gates = outs
            o_rw, v_first = _rwkv(u_rw, rw_p, l, B, S)
        o_mb, o_sw = _attn(mq, mk, mv, sq, sk, sv, sinks, B, S, l)
        x2 = _mix_out(x2, o_rw, o_mb, o_sw, gates, wb, w_out_b, g_mix_post, l)
        k_mem, vt_mem = _mem_kv(mem2, g_mem, w_xk_b, w_xv_b, M, l)
        x2 = _xattn(x2, k_mem, vt_mem, g_xa_pre, w_xq_b, w_xo_b, g_xa_post, B, S, M, l)
        x2 = _mlp(x2, g_mlp_pre, w_up_b, w_down_b, g_mlp_post, l)
    return x2.reshape(B, S, D)
```
